```python
import jax, jax.numpy as jnp
from jax import lax
import numpy as np

D_MODEL = 1024
BATCH = 2
SEQ = 8192
DEPTH = 1

D_MIX = D_MODEL
RET_HEADS = 4
RET_WIDTH = D_MIX // 2
RET_HEAD_DIM = RET_WIDTH // RET_HEADS
RET_CHUNK = 128
ROPE_BASE = 10000.0
SGU_GROUPS = 4
SGU_WIDTH = D_MIX - RET_WIDTH
SGU_GROUP_DIM = SGU_WIDTH // SGU_GROUPS
SGU_CHUNK = 128
IN_COLS = 4 * RET_WIDTH + 2 * SGU_WIDTH
N_EXPERT_GROUPS = 4
EXPERTS_PER_GROUP = 8
TOP_K = 2
EXPERT_FF = D_MODEL // 4
ROUTER_BIAS_SCALE = 0.01
EPS = 1e-6

kernel_name = 'hybrid_retention_sgu_hmoe_block'


def rmsnorm(x, gain):
    x32 = x.astype(jnp.float32)
    y = x32 * lax.rsqrt(jnp.mean(x32 * x32, axis=-1, keepdims=True) + EPS)
    return (y * gain.astype(jnp.float32)).astype(x.dtype)


def rotary(x, positions):
    d = x.shape[-1]
    half = d // 2
    inv_freq = ROPE_BASE ** (-jnp.arange(half, dtype=jnp.float32) * 2.0 / d)
    ang = positions.astype(jnp.float32)[:, :, None, None] * inv_freq
    cos, sin = jnp.cos(ang), jnp.sin(ang)
    x32 = x.astype(jnp.float32)
    x1, x2 = x32[..., :half], x32[..., half:]
    return jnp.concatenate([x1 * cos - x2 * sin, x1 * sin + x2 * cos], axis=-1)


def retention(q, k, v, gate, positions, gn_gain):
    out_dtype = gate.dtype
    B, S, _ = q.shape
    H, dh, C = RET_HEADS, RET_HEAD_DIM, RET_CHUNK
    N = S // C
    q = rotary(q.reshape(B, S, H, dh), positions)
    k = rotary(k.reshape(B, S, H, dh), positions) * (dh ** -0.5)
    v = v.reshape(B, S, H, dh).astype(jnp.float32)
    log_gamma = jnp.log(1.0 - 2.0 ** (-5.0 - jnp.arange(H, dtype=jnp.float32)))
    idx = jnp.arange(C, dtype=jnp.float32)
    diff = idx[:, None] - idx[None, :]
    decay_intra = jnp.where(diff >= 0.0,
                            jnp.exp(log_gamma[:, None, None] * jnp.maximum(diff, 0.0)), 0.0)
    qc = q.reshape(B, N, C, H, dh)
    kc = k.reshape(B, N, C, H, dh)
    vc = v.reshape(B, N, C, H, dh)
    scores = jnp.einsum('bnihd,bnjhd->bnhij', qc, kc) * decay_intra
    intra = jnp.einsum('bnhij,bnjhe->bnihe', scores, vc)
    k_decay = jnp.exp(log_gamma[:, None] * (C - 1.0 - idx))
    kv = jnp.einsum('bnjhd,hj,bnjhe->nbhde', kc, k_decay, vc)
    chunk_decay = jnp.exp(log_gamma * C)[None, :, None, None]

    def step(state, kv_n):
        return state * chunk_decay + kv_n, state

    _, state_prev = lax.scan(step, jnp.zeros((B, H, dh, dh), jnp.float32), kv)
    q_decay = jnp.exp(log_gamma[:, None] * (idx + 1.0))
    cross = jnp.einsum('bnihd,nbhde,hi->bnihe', qc, state_prev, q_decay)
    o = (intra + cross).reshape(B, S, H, dh)
    mu = jnp.mean(o, axis=-1, keepdims=True)
    var = jnp.mean(jnp.square(o - mu), axis=-1, keepdims=True)
    o = ((o - mu) * lax.rsqrt(var + EPS)).reshape(B, S, RET_WIDTH) * gn_gain.astype(jnp.float32)
    return (jax.nn.silu(gate.astype(jnp.float32)) * o).astype(out_dtype)


def spatial_gating(u, v, w_s, b_s, norm_gain):
    B, S, _ = u.shape
    G, dg, C = SGU_GROUPS, SGU_GROUP_DIM, SGU_CHUNK
    N = S // C
    v32 = v.astype(jnp.float32).reshape(B, N, C, G, dg)
    v32 = v32 * lax.rsqrt(jnp.mean(v32 * v32, axis=-1, keepdims=True) + EPS)
    v32 = v32 * norm_gain.astype(jnp.float32).reshape(G, dg)
    causal = jnp.tril(jnp.ones((C, C), jnp.float32))
    w = w_s.astype(jnp.float32) * causal
    s = jnp.einsum('gij,bnjgc->bnigc', w, v32) + b_s.astype(jnp.float32).T[None, None, :, :, None]
    out = u.astype(jnp.float32).reshape(B, N, C, G, dg) * s
    return out.reshape(B, S, SGU_WIDTH).astype(u.dtype)


def hierarchical_moe(h, w_rg, b_rg, w_re, b_re, w_gate_up, w_up, w_down):
    B, S, D = h.shape
    T = B * S
    G, E = N_EXPERT_GROUPS, EXPERTS_PER_GROUP
    t = h.reshape(T, D)
    t32 = t.astype(jnp.float32)
    group_logits = t32 @ w_rg.astype(jnp.float32) + b_rg.astype(jnp.float32)
    group_prob = jax.nn.softmax(group_logits, axis=-1)
    gsel = jax.nn.one_hot(jnp.argmax(group_logits, axis=-1), G, dtype=jnp.float32)
    gweight = jnp.sum(group_prob * gsel, axis=-1)
    exp_logits = jnp.einsum('td,gde->tge', t32, w_re.astype(jnp.float32)) + b_re.astype(jnp.float32)
    sel_logits = jnp.sum(exp_logits * gsel[:, :, None], axis=1)
    top_v, top_i = lax.top_k(sel_logits, TOP_K)
    top_w = jax.nn.softmax(top_v, axis=-1)
    ew = jnp.sum(jax.nn.one_hot(top_i, E, dtype=jnp.float32) * top_w[..., None], axis=1)
    gates = (gweight[:, None] * gsel)[:, :, None] * ew[:, None, :]
    y = jnp.zeros((T, D), jnp.float32)
    for g in range(G):
        a = jnp.einsum('td,edf->tef', t, w_gate_up[g])
        b = jnp.einsum('td,edf->tef', t, w_up[g])
        act = jax.nn.silu(a) * b * gates[:, g, :, None].astype(a.dtype)
        y = y + jnp.einsum('tef,efd->td', act, w_down[g]).astype(jnp.float32)
    return y.reshape(B, S, D).astype(h.dtype)


def setup_inputs(seed: int = 0) -> dict:
    key = jax.random.key(seed)
    ks = jax.random.split(key, 17)
    G, E, F = N_EXPERT_GROUPS, EXPERTS_PER_GROUP, EXPERT_FF
    nrm = jax.random.normal
    return {
        'x': nrm(ks[0], (BATCH, SEQ, D_MODEL), jnp.float32),
        'positions': jnp.broadcast_to(jnp.arange(SEQ, dtype=jnp.int32), (BATCH, SEQ)),
        'norm_mix_gain': 1.0 + 0.02 * nrm(ks[1], (DEPTH, D_MODEL), jnp.float32),
        'w_in': nrm(ks[2], (DEPTH, D_MODEL, IN_COLS), jnp.float32) * D_MODEL ** -0.5,
        'ret_gn_gain': 1.0 + 0.02 * nrm(ks[3], (DEPTH, RET_WIDTH), jnp.float32),
        'sgu_norm_gain': 1.0 + 0.02 * nrm(ks[4], (DEPTH, SGU_WIDTH), jnp.float32),
        'sgu_w': nrm(ks[5], (DEPTH, SGU_GROUPS, SGU_CHUNK, SGU_CHUNK), jnp.float32) * SGU_CHUNK ** -0.5,
        'sgu_b': 1.0 + 0.1 * nrm(ks[6], (DEPTH, SGU_GROUPS, SGU_CHUNK), jnp.float32),
        'w_out': nrm(ks[7], (DEPTH, D_MIX, D_MODEL), jnp.float32) * D_MIX ** -0.5,
        'norm_ffn_gain': 1.0 + 0.02 * nrm(ks[8], (DEPTH, D_MODEL), jnp.float32),
        'w_router_group': nrm(ks[9], (DEPTH, D_MODEL, G), jnp.float32) * D_MODEL ** -0.5,
        'b_router_group': ROUTER_BIAS_SCALE * nrm(ks[10], (DEPTH, G), jnp.float32),
        'w_router_expert': nrm(ks[11], (DEPTH, G, D_MODEL, E), jnp.float32) * D_MODEL ** -0.5,
        'b_router_expert': ROUTER_BIAS_SCALE * nrm(ks[12], (DEPTH, G, E), jnp.float32),
        'w_expert_gate_up': nrm(ks[13], (DEPTH, G, E, D_MODEL, F), jnp.float32) * D_MODEL ** -0.5,
        'w_expert_up': nrm(ks[14], (DEPTH, G, E, D_MODEL, F), jnp.float32) * D_MODEL ** -0.5,
        'w_expert_down': nrm(ks[15], (DEPTH, G, E, F, D_MODEL), jnp.float32) * F ** -0.5,
        'final_norm_gain': 1.0 + 0.02 * nrm(ks[16], (D_MODEL,), jnp.float32),
    }


def reference(x, positions, norm_mix_gain, w_in, ret_gn_gain, sgu_norm_gain, sgu_w, sgu_b, w_out,
              norm_ffn_gain, w_router_group, b_router_group, w_router_expert, b_router_expert,
              w_expert_gate_up, w_expert_up, w_expert_down, final_norm_gain):
    R, U = RET_WIDTH, SGU_WIDTH
    splits = [R, 2 * R, 3 * R, 4 * R, 4 * R + U]
    for l in range(DEPTH):
        h = rmsnorm(x, norm_mix_gain[l])
        proj = h @ w_in[l]
        q, k, v, gate, u, vs = jnp.split(proj, splits, axis=-1)
        ret = retention(q, k, v, gate, positions, ret_gn_gain[l])
        sgu = spatial_gating(jax.nn.gelu(u, approximate=False), jax.nn.gelu(vs, approximate=False),
                             sgu_w[l], sgu_b[l], sgu_norm_gain[l])
        x = x + jnp.concatenate([ret, sgu], axis=-1) @ w_out[l]
        x = x + hierarchical_moe(rmsnorm(x, norm_ffn_gain[l]), w_router_group[l], b_router_group[l],
                                 w_router_expert[l], b_router_expert[l], w_expert_gate_up[l],
                                 w_expert_up[l], w_expert_down[l])
    return rmsnorm(x, final_norm_gain)
```

```python
import functools
import math

import jax
import jax.numpy as jnp
from jax import lax
from jax.experimental import pallas as pl
from jax.experimental.pallas import tpu as pltpu

F32 = jnp.float32
BF16 = jnp.bfloat16
I32 = jnp.int32

RET_HEADS = 4
HEAD_DIM = 128
CHUNK = 128
SGU_GROUPS = 4
GROUP_DIM = 128
RET_WIDTH = RET_HEADS * HEAD_DIM
SGU_WIDTH = SGU_GROUPS * GROUP_DIM
N_SECTIONS = 6
N_GROUPS = 4
N_PER_GROUP = 8
N_EXPERTS = N_GROUPS * N_PER_GROUP
ROUTER_ROWS = 40
ROPE_BASE = 10000.0
EPS = 1e-6

LANES = 128
VMEM_LIMIT_BYTES = 56 * 1024 * 1024

ATTN_TILE = 512
RANK_TILE = 512
MOVE_TILE = 256
EXPERT_TILE = 256

_NT = (((1,), (1,)), ((), ()))
_TN = (((0,), (0,)), ((), ()))

_LOG_GAMMA = [math.log(1.0 - 2.0 ** (-5.0 - h)) for h in range(RET_HEADS)]
_CHUNK_DECAY = [math.exp(lg * CHUNK) for lg in _LOG_GAMMA]
_K_SCALE = HEAD_DIM ** -0.5


def _rms(x, gain):
    return x * lax.rsqrt(jnp.mean(x * x, axis=-1, keepdims=True) + EPS) * gain


def _gelu(x):
    return 0.5 * x * (1.0 + lax.erf(x * 0.7071067811865476))


def _sigmoid(x):
    return 1.0 / (1.0 + jnp.exp(-x))


def _attn_router_kernel(x_ref, pos_ref, gmix_ref, win_ref, gn_ref, sgn_ref, sw_ref, sb_ref, wout_ref,
                        invf_ref, gffn_ref, wrh_ref, wrl_ref, br_ref,
                        x1_ref, h2_ref, ids_ref, gcol_ref,
                        proj_ref, mix_ref, state_ref, dintra_ref, qdec_ref, kdec_ref, wc_ref, btab_ref):
    ts = x_ref.shape[0]
    b = pl.program_id(0)
    s = pl.program_id(1)

    @pl.when((b == 0) & (s == 0))
    def _init_tables():
        i = lax.broadcasted_iota(I32, (CHUNK, CHUNK), 0)
        j = lax.broadcasted_iota(I32, (CHUNK, CHUNK), 1)
        diff = (i - j).astype(F32)
        fi = i.astype(F32)
        for h in range(RET_HEADS):
            lg = _LOG_GAMMA[h]
            dintra_ref[h] = jnp.where(i >= j, jnp.exp(lg * diff), 0.0) * _K_SCALE
            qdec_ref[h] = jnp.exp(lg * (fi + 1.0))
            kdec_ref[h] = jnp.exp(lg * (CHUNK - 1.0 - fi)) * _K_SCALE
        for g in range(SGU_GROUPS):
            wc_ref[g] = jnp.where(i >= j, sw_ref[g], 0.0).astype(BF16)
            btab_ref[g] = jnp.broadcast_to(sb_ref[g], (CHUNK, GROUP_DIM))

    @pl.when(s == 0)
    def _reset_state():
        state_ref[...] = jnp.zeros_like(state_ref)

    x = x_ref[...]
    h = _rms(x, gmix_ref[...]).astype(BF16)
    for sec in range(N_SECTIONS):
        proj_ref[sec] = jnp.dot(h, win_ref[:, sec * RET_WIDTH:(sec + 1) * RET_WIDTH],
                                preferred_element_type=F32)

    def chunk_body(c, carry):
        rows = pl.ds(pl.multiple_of(c * CHUNK, CHUNK), CHUNK)
        ang_t = invf_ref[...] * pos_ref[c].astype(F32)
        cos_t = jnp.cos(ang_t)
        sin_t = jnp.sin(ang_t)
        cosf = jnp.concatenate([cos_t, cos_t], axis=0).T
        sinf = jnp.concatenate([-sin_t, sin_t], axis=0).T
        for hd in range(RET_HEADS):
            hs = slice(hd * HEAD_DIM, (hd + 1) * HEAD_DIM)
            q = proj_ref[0, rows, hs]
            k = proj_ref[1, rows, hs]
            vb = proj_ref[2, rows, hs].astype(BF16)
            gate = proj_ref[3, rows, hs]
            qr = q * cosf + pltpu.roll(q, HEAD_DIM // 2, 1) * sinf
            kr = k * cosf + pltpu.roll(k, HEAD_DIM // 2, 1) * sinf
            scores = lax.dot_general(qr.astype(BF16), kr.astype(BF16), _NT,
                                     preferred_element_type=F32) * dintra_ref[hd]
            st = state_ref[hd]
            o = (jnp.dot(scores.astype(BF16), vb, preferred_element_type=F32)
                 + jnp.dot((qr * qdec_ref[hd]).astype(BF16), st.astype(BF16), preferred_element_type=F32))
            kv = lax.dot_general((kr * kdec_ref[hd]).astype(BF16), vb, _TN, preferred_element_type=F32)
            state_ref[hd] = st * _CHUNK_DECAY[hd] + kv
            d = o - jnp.mean(o, axis=-1, keepdims=True)
            on = d * lax.rsqrt(jnp.mean(d * d, axis=-1, keepdims=True) + EPS) * gn_ref[:, hs]
            mix_ref[rows, hs] = (gate * _sigmoid(gate) * on).astype(BF16)
        for g in range(SGU_GROUPS):
            gs = slice(g * GROUP_DIM, (g + 1) * GROUP_DIM)
            u = _gelu(proj_ref[4, rows, gs])
            v = _gelu(proj_ref[5, rows, gs])
            vn = v * lax.rsqrt(jnp.mean(v * v, axis=-1, keepdims=True) + EPS) * sgn_ref[:, gs]
            sg = jnp.dot(wc_ref[g], vn.astype(BF16), preferred_element_type=F32) + btab_ref[g]
            mix_ref[rows, pl.ds(RET_WIDTH + g * GROUP_DIM, GROUP_DIM)] = (u * sg).astype(BF16)
        return carry

    lax.fori_loop(0, ts // CHUNK, chunk_body, 0)

    x1 = x_ref[...] + jnp.dot(mix_ref[...], wout_ref[...], preferred_element_type=F32)
    x1_ref[...] = x1
    h2 = _rms(x1, gffn_ref[...])
    h2_ref[...] = h2

    h2h = h2.astype(BF16)
    h2l = (h2 - h2h.astype(F32)).astype(BF16)
    wrh = wrh_ref[...]
    logits = (lax.dot_general(wrh, h2h, _NT, preferred_element_type=F32)
              + lax.dot_general(wrh, h2l, _NT, preferred_element_type=F32)
              + lax.dot_general(wrl_ref[...], h2h, _NT, preferred_element_type=F32)
              + br_ref[...])
    el = logits[0:N_EXPERTS]
    gl = logits[N_EXPERTS:N_EXPERTS + N_GROUPS]
    rg = lax.broadcasted_iota(I32, (N_GROUPS, ts), 0).astype(F32)
    gmax = jnp.max(gl, axis=0, keepdims=True)
    gidx = jnp.min(jnp.where(gl == gmax, rg, float(N_GROUPS)), axis=0, keepdims=True)
    gweight = 1.0 / jnp.sum(jnp.exp(gl - gmax), axis=0, keepdims=True)
    re_i = lax.broadcasted_iota(I32, (N_EXPERTS, ts), 0)
    re = re_i.astype(F32)
    in_group = (re_i // N_PER_GROUP).astype(F32) == gidx
    neg = -jnp.inf
    sel = jnp.where(in_group, el, neg)
    m1 = jnp.max(sel, axis=0, keepdims=True)
    i1 = jnp.min(jnp.where(sel == m1, re, float(N_EXPERTS)), axis=0, keepdims=True)
    sel2 = jnp.where(re == i1, neg, sel)
    m2 = jnp.max(sel2, axis=0, keepdims=True)
    i2 = jnp.min(jnp.where(sel2 == m2, re, float(N_EXPERTS)), axis=0, keepdims=True)
    e2 = jnp.exp(m2 - m1)
    w1 = 1.0 / (1.0 + e2)
    ids_ref[...] = jnp.concatenate([i1, i2], axis=0).astype(I32)
    gates_t = jnp.concatenate([gweight * w1, gweight * (e2 * w1), jnp.zeros((LANES - 2, ts), F32)], axis=0)
    gcol_ref[...] = gates_t.T


def _attn_router(x, pos4, gmix, win, gn, sgn, sw, sb, wout, invf, gffn, wrh, wrl, br):
    bsz, seq, d = x.shape
    ts = ATTN_TILE
    ns = seq // ts
    t = bsz * seq
    const2 = lambda b, s: (0, 0)
    const3 = lambda b, s: (0, 0, 0)
    single = dict(pipeline_mode=pl.Buffered(1))
    in_specs = [
        pl.BlockSpec((None, ts, d), lambda b, s: (b, s, 0)),
        pl.BlockSpec((None, ts // CHUNK, 1, CHUNK), lambda b, s: (b, s, 0, 0)),
        pl.BlockSpec((1, d), const2),
        pl.BlockSpec(win.shape, const2, **single),
        pl.BlockSpec((1, RET_WIDTH), const2),
        pl.BlockSpec((1, SGU_WIDTH), const2),
        pl.BlockSpec(sw.shape, const3),
        pl.BlockSpec(sb.shape, const3),
        pl.BlockSpec(wout.shape, const2, **single),
        pl.BlockSpec(invf.shape, const2),
        pl.BlockSpec((1, d), const2),
        pl.BlockSpec(wrh.shape, const2),
        pl.BlockSpec(wrl.shape, const2),
        pl.BlockSpec(br.shape, const2),
    ]
    tok = lambda b, s: (b * ns + s, 0)
    out_specs = [
        pl.BlockSpec((ts, d), tok),
        pl.BlockSpec((ts, d), tok),
        pl.BlockSpec((2, ts), lambda b, s: (0, b * ns + s)),
        pl.BlockSpec((ts, LANES), tok),
    ]
    out_shape = [
        jax.ShapeDtypeStruct((t, d), F32),
        jax.ShapeDtypeStruct((t, d), F32),
        jax.ShapeDtypeStruct((2, t), I32),
        jax.ShapeDtypeStruct((t, LANES), F32),
    ]
    scratch = [
        pltpu.VMEM((N_SECTIONS, ts, RET_WIDTH), F32),
        pltpu.VMEM((ts, RET_WIDTH + SGU_WIDTH), BF16),
        pltpu.VMEM((RET_HEADS, HEAD_DIM, HEAD_DIM), F32),
        pltpu.VMEM((RET_HEADS, CHUNK, CHUNK), F32),
        pltpu.VMEM((RET_HEADS, CHUNK, HEAD_DIM), F32),
        pltpu.VMEM((RET_HEADS, CHUNK, HEAD_DIM), F32),
        pltpu.VMEM((SGU_GROUPS, CHUNK, CHUNK), BF16),
        pltpu.VMEM((SGU_GROUPS, CHUNK, GROUP_DIM), F32),
    ]
    return pl.pallas_call(
        _attn_router_kernel,
        grid=(bsz, ns),
        in_specs=in_specs,
        out_specs=out_specs,
        out_shape=out_shape,
        scratch_shapes=scratch,
        compiler_params=pltpu.CompilerParams(
            dimension_semantics=("arbitrary", "arbitrary"), vmem_limit_bytes=VMEM_LIMIT_BYTES),
        name="attn_router",
    )(x, pos4, gmix, win, gn, sgn, sw, sb, wout, invf, gffn, wrh, wrl, br)


def _rank_kernel(ids_ref, pos_ref, cnt_ref, run_ref):
    phase = pl.program_id(0)
    i = pl.program_id(1)
    tr = ids_ref.shape[1]
    ids = ids_ref[...]
    re = lax.broadcasted_iota(I32, (N_EXPERTS, tr), 0)
    hit1 = re == ids[0:1]
    hit2 = re == ids[1:2]
    onehot = jnp.where(hit1 | hit2, 1.0, 0.0)
    tile_cnt = jnp.sum(onehot, axis=1, keepdims=True)

    @pl.when((phase == 0) & (i == 0))
    def _zero():
        cnt_ref[...] = jnp.zeros_like(cnt_ref)

    @pl.when(phase == 0)
    def _count():
        cnt_ref[...] += jnp.broadcast_to(tile_cnt, cnt_ref.shape)

    @pl.when((phase == 1) & (i == 0))
    def _offsets():
        c = cnt_ref[...]
        c_hi = jnp.floor(c * (1.0 / 256.0))
        c_lo = c - 256.0 * c_hi
        a = lax.broadcasted_iota(I32, (N_EXPERTS, N_EXPERTS), 0)
        bb = lax.broadcasted_iota(I32, (N_EXPERTS, N_EXPERTS), 1)
        lower = jnp.where(bb < a, 1.0, 0.0).astype(BF16)
        run_ref[...] = (256.0 * jnp.dot(lower, c_hi.astype(BF16), preferred_element_type=F32)
                        + jnp.dot(lower, c_lo.astype(BF16), preferred_element_type=F32))

    @pl.when(phase == 1)
    def _rank():
        a = lax.broadcasted_iota(I32, (tr, tr), 0)
        bb = lax.broadcasted_iota(I32, (tr, tr), 1)
        before = jnp.where(a < bb, 1.0, 0.0).astype(BF16)
        prefix = jnp.dot(onehot.astype(BF16), before, preferred_element_type=F32)
        posmat = run_ref[:, 0:1] + prefix
        p1 = jnp.sum(jnp.where(hit1, posmat, 0.0), axis=0, keepdims=True)
        p2 = jnp.sum(jnp.where(hit2, posmat, 0.0), axis=0, keepdims=True)
        pos_ref[...] = jnp.concatenate([p1, p2], axis=0).astype(I32)
        run_ref[...] += jnp.broadcast_to(tile_cnt, run_ref.shape)


def _rank(ids):
    t = ids.shape[1]
    tr = RANK_TILE
    return pl.pallas_call(
        _rank_kernel,
        grid=(2, t // tr),
        in_specs=[pl.BlockSpec((2, tr), lambda p, i: (0, i))],
        out_specs=[pl.BlockSpec((2, tr), lambda p, i: (0, i * p)),
                   pl.BlockSpec((N_EXPERTS, LANES), lambda p, i: (0, 0))],
        out_shape=[jax.ShapeDtypeStruct((2, t), I32), jax.ShapeDtypeStruct((N_EXPERTS, LANES), F32)],
        scratch_shapes=[pltpu.VMEM((N_EXPERTS, LANES), F32)],
        compiler_params=pltpu.CompilerParams(dimension_semantics=("arbitrary", "arbitrary")),
        name="rank",
    )(ids)


def _scatter_kernel(p1_ref, p2_ref, h2_ref, sorted_ref, sem):
    tm = h2_ref.shape[0]

    def copy(t, dst_row):
        return pltpu.make_async_copy(h2_ref.at[pl.ds(t, 1)], sorted_ref.at[pl.ds(dst_row, 1)], sem)

    def start(t, carry):
        copy(t, p1_ref[t]).start()
        copy(t, p2_ref[t]).start()
        return carry

    def wait(t, carry):
        copy(t, p1_ref[t]).wait()
        copy(t, p2_ref[t]).wait()
        return carry

    lax.fori_loop(0, tm, start, 0)
    lax.fori_loop(0, tm, wait, 0)


def _scatter(p1, p2, h2):
    t, d = h2.shape
    tm = MOVE_TILE
    smem = lambda: pl.BlockSpec((tm,), lambda i: (i,), memory_space=pltpu.SMEM)
    return pl.pallas_call(
        _scatter_kernel,
        grid=(t // tm,),
        in_specs=[smem(), smem(), pl.BlockSpec((tm, d), lambda i: (i, 0))],
        out_specs=pl.BlockSpec(memory_space=pl.ANY),
        out_shape=jax.ShapeDtypeStruct((2 * t, d), F32),
        scratch_shapes=[pltpu.SemaphoreType.DMA],
        compiler_params=pltpu.CompilerParams(dimension_semantics=("arbitrary",)),
        name="scatter_rows",
    )(p1, p2, h2)


def _expert_kernel(ptile_ref, pexp_ref, plo_ref, phi_ref, np_ref, xs_ref, wg_ref, wu_ref, wd_ref, o_ref):
    i = pl.program_id(0)
    tm = xs_ref.shape[0]

    @pl.when(i < np_ref[0])
    def _():
        xt = xs_ref[...].astype(BF16)
        a = jnp.dot(xt, wg_ref[...], preferred_element_type=F32)
        bb = jnp.dot(xt, wu_ref[...], preferred_element_type=F32)
        row = lax.broadcasted_iota(I32, (tm, 1), 0)
        mine = (row >= plo_ref[i]) & (row < phi_ref[i])
        act = jnp.where(mine, a * _sigmoid(a) * bb, 0.0)
        o = jnp.dot(act.astype(BF16), wd_ref[...], preferred_element_type=F32)
        first = (i == 0) | (ptile_ref[i] != ptile_ref[jnp.maximum(i - 1, 0)])

        @pl.when(first)
        def _():
            o_ref[...] = o

        @pl.when(jnp.logical_not(first))
        def _():
            o_ref[...] += o


def _experts(ptile, pexp, plo, phi, npairs, xs, wg, wu, wd):
    r, d = xs.shape
    f = wg.shape[-1]
    tm = EXPERT_TILE
    n_steps = ptile.shape[0]
    grid_spec = pltpu.PrefetchScalarGridSpec(
        num_scalar_prefetch=5,
        grid=(n_steps,),
        in_specs=[
            pl.BlockSpec((tm, d), lambda i, pt, pe, lo, hi, n: (pt[i], 0)),
            pl.BlockSpec((None, d, f), lambda i, pt, pe, lo, hi, n: (pe[i], 0, 0)),
            pl.BlockSpec((None, d, f), lambda i, pt, pe, lo, hi, n: (pe[i], 0, 0)),
            pl.BlockSpec((None, f, d), lambda i, pt, pe, lo, hi, n: (pe[i], 0, 0)),
        ],
        out_specs=pl.BlockSpec((tm, d), lambda i, pt, pe, lo, hi, n: (pt[i], 0)),
    )
    return pl.pallas_call(
        _expert_kernel,
        grid_spec=grid_spec,
        out_shape=jax.ShapeDtypeStruct((r, d), F32),
        compiler_params=pltpu.CompilerParams(
            dimension_semantics=("arbitrary",), vmem_limit_bytes=VMEM_LIMIT_BYTES),
        name="experts",
    )(ptile, pexp, plo, phi, npairs, xs, wg, wu, wd)


def _pair_table(cnt, n_tiles):
    tm = EXPERT_TILE
    n_steps = n_tiles + N_EXPERTS - 1
    offs = jnp.cumsum(cnt) - cnt
    first = offs // tm
    last = jnp.where(cnt > 0, (offs + cnt - 1) // tm, first - 1)
    n_e = last - first + 1
    ends = jnp.cumsum(n_e)
    total = ends[-1]
    p = jnp.minimum(jnp.arange(n_steps, dtype=I32), total - 1)
    e = jnp.sum((p[:, None] >= ends[None, :]).astype(I32), axis=1)
    tile = first[e] + (p - (ends[e] - n_e[e]))
    lo = jnp.clip(offs[e] - tile * tm, 0, tm)
    hi = jnp.clip(offs[e] + cnt[e] - tile * tm, 0, tm)
    return tile.astype(I32), e.astype(I32), lo.astype(I32), hi.astype(I32), total.reshape(1).astype(I32)


def _combine_kernel(p1_ref, p2_ref, gcol_ref, x1_ref, gfin_ref, os_ref, out_ref, buf_ref, sem):
    tm = x1_ref.shape[0]

    def copy(t, slot, src_row):
        return pltpu.make_async_copy(os_ref.at[pl.ds(src_row, 1)], buf_ref.at[slot, pl.ds(t, 1)], sem)

    def start(t, carry):
        copy(t, 0, p1_ref[t]).start()
        copy(t, 1, p2_ref[t]).start()
        return carry

    def wait(t, carry):
        copy(t, 0, p1_ref[t]).wait()
        copy(t, 1, p2_ref[t]).wait()
        return carry

    lax.fori_loop(0, tm, start, 0)
    lax.fori_loop(0, tm, wait, 0)
    g = gcol_ref[...]
    y = g[:, 0:1] * buf_ref[0] + g[:, 1:2] * buf_ref[1]
    out_ref[...] = _rms(x1_ref[...] + y, gfin_ref[...])


def _combine(p1, p2, gcol, x1, gfin, osorted):
    t, d = x1.shape
    tm = MOVE_TILE
    smem = lambda: pl.BlockSpec((tm,), lambda i: (i,), memory_space=pltpu.SMEM)
    return pl.pallas_call(
        _combine_kernel,
        grid=(t // tm,),
        in_specs=[smem(), smem(),
                  pl.BlockSpec((tm, LANES), lambda i: (i, 0)),
                  pl.BlockSpec((tm, d), lambda i: (i, 0)),
                  pl.BlockSpec((1, d), lambda i: (0, 0)),
                  pl.BlockSpec(memory_space=pl.ANY)],
        out_specs=pl.BlockSpec((tm, d), lambda i: (i, 0)),
        out_shape=jax.ShapeDtypeStruct((t, d), F32),
        scratch_shapes=[pltpu.VMEM((2, tm, d), F32), pltpu.SemaphoreType.DMA],
        compiler_params=pltpu.CompilerParams(dimension_semantics=("arbitrary",)),
        name="combine",
    )(p1, p2, gcol, x1, gfin, osorted)


def kernel(x, positions, norm_mix_gain, w_in, ret_gn_gain, sgu_norm_gain, sgu_w, sgu_b, w_out, norm_ffn_gain,
           w_router_group, b_router_group, w_router_expert, b_router_expert, w_expert_gate_up, w_expert_up,
           w_expert_down, final_norm_gain):
    bsz, seq, d = x.shape
    assert w_in.shape[0] == 1, "single-layer block"
    assert seq % ATTN_TILE == 0 and (bsz * seq) % RANK_TILE == 0 and (2 * bsz * seq) % EXPERT_TILE == 0
    t = bsz * seq
    half = HEAD_DIM // 2

    pos4 = positions.reshape(bsz, seq // CHUNK, 1, CHUNK)
    invf = (ROPE_BASE ** (-jnp.arange(half, dtype=F32) * 2.0 / HEAD_DIM)).reshape(half, 1)
    wr = jnp.concatenate([
        jnp.transpose(w_router_expert[0], (0, 2, 1)).reshape(N_EXPERTS, d),
        w_router_group[0].T,
        jnp.zeros((ROUTER_ROWS - N_EXPERTS - N_GROUPS, d), F32)], axis=0)
    wrh = wr.astype(BF16)
    wrl = (wr - wrh.astype(F32)).astype(BF16)
    br = jnp.concatenate([b_router_expert[0].reshape(N_EXPERTS), b_router_group[0],
                          jnp.zeros((ROUTER_ROWS - N_EXPERTS - N_GROUPS,), F32)]).reshape(ROUTER_ROWS, 1)

    x1, h2, ids, gcol = _attn_router(
        x, pos4, norm_mix_gain[0].reshape(1, d), w_in[0].astype(BF16), ret_gn_gain[0].reshape(1, RET_WIDTH),
        sgu_norm_gain[0].reshape(1, SGU_WIDTH), sgu_w[0], sgu_b[0].reshape(SGU_GROUPS, CHUNK, 1),
        w_out[0].astype(BF16), invf, norm_ffn_gain[0].reshape(1, d), wrh, wrl, br)

    pos, cnt = _rank(ids)
    p1, p2 = pos[0], pos[1]
    xs = _scatter(p1, p2, h2)

    ff = w_expert_gate_up.shape[-1]
    ptile, pexp, plo, phi, npairs = _pair_table(cnt[:, 0].astype(I32), (2 * t) // EXPERT_TILE)
    osorted = _experts(ptile, pexp, plo, phi, npairs, xs,
                       w_expert_gate_up[0].reshape(N_EXPERTS, d, ff).astype(BF16),
                       w_expert_up[0].reshape(N_EXPERTS, d, ff).astype(BF16),
                       w_expert_down[0].reshape(N_EXPERTS, ff, d).astype(BF16))

    out = _combine(p1, p2, gcol, x1, final_norm_gain.reshape(1, d), osorted)
    return out.reshape(bsz, seq, d)
```

```python
import functools
import math

import jax
import jax.numpy as jnp
from jax import lax
from jax.experimental import pallas as pl
from jax.experimental.pallas import tpu as pltpu

F32 = jnp.float32
BF16 = jnp.bfloat16
I32 = jnp.int32

RET_HEADS = 4
HEAD_DIM = 128
CHUNK = 128
SGU_GROUPS = 4
GROUP_DIM = 128
RET_WIDTH = RET_HEADS * HEAD_DIM
SGU_WIDTH = SGU_GROUPS * GROUP_DIM
N_SECTIONS = 6
N_GROUPS = 4
N_PER_GROUP = 8
N_EXPERTS = N_GROUPS * N_PER_GROUP
ROUTER_ROWS = 40
ROPE_BASE = 10000.0
EPS = 1e-6

LANES = 128
VMEM_LIMIT_BYTES = 56 * 1024 * 1024

ATTN_TILE = 512
RANK_TILE = 512
INVERT_TILE = 1024
MOVE_TILE = 256
EXPERT_TILE = 256

_NT = (((1,), (1,)), ((), ()))
_TN = (((0,), (0,)), ((), ()))

_LOG_GAMMA = [math.log(1.0 - 2.0 ** (-5.0 - h)) for h in range(RET_HEADS)]
_CHUNK_DECAY = [math.exp(lg * CHUNK) for lg in _LOG_GAMMA]
_K_SCALE = HEAD_DIM ** -0.5


def _rms(x, gain):
    return x * lax.rsqrt(jnp.mean(x * x, axis=-1, keepdims=True) + EPS) * gain


def _gelu(x):
    return 0.5 * x * (1.0 + lax.erf(x * 0.7071067811865476))


def _sigmoid(x):
    return 1.0 / (1.0 + jnp.exp(-x))


def _attn_router_kernel(x_ref, pos_ref, gmix_ref, win_ref, gn_ref, sgn_ref, sw_ref, sb_ref, wout_ref,
                        invf_ref, gffn_ref, wrh_ref, wrl_ref, br_ref,
                        x1_ref, h2_ref, ids_ref, gcol_ref,
                        proj_ref, mix_ref, state_ref, dintra_ref, qdec_ref, kdec_ref, wc_ref, btab_ref):
    ts, d = x_ref.shape
    b = pl.program_id(0)
    s = pl.program_id(1)

    @pl.when((b == 0) & (s == 0))
    def _init_tables():
        i = lax.broadcasted_iota(I32, (CHUNK, CHUNK), 0)
        j = lax.broadcasted_iota(I32, (CHUNK, CHUNK), 1)
        diff = (i - j).astype(F32)
        fi = i.astype(F32)
        for h in range(RET_HEADS):
            lg = _LOG_GAMMA[h]
            dintra_ref[h] = jnp.where(i >= j, jnp.exp(lg * diff), 0.0) * _K_SCALE
            qdec_ref[h] = jnp.exp(lg * (fi + 1.0))
            kdec_ref[h] = jnp.exp(lg * (CHUNK - 1.0 - fi)) * _K_SCALE
        for g in range(SGU_GROUPS):
            wc_ref[g] = jnp.where(i >= j, sw_ref[g], 0.0).astype(BF16)
            btab_ref[g] = jnp.broadcast_to(sb_ref[g], (CHUNK, GROUP_DIM))

    @pl.when(s == 0)
    def _reset_state():
        state_ref[...] = jnp.zeros_like(state_ref)

    x = x_ref[...]
    h = _rms(x, gmix_ref[...]).astype(BF16)
    for sec in range(N_SECTIONS):
        proj_ref[sec] = jnp.dot(h, win_ref[:, sec * RET_WIDTH:(sec + 1) * RET_WIDTH],
                                preferred_element_type=F32)

    def chunk_body(c, carry):
        rows = pl.ds(pl.multiple_of(c * CHUNK, CHUNK), CHUNK)
        ang_t = invf_ref[...] * pos_ref[c].astype(F32)
        cos_t = jnp.cos(ang_t)
        sin_t = jnp.sin(ang_t)
        cosf = jnp.concatenate([cos_t, cos_t], axis=0).T
        sinf = jnp.concatenate([-sin_t, sin_t], axis=0).T
        for hd in range(RET_HEADS):
            hs = slice(hd * HEAD_DIM, (hd + 1) * HEAD_DIM)
            q = proj_ref[0, rows, hs]
            k = proj_ref[1, rows, hs]
            vb = proj_ref[2, rows, hs].astype(BF16)
            gate = proj_ref[3, rows, hs]
            qr = q * cosf + pltpu.roll(q, HEAD_DIM // 2, 1) * sinf
            kr = k * cosf + pltpu.roll(k, HEAD_DIM // 2, 1) * sinf
            scores = lax.dot_general(qr.astype(BF16), kr.astype(BF16), _NT,
                                     preferred_element_type=F32) * dintra_ref[hd]
            st = state_ref[hd]
            o = (jnp.dot(scores.astype(BF16), vb, preferred_element_type=F32)
                 + jnp.dot((qr * qdec_ref[hd]).astype(BF16), st.astype(BF16), preferred_element_type=F32))
            kv = lax.dot_general((kr * kdec_ref[hd]).astype(BF16), vb, _TN, preferred_element_type=F32)
            state_ref[hd] = st * _CHUNK_DECAY[hd] + kv
            d = o - jnp.mean(o, axis=-1, keepdims=True)
            on = d * lax.rsqrt(jnp.mean(d * d, axis=-1, keepdims=True) + EPS) * gn_ref[:, hs]
            mix_ref[rows, hs] = (gate * _sigmoid(gate) * on).astype(BF16)
        for g in range(SGU_GROUPS):
            gs = slice(g * GROUP_DIM, (g + 1) * GROUP_DIM)
            u = _gelu(proj_ref[4, rows, gs])
            v = _gelu(proj_ref[5, rows, gs])
            vn = v * lax.rsqrt(jnp.mean(v * v, axis=-1, keepdims=True) + EPS) * sgn_ref[:, gs]
            sg = jnp.dot(wc_ref[g], vn.astype(BF16), preferred_element_type=F32) + btab_ref[g]
            mix_ref[rows, pl.ds(RET_WIDTH + g * GROUP_DIM, GROUP_DIM)] = (u * sg).astype(BF16)
        return carry

    lax.fori_loop(0, ts // CHUNK, chunk_body, 0)

    x1 = x_ref[...] + jnp.dot(mix_ref[...], wout_ref[...], preferred_element_type=F32)
    x1_ref[...] = x1
    h2 = _rms(x1, gffn_ref[...])
    for sl in range(d // LANES):
        h2_ref[:, sl, :] = h2[:, sl * LANES:(sl + 1) * LANES]

    h2h = h2.astype(BF16)
    h2l = (h2 - h2h.astype(F32)).astype(BF16)
    wrh = wrh_ref[...]
    logits = (lax.dot_general(wrh, h2h, _NT, preferred_element_type=F32)
              + lax.dot_general(wrh, h2l, _NT, preferred_element_type=F32)
              + lax.dot_general(wrl_ref[...], h2h, _NT, preferred_element_type=F32)
              + br_ref[...])
    el = logits[0:N_EXPERTS]
    gl = logits[N_EXPERTS:N_EXPERTS + N_GROUPS]
    rg = lax.broadcasted_iota(I32, (N_GROUPS, ts), 0).astype(F32)
    gmax = jnp.max(gl, axis=0, keepdims=True)
    gidx = jnp.min(jnp.where(gl == gmax, rg, float(N_GROUPS)), axis=0, keepdims=True)
    gweight = 1.0 / jnp.sum(jnp.exp(gl - gmax), axis=0, keepdims=True)
    re_i = lax.broadcasted_iota(I32, (N_EXPERTS, ts), 0)
    re = re_i.astype(F32)
    in_group = (re_i // N_PER_GROUP).astype(F32) == gidx
    neg = -jnp.inf
    sel = jnp.where(in_group, el, neg)
    m1 = jnp.max(sel, axis=0, keepdims=True)
    i1 = jnp.min(jnp.where(sel == m1, re, float(N_EXPERTS)), axis=0, keepdims=True)
    sel2 = jnp.where(re == i1, neg, sel)
    m2 = jnp.max(sel2, axis=0, keepdims=True)
    i2 = jnp.min(jnp.where(sel2 == m2, re, float(N_EXPERTS)), axis=0, keepdims=True)
    e2 = jnp.exp(m2 - m1)
    w1 = 1.0 / (1.0 + e2)
    ids_ref[...] = jnp.concatenate([i1, i2], axis=0).astype(I32)
    gates_t = jnp.concatenate([gweight * w1, gweight * (e2 * w1), jnp.zeros((LANES - 2, ts), F32)], axis=0)
    gcol_ref[...] = gates_t.T


def _attn_router(x, pos4, gmix, win, gn, sgn, sw, sb, wout, invf, gffn, wrh, wrl, br):
    bsz, seq, d = x.shape
    ts = ATTN_TILE
    ns = seq // ts
    t = bsz * seq
    const2 = lambda b, s: (0, 0)
    const3 = lambda b, s: (0, 0, 0)
    single = dict(pipeline_mode=pl.Buffered(1))
    in_specs = [
        pl.BlockSpec((None, ts, d), lambda b, s: (b, s, 0)),
        pl.BlockSpec((None, ts // CHUNK, 1, CHUNK), lambda b, s: (b, s, 0, 0)),
        pl.BlockSpec((1, d), const2),
        pl.BlockSpec(win.shape, const2, **single),
        pl.BlockSpec((1, RET_WIDTH), const2),
        pl.BlockSpec((1, SGU_WIDTH), const2),
        pl.BlockSpec(sw.shape, const3),
        pl.BlockSpec(sb.shape, const3),
        pl.BlockSpec(wout.shape, const2, **single),
        pl.BlockSpec(invf.shape, const2),
        pl.BlockSpec((1, d), const2),
        pl.BlockSpec(wrh.shape, const2),
        pl.BlockSpec(wrl.shape, const2),
        pl.BlockSpec(br.shape, const2),
    ]
    tok = lambda b, s: (b * ns + s, 0)
    out_specs = [
        pl.BlockSpec((ts, d), tok),
        pl.BlockSpec((ts, d // LANES, LANES), lambda b, s: (b * ns + s, 0, 0)),
        pl.BlockSpec((2, ts), lambda b, s: (0, b * ns + s)),
        pl.BlockSpec((ts, LANES), tok),
    ]
    out_shape = [
        jax.ShapeDtypeStruct((t, d), F32),
        jax.ShapeDtypeStruct((t, d // LANES, LANES), F32),
        jax.ShapeDtypeStruct((2, t), I32),
        jax.ShapeDtypeStruct((t, LANES), F32),
    ]
    scratch = [
        pltpu.VMEM((N_SECTIONS, ts, RET_WIDTH), F32),
        pltpu.VMEM((ts, RET_WIDTH + SGU_WIDTH), BF16),
        pltpu.VMEM((RET_HEADS, HEAD_DIM, HEAD_DIM), F32),
        pltpu.VMEM((RET_HEADS, CHUNK, CHUNK), F32),
        pltpu.VMEM((RET_HEADS, CHUNK, HEAD_DIM), F32),
        pltpu.VMEM((RET_HEADS, CHUNK, HEAD_DIM), F32),
        pltpu.VMEM((SGU_GROUPS, CHUNK, CHUNK), BF16),
        pltpu.VMEM((SGU_GROUPS, CHUNK, GROUP_DIM), F32),
    ]
    return pl.pallas_call(
        _attn_router_kernel,
        grid=(bsz, ns),
        in_specs=in_specs,
        out_specs=out_specs,
        out_shape=out_shape,
        scratch_shapes=scratch,
        compiler_params=pltpu.CompilerParams(
            dimension_semantics=("arbitrary", "arbitrary"), vmem_limit_bytes=VMEM_LIMIT_BYTES),
        name="attn_router",
    )(x, pos4, gmix, win, gn, sgn, sw, sb, wout, invf, gffn, wrh, wrl, br)


def _rank_kernel(ids_ref, pos_ref, cnt_ref, run_ref, before_ref):
    phase = pl.program_id(0)
    i = pl.program_id(1)
    tr = ids_ref.shape[1]
    ids = ids_ref[...]
    re = lax.broadcasted_iota(I32, (N_EXPERTS, tr), 0)
    hit1 = re == ids[0:1]
    hit2 = re == ids[1:2]
    onehot = jnp.where(hit1 | hit2, 1.0, 0.0)
    tile_cnt = jnp.sum(onehot, axis=1, keepdims=True)

    @pl.when((phase == 0) & (i == 0))
    def _zero():
        cnt_ref[...] = jnp.zeros_like(cnt_ref)

    @pl.when(phase == 0)
    def _count():
        cnt_ref[...] += jnp.broadcast_to(tile_cnt, cnt_ref.shape)

    @pl.when((phase == 1) & (i == 0))
    def _offsets():
        c = cnt_ref[...]
        c_hi = jnp.floor(c * (1.0 / 256.0))
        c_lo = c - 256.0 * c_hi
        a = lax.broadcasted_iota(I32, (N_EXPERTS, N_EXPERTS), 0)
        bb = lax.broadcasted_iota(I32, (N_EXPERTS, N_EXPERTS), 1)
        lower = jnp.where(bb < a, 1.0, 0.0).astype(BF16)
        run_ref[...] = (256.0 * jnp.dot(lower, c_hi.astype(BF16), preferred_element_type=F32)
                        + jnp.dot(lower, c_lo.astype(BF16), preferred_element_type=F32))
        ta = lax.broadcasted_iota(I32, (tr, tr), 0)
        tb = lax.broadcasted_iota(I32, (tr, tr), 1)
        before_ref[...] = jnp.where(ta < tb, 1.0, 0.0).astype(BF16)

    @pl.when(phase == 1)
    def _rank():
        prefix = jnp.dot(onehot.astype(BF16), before_ref[...], preferred_element_type=F32)
        posmat = run_ref[:, 0:1] + prefix
        p1 = jnp.sum(jnp.where(hit1, posmat, 0.0), axis=0, keepdims=True)
        p2 = jnp.sum(jnp.where(hit2, posmat, 0.0), axis=0, keepdims=True)
        pos_ref[...] = jnp.concatenate([p1, p2], axis=0).astype(I32)
        run_ref[...] += jnp.broadcast_to(tile_cnt, run_ref.shape)


def _rank(ids):
    t = ids.shape[1]
    tr = RANK_TILE
    return pl.pallas_call(
        _rank_kernel,
        grid=(2, t // tr),
        in_specs=[pl.BlockSpec((2, tr), lambda p, i: (0, i))],
        out_specs=[pl.BlockSpec((2, tr), lambda p, i: (0, i * p)),
                   pl.BlockSpec((N_EXPERTS, LANES), lambda p, i: (0, 0))],
        out_shape=[jax.ShapeDtypeStruct((2, t), I32), jax.ShapeDtypeStruct((N_EXPERTS, LANES), F32)],
        scratch_shapes=[pltpu.VMEM((N_EXPERTS, LANES), F32), pltpu.VMEM((tr, tr), BF16)],
        compiler_params=pltpu.CompilerParams(dimension_semantics=("arbitrary", "arbitrary")),
        name="rank",
    )(ids)


def _invert_kernel(p1_ref, p2_ref, tok_ref):
    n = p1_ref.shape[0]
    base = pl.program_id(0) * n

    def body(j, carry):
        tok_ref[p1_ref[j]] = base + j
        tok_ref[p2_ref[j]] = base + j
        return carry

    lax.fori_loop(0, n, body, 0, unroll=8)


def _invert(p1, p2):
    t = p1.shape[0]
    n = INVERT_TILE
    smem = lambda: pl.BlockSpec((n,), lambda i: (i,), memory_space=pltpu.SMEM)
    return pl.pallas_call(
        _invert_kernel,
        grid=(t // n,),
        in_specs=[smem(), smem()],
        out_specs=pl.BlockSpec(memory_space=pltpu.SMEM),
        out_shape=jax.ShapeDtypeStruct((2 * t,), I32),
        compiler_params=pltpu.CompilerParams(dimension_semantics=("arbitrary",)),
        name="invert",
    )(p1, p2)


def _expert_kernel(ptile_ref, pexp_ref, plo_ref, phi_ref, np_ref, tok_ref, toknext_ref, h2_ref,
                   wg_ref, wu_ref, wd_ref, o_ref, xbuf_ref, acc_ref, sems, *, n_tiles):
    i = pl.program_id(0)
    n_steps = pl.num_programs(0)
    tm, n_sub, _ = o_ref.shape
    tile = ptile_ref[i]
    valid = i < np_ref[0]
    first = (i == 0) | (tile != ptile_ref[jnp.maximum(i - 1, 0)])
    last = (i == np_ref[0] - 1) | (ptile_ref[jnp.minimum(i + 1, n_steps - 1)] != tile)
    slot = tile & 1

    def row_copy(src_row, j, sl):
        return pltpu.make_async_copy(h2_ref.at[pl.ds(src_row, 1)], xbuf_ref.at[sl, pl.ds(j, 1)], sems.at[sl])

    def start_gather(idx_ref, sl):
        def body(jj, carry):
            j = 2 * jj
            row_copy(idx_ref[j], j, sl).start(priority=0)
            row_copy(idx_ref[j + 1], j + 1, sl).start(priority=1)
            return carry
        lax.fori_loop(0, tm // 2, body, 0, unroll=4)

    def wait_gather(sl):
        def body(j, carry):
            row_copy(0, j, sl).wait()
            return carry
        lax.fori_loop(0, tm, body, 0, unroll=8)

    @pl.when(i == 0)
    def _():
        acc_ref[...] = jnp.zeros_like(acc_ref)
        start_gather(tok_ref, slot)

    @pl.when(valid & first)
    def _():
        wait_gather(slot)

        @pl.when(tile + 1 < n_tiles)
        def _():
            start_gather(toknext_ref, 1 - slot)

    @pl.when(valid)
    def _():
        xt = jnp.concatenate([xbuf_ref[slot, :, sl, :] for sl in range(n_sub)], axis=1).astype(BF16)
        a = jnp.dot(xt, wg_ref[...], preferred_element_type=F32)
        bb = jnp.dot(xt, wu_ref[...], preferred_element_type=F32)
        row = lax.broadcasted_iota(I32, (tm, 1), 0)
        mine = (row >= plo_ref[i]) & (row < phi_ref[i])
        act = jnp.where(mine, a * _sigmoid(a) * bb, 0.0)
        o = jnp.dot(act.astype(BF16), wd_ref[...], preferred_element_type=F32)
        acc_ref[...] = jnp.where(first, 0.0, acc_ref[...]) + o

        @pl.when(last)
        def _():
            for sl in range(n_sub):
                o_ref[:, sl, :] = acc_ref[:, sl * LANES:(sl + 1) * LANES]


def _experts(ptile, pexp, plo, phi, npairs, tok_of, h2, wg, wu, wd):
    t, n_sub, _ = h2.shape
    d = n_sub * LANES
    r = tok_of.shape[0]
    f = wg.shape[-1]
    tm = EXPERT_TILE
    n_tiles = r // tm
    n_steps = ptile.shape[0]
    grid_spec = pltpu.PrefetchScalarGridSpec(
        num_scalar_prefetch=5,
        grid=(n_steps,),
        in_specs=[
            pl.BlockSpec((tm,), lambda i, pt, pe, lo, hi, n: (pt[i],), memory_space=pltpu.SMEM),
            pl.BlockSpec((tm,), lambda i, pt, pe, lo, hi, n: (jnp.minimum(pt[i] + 1, n_tiles - 1),),
                         memory_space=pltpu.SMEM),
            pl.BlockSpec(memory_space=pl.ANY),
            pl.BlockSpec((None, d, f), lambda i, pt, pe, lo, hi, n: (pe[i], 0, 0)),
            pl.BlockSpec((None, d, f), lambda i, pt, pe, lo, hi, n: (pe[i], 0, 0)),
            pl.BlockSpec((None, f, d), lambda i, pt, pe, lo, hi, n: (pe[i], 0, 0)),
        ],
        out_specs=pl.BlockSpec((tm, n_sub, LANES), lambda i, pt, pe, lo, hi, n: (pt[i], 0, 0)),
        scratch_shapes=[pltpu.VMEM((2, tm, n_sub, LANES), F32), pltpu.VMEM((tm, d), F32),
                        pltpu.SemaphoreType.DMA((2,))],
    )
    return pl.pallas_call(
        functools.partial(_expert_kernel, n_tiles=n_tiles),
        grid_spec=grid_spec,
        out_shape=jax.ShapeDtypeStruct((r, n_sub, LANES), F32),
        compiler_params=pltpu.CompilerParams(
            dimension_semantics=("arbitrary",), vmem_limit_bytes=VMEM_LIMIT_BYTES),
        name="experts",
    )(ptile, pexp, plo, phi, npairs, tok_of, tok_of, h2, wg, wu, wd)


def _pair_table(cnt, n_tiles):
    tm = EXPERT_TILE
    n_steps = n_tiles + N_EXPERTS - 1
    offs = jnp.cumsum(cnt) - cnt
    first = offs // tm
    last = jnp.where(cnt > 0, (offs + cnt - 1) // tm, first - 1)
    n_e = last - first + 1
    ends = jnp.cumsum(n_e)
    total = ends[-1]
    p = jnp.minimum(jnp.arange(n_steps, dtype=I32), total - 1)
    e = jnp.sum((p[:, None] >= ends[None, :]).astype(I32), axis=1)
    tile = first[e] + (p - (ends[e] - n_e[e]))
    lo = jnp.clip(offs[e] - tile * tm, 0, tm)
    hi = jnp.clip(offs[e] + cnt[e] - tile * tm, 0, tm)
    return tile.astype(I32), e.astype(I32), lo.astype(I32), hi.astype(I32), total.reshape(1).astype(I32)


def _combine_kernel(p1_ref, p2_ref, p1n_ref, p2n_ref, gcol_ref, x1_ref, gfin_ref, os_ref, out_ref, buf_ref, sems):
    i = pl.program_id(0)
    n_steps = pl.num_programs(0)
    tm = x1_ref.shape[0]
    n_sub = buf_ref.shape[3]
    slot = i & 1

    def row_copy(src_row, j, sl, k):
        return pltpu.make_async_copy(os_ref.at[pl.ds(src_row, 1)], buf_ref.at[sl, k, pl.ds(j, 1)], sems.at[sl])

    def start_gather(r1_ref, r2_ref, sl):
        def body(j, carry):
            row_copy(r1_ref[j], j, sl, 0).start(priority=0)
            row_copy(r2_ref[j], j, sl, 1).start(priority=1)
            return carry
        lax.fori_loop(0, tm, body, 0, unroll=4)

    @pl.when(i == 0)
    def _():
        start_gather(p1_ref, p2_ref, slot)

    @pl.when(i + 1 < n_steps)
    def _():
        start_gather(p1n_ref, p2n_ref, 1 - slot)

    def wait_body(j, carry):
        row_copy(0, j, slot, 0).wait()
        row_copy(0, j, slot, 1).wait()
        return carry
    lax.fori_loop(0, tm, wait_body, 0, unroll=8)

    o1 = jnp.concatenate([buf_ref[slot, 0, :, sl, :] for sl in range(n_sub)], axis=1)
    o2 = jnp.concatenate([buf_ref[slot, 1, :, sl, :] for sl in range(n_sub)], axis=1)
    g = gcol_ref[...]
    y = g[:, 0:1] * o1 + g[:, 1:2] * o2
    out_ref[...] = _rms(x1_ref[...] + y, gfin_ref[...])


def _combine(p1, p2, gcol, x1, gfin, osorted):
    t, d = x1.shape
    n_sub = d // LANES
    tm = MOVE_TILE
    n_steps = t // tm
    cur = lambda: pl.BlockSpec((tm,), lambda i: (i,), memory_space=pltpu.SMEM)
    nxt = lambda: pl.BlockSpec((tm,), lambda i: (jnp.minimum(i + 1, n_steps - 1),), memory_space=pltpu.SMEM)
    return pl.pallas_call(
        _combine_kernel,
        grid=(n_steps,),
        in_specs=[cur(), cur(), nxt(), nxt(),
                  pl.BlockSpec((tm, LANES), lambda i: (i, 0)),
                  pl.BlockSpec((tm, d), lambda i: (i, 0)),
                  pl.BlockSpec((1, d), lambda i: (0, 0)),
                  pl.BlockSpec(memory_space=pl.ANY)],
        out_specs=pl.BlockSpec((tm, d), lambda i: (i, 0)),
        out_shape=jax.ShapeDtypeStruct((t, d), F32),
        scratch_shapes=[pltpu.VMEM((2, 2, tm, n_sub, LANES), F32), pltpu.SemaphoreType.DMA((2,))],
        compiler_params=pltpu.CompilerParams(dimension_semantics=("arbitrary",)),
        name="combine",
    )(p1, p2, p1, p2, gcol, x1, gfin, osorted)


def kernel(x, positions, norm_mix_gain, w_in, ret_gn_gain, sgu_norm_gain, sgu_w, sgu_b, w_out, norm_ffn_gain,
           w_router_group, b_router_group, w_router_expert, b_router_expert, w_expert_gate_up, w_expert_up,
           w_expert_down, final_norm_gain):
    bsz, seq, d = x.shape
    assert w_in.shape[0] == 1, "single-layer block"
    assert seq % ATTN_TILE == 0 and (bsz * seq) % RANK_TILE == 0 and (2 * bsz * seq) % EXPERT_TILE == 0
    t = bsz * seq
    half = HEAD_DIM // 2

    pos4 = positions.reshape(bsz, seq // CHUNK, 1, CHUNK)
    invf = (ROPE_BASE ** (-jnp.arange(half, dtype=F32) * 2.0 / HEAD_DIM)).reshape(half, 1)
    wr = jnp.concatenate([
        jnp.transpose(w_router_expert[0], (0, 2, 1)).reshape(N_EXPERTS, d),
        w_router_group[0].T,
        jnp.zeros((ROUTER_ROWS - N_EXPERTS - N_GROUPS, d), F32)], axis=0)
    wrh = wr.astype(BF16)
    wrl = (wr - wrh.astype(F32)).astype(BF16)
    br = jnp.concatenate([b_router_expert[0].reshape(N_EXPERTS), b_router_group[0],
                          jnp.zeros((ROUTER_ROWS - N_EXPERTS - N_GROUPS,), F32)]).reshape(ROUTER_ROWS, 1)

    x1, h2, ids, gcol = _attn_router(
        x, pos4, norm_mix_gain[0].reshape(1, d), w_in[0].astype(BF16), ret_gn_gain[0].reshape(1, RET_WIDTH),
        sgu_norm_gain[0].reshape(1, SGU_WIDTH), sgu_w[0], sgu_b[0].reshape(SGU_GROUPS, CHUNK, 1),
        w_out[0].astype(BF16), invf, norm_ffn_gain[0].reshape(1, d), wrh, wrl, br)

    pos, cnt = _rank(ids)
    p1, p2 = pos[0], pos[1]
    tok_of = _invert(p1, p2)

    ff = w_expert_gate_up.shape[-1]
    ptile, pexp, plo, phi, npairs = _pair_table(cnt[:, 0].astype(I32), (2 * t) // EXPERT_TILE)
    osorted = _experts(ptile, pexp, plo, phi, npairs, tok_of, h2,
                       w_expert_gate_up[0].reshape(N_EXPERTS, d, ff).astype(BF16),
                       w_expert_up[0].reshape(N_EXPERTS, d, ff).astype(BF16),
                       w_expert_down[0].reshape(N_EXPERTS, ff, d).astype(BF16))

    out = _combine(p1, p2, gcol, x1, final_norm_gain.reshape(1, d), osorted)
    return out.reshape(bsz, seq, d)
```

```python
import functools
import math

import jax
import jax.numpy as jnp
from jax import lax
from jax.experimental import pallas as pl
from jax.experimental.pallas import tpu as pltpu

F32 = jnp.float32
BF16 = jnp.bfloat16
I32 = jnp.int32

RET_HEADS = 4
HEAD_DIM = 128
CHUNK = 128
SGU_GROUPS = 4
GROUP_DIM = 128
RET_WIDTH = RET_HEADS * HEAD_DIM
SGU_WIDTH = SGU_GROUPS * GROUP_DIM
N_SECTIONS = 6
N_GROUPS = 4
N_PER_GROUP = 8
N_EXPERTS = N_GROUPS * N_PER_GROUP
ROUTER_ROWS = 40
ROPE_BASE = 10000.0
EPS = 1e-6

LANES = 128
SUBLANES = 8
VMEM_LIMIT_BYTES = 56 * 1024 * 1024

ATTN_TILE = 512
RANK_TILE = 512
INVERT_TILE = 1024
MOVE_TILE = 256
EXPERT_TILE = 256

_NT = (((1,), (1,)), ((), ()))
_TN = (((0,), (0,)), ((), ()))

_LOG_GAMMA = [math.log(1.0 - 2.0 ** (-5.0 - h)) for h in range(RET_HEADS)]
_CHUNK_DECAY = [math.exp(lg * CHUNK) for lg in _LOG_GAMMA]
_K_SCALE = HEAD_DIM ** -0.5


def _rms(x, gain):
    return x * lax.rsqrt(jnp.mean(x * x, axis=-1, keepdims=True) + EPS) * gain


def _gelu(x):
    return 0.5 * x * (1.0 + lax.erf(x * 0.7071067811865476))


def _sigmoid(x):
    return 1.0 / (1.0 + jnp.exp(-x))


def _token_rows(row):
    return pl.ds(pl.multiple_of(row * SUBLANES, SUBLANES), SUBLANES)


def _load_token_major(ref, lead, n_tok):
    return jnp.concatenate(
        [ref[lead + (pl.ds(sl, n_tok, stride=SUBLANES), slice(None))] for sl in range(SUBLANES)], axis=1)


def _store_token_major(ref, value):
    n_tok = value.shape[0]
    for sl in range(SUBLANES):
        ref[pl.ds(sl, n_tok, stride=SUBLANES), :] = value[:, sl * LANES:(sl + 1) * LANES]


def _attn_router_kernel(x_ref, pos_ref, gmix_ref, win_ref, gn_ref, sgn_ref, sw_ref, sb_ref, wout_ref,
                        invf_ref, gffn_ref, wrh_ref, wrl_ref, br_ref,
                        x1_ref, h2_ref, ids_ref, gcol_ref,
                        proj_ref, mix_ref, state_ref, dintra_ref, qdec_ref, kdec_ref, wc_ref, btab_ref):
    ts, d = x_ref.shape
    b = pl.program_id(0)
    s = pl.program_id(1)

    @pl.when((b == 0) & (s == 0))
    def _init_tables():
        i = lax.broadcasted_iota(I32, (CHUNK, CHUNK), 0)
        j = lax.broadcasted_iota(I32, (CHUNK, CHUNK), 1)
        diff = (i - j).astype(F32)
        fi = i.astype(F32)
        for h in range(RET_HEADS):
            lg = _LOG_GAMMA[h]
            dintra_ref[h] = jnp.where(i >= j, jnp.exp(lg * diff), 0.0) * _K_SCALE
            qdec_ref[h] = jnp.exp(lg * (fi + 1.0))
            kdec_ref[h] = jnp.exp(lg * (CHUNK - 1.0 - fi)) * _K_SCALE
        for g in range(SGU_GROUPS):
            wc_ref[g] = jnp.where(i >= j, sw_ref[g], 0.0).astype(BF16)
            btab_ref[g] = jnp.broadcast_to(sb_ref[g], (CHUNK, GROUP_DIM))

    @pl.when(s == 0)
    def _reset_state():
        state_ref[...] = jnp.zeros_like(state_ref)

    x = x_ref[...]
    h = _rms(x, gmix_ref[...]).astype(BF16)
    for sec in range(N_SECTIONS):
        proj_ref[sec] = jnp.dot(h, win_ref[:, sec * RET_WIDTH:(sec + 1) * RET_WIDTH],
                                preferred_element_type=F32)

    def chunk_body(c, carry):
        rows = pl.ds(pl.multiple_of(c * CHUNK, CHUNK), CHUNK)
        ang_t = invf_ref[...] * pos_ref[c].astype(F32)
        cos_t = jnp.cos(ang_t)
        sin_t = jnp.sin(ang_t)
        cosf = jnp.concatenate([cos_t, cos_t], axis=0).T
        sinf = jnp.concatenate([-sin_t, sin_t], axis=0).T
        for hd in range(RET_HEADS):
            hs = slice(hd * HEAD_DIM, (hd + 1) * HEAD_DIM)
            q = proj_ref[0, rows, hs]
            k = proj_ref[1, rows, hs]
            vb = proj_ref[2, rows, hs].astype(BF16)
            gate = proj_ref[3, rows, hs]
            qr = q * cosf + pltpu.roll(q, HEAD_DIM // 2, 1) * sinf
            kr = k * cosf + pltpu.roll(k, HEAD_DIM // 2, 1) * sinf
            scores = lax.dot_general(qr.astype(BF16), kr.astype(BF16), _NT,
                                     preferred_element_type=F32) * dintra_ref[hd]
            st = state_ref[hd]
            o = (jnp.dot(scores.astype(BF16), vb, preferred_element_type=F32)
                 + jnp.dot((qr * qdec_ref[hd]).astype(BF16), st.astype(BF16), preferred_element_type=F32))
            kv = lax.dot_general((kr * kdec_ref[hd]).astype(BF16), vb, _TN, preferred_element_type=F32)
            state_ref[hd] = st * _CHUNK_DECAY[hd] + kv
            d = o - jnp.mean(o, axis=-1, keepdims=True)
            on = d * lax.rsqrt(jnp.mean(d * d, axis=-1, keepdims=True) + EPS) * gn_ref[:, hs]
            mix_ref[rows, hs] = (gate * _sigmoid(gate) * on).astype(BF16)
        for g in range(SGU_GROUPS):
            gs = slice(g * GROUP_DIM, (g + 1) * GROUP_DIM)
            u = _gelu(proj_ref[4, rows, gs])
            v = _gelu(proj_ref[5, rows, gs])
            vn = v * lax.rsqrt(jnp.mean(v * v, axis=-1, keepdims=True) + EPS) * sgn_ref[:, gs]
            sg = jnp.dot(wc_ref[g], vn.astype(BF16), preferred_element_type=F32) + btab_ref[g]
            mix_ref[rows, pl.ds(RET_WIDTH + g * GROUP_DIM, GROUP_DIM)] = (u * sg).astype(BF16)
        return carry

    lax.fori_loop(0, ts // CHUNK, chunk_body, 0)

    x1 = x_ref[...] + jnp.dot(mix_ref[...], wout_ref[...], preferred_element_type=F32)
    x1_ref[...] = x1
    h2 = _rms(x1, gffn_ref[...])
    _store_token_major(h2_ref, h2)

    h2h = h2.astype(BF16)
    h2l = (h2 - h2h.astype(F32)).astype(BF16)
    wrh = wrh_ref[...]
    logits = (lax.dot_general(wrh, h2h, _NT, preferred_element_type=F32)
              + lax.dot_general(wrh, h2l, _NT, preferred_element_type=F32)
              + lax.dot_general(wrl_ref[...], h2h, _NT, preferred_element_type=F32)
              + br_ref[...])
    el = logits[0:N_EXPERTS]
    gl = logits[N_EXPERTS:N_EXPERTS + N_GROUPS]
    rg = lax.broadcasted_iota(I32, (N_GROUPS, ts), 0).astype(F32)
    gmax = jnp.max(gl, axis=0, keepdims=True)
    gidx = jnp.min(jnp.where(gl == gmax, rg, float(N_GROUPS)), axis=0, keepdims=True)
    gweight = 1.0 / jnp.sum(jnp.exp(gl - gmax), axis=0, keepdims=True)
    re_i = lax.broadcasted_iota(I32, (N_EXPERTS, ts), 0)
    re = re_i.astype(F32)
    in_group = (re_i // N_PER_GROUP).astype(F32) == gidx
    neg = -jnp.inf
    sel = jnp.where(in_group, el, neg)
    m1 = jnp.max(sel, axis=0, keepdims=True)
    i1 = jnp.min(jnp.where(sel == m1, re, float(N_EXPERTS)), axis=0, keepdims=True)
    sel2 = jnp.where(re == i1, neg, sel)
    m2 = jnp.max(sel2, axis=0, keepdims=True)
    i2 = jnp.min(jnp.where(sel2 == m2, re, float(N_EXPERTS)), axis=0, keepdims=True)
    e2 = jnp.exp(m2 - m1)
    w1 = 1.0 / (1.0 + e2)
    ids_ref[...] = jnp.concatenate([i1, i2], axis=0).astype(I32)
    gates_t = jnp.concatenate([gweight * w1, gweight * (e2 * w1), jnp.zeros((LANES - 2, ts), F32)], axis=0)
    gcol_ref[...] = gates_t.T


def _attn_router(x, pos4, gmix, win, gn, sgn, sw, sb, wout, invf, gffn, wrh, wrl, br):
    bsz, seq, d = x.shape
    ts = ATTN_TILE
    ns = seq // ts
    t = bsz * seq
    const2 = lambda b, s: (0, 0)
    const3 = lambda b, s: (0, 0, 0)
    single = dict(pipeline_mode=pl.Buffered(1))
    in_specs = [
        pl.BlockSpec((None, ts, d), lambda b, s: (b, s, 0)),
        pl.BlockSpec((None, ts // CHUNK, 1, CHUNK), lambda b, s: (b, s, 0, 0)),
        pl.BlockSpec((1, d), const2),
        pl.BlockSpec(win.shape, const2, **single),
        pl.BlockSpec((1, RET_WIDTH), const2),
        pl.BlockSpec((1, SGU_WIDTH), const2),
        pl.BlockSpec(sw.shape, const3),
        pl.BlockSpec(sb.shape, const3),
        pl.BlockSpec(wout.shape, const2, **single),
        pl.BlockSpec(invf.shape, const2),
        pl.BlockSpec((1, d), const2),
        pl.BlockSpec(wrh.shape, const2),
        pl.BlockSpec(wrl.shape, const2),
        pl.BlockSpec(br.shape, const2),
    ]
    tok = lambda b, s: (b * ns + s, 0)
    out_specs = [
        pl.BlockSpec((ts, d), tok),
        pl.BlockSpec((ts * SUBLANES, LANES), tok),
        pl.BlockSpec((2, ts), lambda b, s: (0, b * ns + s)),
        pl.BlockSpec((ts, LANES), tok),
    ]
    out_shape = [
        jax.ShapeDtypeStruct((t, d), F32),
        jax.ShapeDtypeStruct((t * SUBLANES, LANES), F32),
        jax.ShapeDtypeStruct((2, t), I32),
        jax.ShapeDtypeStruct((t, LANES), F32),
    ]
    scratch = [
        pltpu.VMEM((N_SECTIONS, ts, RET_WIDTH), F32),
        pltpu.VMEM((ts, RET_WIDTH + SGU_WIDTH), BF16),
        pltpu.VMEM((RET_HEADS, HEAD_DIM, HEAD_DIM), F32),
        pltpu.VMEM((RET_HEADS, CHUNK, CHUNK), F32),
        pltpu.VMEM((RET_HEADS, CHUNK, HEAD_DIM), F32),
        pltpu.VMEM((RET_HEADS, CHUNK, HEAD_DIM), F32),
        pltpu.VMEM((SGU_GROUPS, CHUNK, CHUNK), BF16),
        pltpu.VMEM((SGU_GROUPS, CHUNK, GROUP_DIM), F32),
    ]
    return pl.pallas_call(
        _attn_router_kernel,
        grid=(bsz, ns),
        in_specs=in_specs,
        out_specs=out_specs,
        out_shape=out_shape,
        scratch_shapes=scratch,
        compiler_params=pltpu.CompilerParams(
            dimension_semantics=("arbitrary", "arbitrary"), vmem_limit_bytes=VMEM_LIMIT_BYTES),
        name="attn_router",
    )(x, pos4, gmix, win, gn, sgn, sw, sb, wout, invf, gffn, wrh, wrl, br)


def _rank_kernel(ids_ref, pos_ref, cnt_ref, run_ref, before_ref):
    phase = pl.program_id(0)
    i = pl.program_id(1)
    tr = ids_ref.shape[1]
    ids = ids_ref[...]
    re = lax.broadcasted_iota(I32, (N_EXPERTS, tr), 0)
    hit1 = re == ids[0:1]
    hit2 = re == ids[1:2]
    onehot = jnp.where(hit1 | hit2, 1.0, 0.0)
    tile_cnt = jnp.sum(onehot, axis=1, keepdims=True)

    @pl.when((phase == 0) & (i == 0))
    def _zero():
        cnt_ref[...] = jnp.zeros_like(cnt_ref)

    @pl.when(phase == 0)
    def _count():
        cnt_ref[...] += jnp.broadcast_to(tile_cnt, cnt_ref.shape)

    @pl.when((phase == 1) & (i == 0))
    def _offsets():
        c = cnt_ref[...]
        c_hi = jnp.floor(c * (1.0 / 256.0))
        c_lo = c - 256.0 * c_hi
        a = lax.broadcasted_iota(I32, (N_EXPERTS, N_EXPERTS), 0)
        bb = lax.broadcasted_iota(I32, (N_EXPERTS, N_EXPERTS), 1)
        lower = jnp.where(bb < a, 1.0, 0.0).astype(BF16)
        run_ref[...] = (256.0 * jnp.dot(lower, c_hi.astype(BF16), preferred_element_type=F32)
                        + jnp.dot(lower, c_lo.astype(BF16), preferred_element_type=F32))
        ta = lax.broadcasted_iota(I32, (tr, tr), 0)
        tb = lax.broadcasted_iota(I32, (tr, tr), 1)
        before_ref[...] = jnp.where(ta < tb, 1.0, 0.0).astype(BF16)

    @pl.when(phase == 1)
    def _rank():
        prefix = jnp.dot(onehot.astype(BF16), before_ref[...], preferred_element_type=F32)
        posmat = run_ref[:, 0:1] + prefix
        p1 = jnp.sum(jnp.where(hit1, posmat, 0.0), axis=0, keepdims=True)
        p2 = jnp.sum(jnp.where(hit2, posmat, 0.0), axis=0, keepdims=True)
        pos_ref[...] = jnp.concatenate([p1, p2], axis=0).astype(I32)
        run_ref[...] += jnp.broadcast_to(tile_cnt, run_ref.shape)


def _rank(ids):
    t = ids.shape[1]
    tr = RANK_TILE
    return pl.pallas_call(
        _rank_kernel,
        grid=(2, t // tr),
        in_specs=[pl.BlockSpec((2, tr), lambda p, i: (0, i))],
        out_specs=[pl.BlockSpec((2, tr), lambda p, i: (0, i * p)),
                   pl.BlockSpec((N_EXPERTS, LANES), lambda p, i: (0, 0))],
        out_shape=[jax.ShapeDtypeStruct((2, t), I32), jax.ShapeDtypeStruct((N_EXPERTS, LANES), F32)],
        scratch_shapes=[pltpu.VMEM((N_EXPERTS, LANES), F32), pltpu.VMEM((tr, tr), BF16)],
        compiler_params=pltpu.CompilerParams(dimension_semantics=("arbitrary", "arbitrary")),
        name="rank",
    )(ids)


def _invert_kernel(p1_ref, p2_ref, tok_ref):
    n = p1_ref.shape[0]
    base = pl.program_id(0) * n

    def body(j, carry):
        tok_ref[p1_ref[j]] = base + j
        tok_ref[p2_ref[j]] = base + j
        return carry

    lax.fori_loop(0, n, body, 0, unroll=8)


def _invert(p1, p2):
    t = p1.shape[0]
    n = INVERT_TILE
    smem = lambda: pl.BlockSpec((n,), lambda i: (i,), memory_space=pltpu.SMEM)
    return pl.pallas_call(
        _invert_kernel,
        grid=(t // n,),
        in_specs=[smem(), smem()],
        out_specs=pl.BlockSpec(memory_space=pltpu.SMEM),
        out_shape=jax.ShapeDtypeStruct((2 * t,), I32),
        compiler_params=pltpu.CompilerParams(dimension_semantics=("arbitrary",)),
        name="invert",
    )(p1, p2)


def _expert_kernel(ptile_ref, pexp_ref, plo_ref, phi_ref, np_ref, tok_ref, toknext_ref, h2_ref,
                   wg_ref, wu_ref, wd_ref, o_ref, xbuf_ref, acc_ref, sems, *, n_tiles):
    i = pl.program_id(0)
    n_steps = pl.num_programs(0)
    tm = acc_ref.shape[0]
    tile = ptile_ref[i]
    valid = i < np_ref[0]
    first = (i == 0) | (tile != ptile_ref[jnp.maximum(i - 1, 0)])
    last = (i == np_ref[0] - 1) | (ptile_ref[jnp.minimum(i + 1, n_steps - 1)] != tile)
    slot = tile & 1

    def row_copy(src_row, j, sl):
        return pltpu.make_async_copy(h2_ref.at[_token_rows(src_row)], xbuf_ref.at[sl, _token_rows(j)], sems.at[sl])

    def start_gather(idx_ref, sl):
        def body(jj, carry):
            j = 2 * jj
            row_copy(idx_ref[j], j, sl).start(priority=0)
            row_copy(idx_ref[j + 1], j + 1, sl).start(priority=1)
            return carry
        lax.fori_loop(0, tm // 2, body, 0, unroll=4)

    def wait_gather(sl):
        def body(j, carry):
            row_copy(0, j, sl).wait()
            return carry
        lax.fori_loop(0, tm, body, 0, unroll=8)

    @pl.when(i == 0)
    def _():
        acc_ref[...] = jnp.zeros_like(acc_ref)
        start_gather(tok_ref, slot)

    @pl.when(valid & first)
    def _():
        wait_gather(slot)

        @pl.when(tile + 1 < n_tiles)
        def _():
            start_gather(toknext_ref, 1 - slot)

    @pl.when(valid)
    def _():
        xt = _load_token_major(xbuf_ref, (slot,), tm).astype(BF16)
        a = jnp.dot(xt, wg_ref[...], preferred_element_type=F32)
        bb = jnp.dot(xt, wu_ref[...], preferred_element_type=F32)
        row = lax.broadcasted_iota(I32, (tm, 1), 0)
        mine = (row >= plo_ref[i]) & (row < phi_ref[i])
        act = jnp.where(mine, a * _sigmoid(a) * bb, 0.0)
        o = jnp.dot(act.astype(BF16), wd_ref[...], preferred_element_type=F32)
        acc_ref[...] = jnp.where(first, 0.0, acc_ref[...]) + o

        @pl.when(last)
        def _():
            _store_token_major(o_ref, acc_ref[...])


def _experts(ptile, pexp, plo, phi, npairs, tok_of, h2, wg, wu, wd):
    d = SUBLANES * LANES
    r = tok_of.shape[0]
    f = wg.shape[-1]
    tm = EXPERT_TILE
    n_tiles = r // tm
    n_steps = ptile.shape[0]
    grid_spec = pltpu.PrefetchScalarGridSpec(
        num_scalar_prefetch=5,
        grid=(n_steps,),
        in_specs=[
            pl.BlockSpec((tm,), lambda i, pt, pe, lo, hi, n: (pt[i],), memory_space=pltpu.SMEM),
            pl.BlockSpec((tm,), lambda i, pt, pe, lo, hi, n: (jnp.minimum(pt[i] + 1, n_tiles - 1),),
                         memory_space=pltpu.SMEM),
            pl.BlockSpec(memory_space=pl.ANY),
            pl.BlockSpec((None, d, f), lambda i, pt, pe, lo, hi, n: (pe[i], 0, 0)),
            pl.BlockSpec((None, d, f), lambda i, pt, pe, lo, hi, n: (pe[i], 0, 0)),
            pl.BlockSpec((None, f, d), lambda i, pt, pe, lo, hi, n: (pe[i], 0, 0)),
        ],
        out_specs=pl.BlockSpec((tm * SUBLANES, LANES), lambda i, pt, pe, lo, hi, n: (pt[i], 0)),
        scratch_shapes=[pltpu.VMEM((2, tm * SUBLANES, LANES), F32), pltpu.VMEM((tm, d), F32),
                        pltpu.SemaphoreType.DMA((2,))],
    )
    return pl.pallas_call(
        functools.partial(_expert_kernel, n_tiles=n_tiles),
        grid_spec=grid_spec,
        out_shape=jax.ShapeDtypeStruct((r * SUBLANES, LANES), F32),
        compiler_params=pltpu.CompilerParams(
            dimension_semantics=("arbitrary",), vmem_limit_bytes=VMEM_LIMIT_BYTES),
        name="experts",
    )(ptile, pexp, plo, phi, npairs, tok_of, tok_of, h2, wg, wu, wd)


def _pair_table(cnt, n_tiles):
    tm = EXPERT_TILE
    n_steps = n_tiles + N_EXPERTS - 1
    offs = jnp.cumsum(cnt) - cnt
    first = offs // tm
    last = jnp.where(cnt > 0, (offs + cnt - 1) // tm, first - 1)
    n_e = last - first + 1
    ends = jnp.cumsum(n_e)
    total = ends[-1]
    p = jnp.minimum(jnp.arange(n_steps, dtype=I32), total - 1)
    e = jnp.sum((p[:, None] >= ends[None, :]).astype(I32), axis=1)
    tile = first[e] + (p - (ends[e] - n_e[e]))
    lo = jnp.clip(offs[e] - tile * tm, 0, tm)
    hi = jnp.clip(offs[e] + cnt[e] - tile * tm, 0, tm)
    return tile.astype(I32), e.astype(I32), lo.astype(I32), hi.astype(I32), total.reshape(1).astype(I32)


def _combine_kernel(p1_ref, p2_ref, p1n_ref, p2n_ref, gcol_ref, x1_ref, gfin_ref, os_ref, out_ref, buf_ref, sems):
    i = pl.program_id(0)
    n_steps = pl.num_programs(0)
    tm = x1_ref.shape[0]
    slot = i & 1

    def row_copy(src_row, j, sl, k):
        return pltpu.make_async_copy(os_ref.at[_token_rows(src_row)], buf_ref.at[sl, k, _token_rows(j)], sems.at[sl])

    def start_gather(r1_ref, r2_ref, sl):
        def body(j, carry):
            row_copy(r1_ref[j], j, sl, 0).start(priority=0)
            row_copy(r2_ref[j], j, sl, 1).start(priority=1)
            return carry
        lax.fori_loop(0, tm, body, 0, unroll=4)

    @pl.when(i == 0)
    def _():
        start_gather(p1_ref, p2_ref, slot)

    @pl.when(i + 1 < n_steps)
    def _():
        start_gather(p1n_ref, p2n_ref, 1 - slot)

    def wait_body(j, carry):
        row_copy(0, j, slot, 0).wait()
        row_copy(0, j, slot, 1).wait()
        return carry
    lax.fori_loop(0, tm, wait_body, 0, unroll=8)

    o1 = _load_token_major(buf_ref, (slot, 0), tm)
    o2 = _load_token_major(buf_ref, (slot, 1), tm)
    g = gcol_ref[...]
    y = g[:, 0:1] * o1 + g[:, 1:2] * o2
    out_ref[...] = _rms(x1_ref[...] + y, gfin_ref[...])


def _combine(p1, p2, gcol, x1, gfin, osorted):
    t, d = x1.shape
    tm = MOVE_TILE
    n_steps = t // tm
    cur = lambda: pl.BlockSpec((tm,), lambda i: (i,), memory_space=pltpu.SMEM)
    nxt = lambda: pl.BlockSpec((tm,), lambda i: (jnp.minimum(i + 1, n_steps - 1),), memory_space=pltpu.SMEM)
    return pl.pallas_call(
        _combine_kernel,
        grid=(n_steps,),
        in_specs=[cur(), cur(), nxt(), nxt(),
                  pl.BlockSpec((tm, LANES), lambda i: (i, 0)),
                  pl.BlockSpec((tm, d), lambda i: (i, 0)),
                  pl.BlockSpec((1, d), lambda i: (0, 0)),
                  pl.BlockSpec(memory_space=pl.ANY)],
        out_specs=pl.BlockSpec((tm, d), lambda i: (i, 0)),
        out_shape=jax.ShapeDtypeStruct((t, d), F32),
        scratch_shapes=[pltpu.VMEM((2, 2, tm * SUBLANES, LANES), F32), pltpu.SemaphoreType.DMA((2,))],
        compiler_params=pltpu.CompilerParams(dimension_semantics=("arbitrary",)),
        name="combine",
    )(p1, p2, p1, p2, gcol, x1, gfin, osorted)


def kernel(x, positions, norm_mix_gain, w_in, ret_gn_gain, sgu_norm_gain, sgu_w, sgu_b, w_out, norm_ffn_gain,
           w_router_group, b_router_group, w_router_expert, b_router_expert, w_expert_gate_up, w_expert_up,
           w_expert_down, final_norm_gain):
    bsz, seq, d = x.shape
    assert w_in.shape[0] == 1, "single-layer block"
    assert d == SUBLANES * LANES, "token-major layout needs one (8, 128) tile per token"
    assert seq % ATTN_TILE == 0 and (bsz * seq) % RANK_TILE == 0 and (2 * bsz * seq) % EXPERT_TILE == 0
    t = bsz * seq
    half = HEAD_DIM // 2

    pos4 = positions.reshape(bsz, seq // CHUNK, 1, CHUNK)
    invf = (ROPE_BASE ** (-jnp.arange(half, dtype=F32) * 2.0 / HEAD_DIM)).reshape(half, 1)
    wr = jnp.concatenate([
        jnp.transpose(w_router_expert[0], (0, 2, 1)).reshape(N_EXPERTS, d),
        w_router_group[0].T,
        jnp.zeros((ROUTER_ROWS - N_EXPERTS - N_GROUPS, d), F32)], axis=0)
    wrh = wr.astype(BF16)
    wrl = (wr - wrh.astype(F32)).astype(BF16)
    br = jnp.concatenate([b_router_expert[0].reshape(N_EXPERTS), b_router_group[0],
                          jnp.zeros((ROUTER_ROWS - N_EXPERTS - N_GROUPS,), F32)]).reshape(ROUTER_ROWS, 1)

    x1, h2, ids, gcol = _attn_router(
        x, pos4, norm_mix_gain[0].reshape(1, d), w_in[0].astype(BF16), ret_gn_gain[0].reshape(1, RET_WIDTH),
        sgu_norm_gain[0].reshape(1, SGU_WIDTH), sgu_w[0], sgu_b[0].reshape(SGU_GROUPS, CHUNK, 1),
        w_out[0].astype(BF16), invf, norm_ffn_gain[0].reshape(1, d), wrh, wrl, br)

    pos, cnt = _rank(ids)
    p1, p2 = pos[0], pos[1]
    tok_of = _invert(p1, p2)

    ff = w_expert_gate_up.shape[-1]
    ptile, pexp, plo, phi, npairs = _pair_table(cnt[:, 0].astype(I32), (2 * t) // EXPERT_TILE)
    osorted = _experts(ptile, pexp, plo, phi, npairs, tok_of, h2,
                       w_expert_gate_up[0].reshape(N_EXPERTS, d, ff).astype(BF16),
                       w_expert_up[0].reshape(N_EXPERTS, d, ff).astype(BF16),
                       w_expert_down[0].reshape(N_EXPERTS, ff, d).astype(BF16))

    out = _combine(p1, p2, gcol, x1, final_norm_gain.reshape(1, d), osorted)
    return out.reshape(bsz, seq, d)
```

```python
import functools
import math

import jax
import jax.numpy as jnp
from jax import lax
from jax.experimental import pallas as pl
from jax.experimental.pallas import tpu as pltpu

F32 = jnp.float32
BF16 = jnp.bfloat16
I32 = jnp.int32

RET_HEADS = 4
HEAD_DIM = 128
CHUNK = 128
SGU_GROUPS = 4
GROUP_DIM = 128
RET_WIDTH = RET_HEADS * HEAD_DIM
SGU_WIDTH = SGU_GROUPS * GROUP_DIM
N_SECTIONS = 6
N_GROUPS = 4
N_PER_GROUP = 8
N_EXPERTS = N_GROUPS * N_PER_GROUP
ROUTER_ROWS = 40
ROPE_BASE = 10000.0
EPS = 1e-6

LANES = 128
SUBLANES = 8
VMEM_LIMIT_BYTES = 56 * 1024 * 1024

ATTN_TILE = 512
RANK_TILE = 512
INVERT_TILE = 1024
MOVE_TILE = 256
EXPERT_TILE = 256
DMA_GROUPS = 8

_NT = (((1,), (1,)), ((), ()))
_TN = (((0,), (0,)), ((), ()))

_LOG_GAMMA = [math.log(1.0 - 2.0 ** (-5.0 - h)) for h in range(RET_HEADS)]
_CHUNK_DECAY = [math.exp(lg * CHUNK) for lg in _LOG_GAMMA]
_K_SCALE = HEAD_DIM ** -0.5


def _rms(x, gain):
    return x * lax.rsqrt(jnp.mean(x * x, axis=-1, keepdims=True) + EPS) * gain


def _gelu(x):
    return 0.5 * x * (1.0 + lax.erf(x * 0.7071067811865476))


def _sigmoid(x):
    return 1.0 / (1.0 + jnp.exp(-x))


def _token_rows(row):
    return pl.ds(pl.multiple_of(row * SUBLANES, SUBLANES), SUBLANES)


def _load_token_major(ref, lead, tok0, n_tok):
    return jnp.concatenate(
        [ref[lead + (pl.ds(tok0 * SUBLANES + sl, n_tok, stride=SUBLANES), slice(None))]
         for sl in range(SUBLANES)], axis=1)


def _store_token_major(ref, value):
    n_tok = value.shape[0]
    for sl in range(SUBLANES):
        ref[pl.ds(sl, n_tok, stride=SUBLANES), :] = value[:, sl * LANES:(sl + 1) * LANES]


def _attn_router_kernel(x_ref, pos_ref, gmix_ref, win_ref, gn_ref, sgn_ref, sw_ref, sb_ref, wout_ref,
                        invf_ref, gffn_ref, wrh_ref, wrl_ref, br_ref,
                        x1_ref, h2_ref, ids_ref, gcol_ref,
                        proj_ref, mix_ref, state_ref, dintra_ref, qdec_ref, kdec_ref, wc_ref, btab_ref):
    ts, d = x_ref.shape
    b = pl.program_id(0)
    s = pl.program_id(1)

    @pl.when((b == 0) & (s == 0))
    def _init_tables():
        i = lax.broadcasted_iota(I32, (CHUNK, CHUNK), 0)
        j = lax.broadcasted_iota(I32, (CHUNK, CHUNK), 1)
        diff = (i - j).astype(F32)
        fi = i.astype(F32)
        for h in range(RET_HEADS):
            lg = _LOG_GAMMA[h]
            dintra_ref[h] = jnp.where(i >= j, jnp.exp(lg * diff), 0.0) * _K_SCALE
            qdec_ref[h] = jnp.exp(lg * (fi + 1.0))
            kdec_ref[h] = jnp.exp(lg * (CHUNK - 1.0 - fi)) * _K_SCALE
        for g in range(SGU_GROUPS):
            wc_ref[g] = jnp.where(i >= j, sw_ref[g], 0.0).astype(BF16)
            btab_ref[g] = jnp.broadcast_to(sb_ref[g], (CHUNK, GROUP_DIM))

    @pl.when(s == 0)
    def _reset_state():
        state_ref[...] = jnp.zeros_like(state_ref)

    x = x_ref[...]
    h = _rms(x, gmix_ref[...]).astype(BF16)
    for sec in range(N_SECTIONS):
        proj_ref[sec] = jnp.dot(h, win_ref[:, sec * RET_WIDTH:(sec + 1) * RET_WIDTH],
                                preferred_element_type=F32)

    def chunk_body(c, carry):
        rows = pl.ds(pl.multiple_of(c * CHUNK, CHUNK), CHUNK)
        ang_t = invf_ref[...] * pos_ref[c].astype(F32)
        cos_t = jnp.cos(ang_t)
        sin_t = jnp.sin(ang_t)
        cosf = jnp.concatenate([cos_t, cos_t], axis=0).T
        sinf = jnp.concatenate([-sin_t, sin_t], axis=0).T
        for hd in range(RET_HEADS):
            hs = slice(hd * HEAD_DIM, (hd + 1) * HEAD_DIM)
            q = proj_ref[0, rows, hs]
            k = proj_ref[1, rows, hs]
            vb = proj_ref[2, rows, hs].astype(BF16)
            gate = proj_ref[3, rows, hs]
            qr = q * cosf + pltpu.roll(q, HEAD_DIM // 2, 1) * sinf
            kr = k * cosf + pltpu.roll(k, HEAD_DIM // 2, 1) * sinf
            scores = lax.dot_general(qr.astype(BF16), kr.astype(BF16), _NT,
                                     preferred_element_type=F32) * dintra_ref[hd]
            st = state_ref[hd]
            o = (jnp.dot(scores.astype(BF16), vb, preferred_element_type=F32)
                 + jnp.dot((qr * qdec_ref[hd]).astype(BF16), st.astype(BF16), preferred_element_type=F32))
            kv = lax.dot_general((kr * kdec_ref[hd]).astype(BF16), vb, _TN, preferred_element_type=F32)
            state_ref[hd] = st * _CHUNK_DECAY[hd] + kv
            d = o - jnp.mean(o, axis=-1, keepdims=True)
            on = d * lax.rsqrt(jnp.mean(d * d, axis=-1, keepdims=True) + EPS) * gn_ref[:, hs]
            mix_ref[rows, hs] = (gate * _sigmoid(gate) * on).astype(BF16)
        for g in range(SGU_GROUPS):
            gs = slice(g * GROUP_DIM, (g + 1) * GROUP_DIM)
            u = _gelu(proj_ref[4, rows, gs])
            v = _gelu(proj_ref[5, rows, gs])
            vn = v * lax.rsqrt(jnp.mean(v * v, axis=-1, keepdims=True) + EPS) * sgn_ref[:, gs]
            sg = jnp.dot(wc_ref[g], vn.astype(BF16), preferred_element_type=F32) + btab_ref[g]
            mix_ref[rows, pl.ds(RET_WIDTH + g * GROUP_DIM, GROUP_DIM)] = (u * sg).astype(BF16)
        return carry

    lax.fori_loop(0, ts // CHUNK, chunk_body, 0)

    x1 = x_ref[...] + jnp.dot(mix_ref[...], wout_ref[...], preferred_element_type=F32)
    x1_ref[...] = x1
    h2 = _rms(x1, gffn_ref[...])
    _store_token_major(h2_ref, h2)

    h2h = h2.astype(BF16)
    h2l = (h2 - h2h.astype(F32)).astype(BF16)
    wrh = wrh_ref[...]
    logits = (lax.dot_general(wrh, h2h, _NT, preferred_element_type=F32)
              + lax.dot_general(wrh, h2l, _NT, preferred_element_type=F32)
              + lax.dot_general(wrl_ref[...], h2h, _NT, preferred_element_type=F32)
              + br_ref[...])
    el = logits[0:N_EXPERTS]
    gl = logits[N_EXPERTS:N_EXPERTS + N_GROUPS]
    rg = lax.broadcasted_iota(I32, (N_GROUPS, ts), 0).astype(F32)
    gmax = jnp.max(gl, axis=0, keepdims=True)
    gidx = jnp.min(jnp.where(gl == gmax, rg, float(N_GROUPS)), axis=0, keepdims=True)
    gweight = 1.0 / jnp.sum(jnp.exp(gl - gmax), axis=0, keepdims=True)
    re_i = lax.broadcasted_iota(I32, (N_EXPERTS, ts), 0)
    re = re_i.astype(F32)
    in_group = (re_i // N_PER_GROUP).astype(F32) == gidx
    neg = -jnp.inf
    sel = jnp.where(in_group, el, neg)
    m1 = jnp.max(sel, axis=0, keepdims=True)
    i1 = jnp.min(jnp.where(sel == m1, re, float(N_EXPERTS)), axis=0, keepdims=True)
    sel2 = jnp.where(re == i1, neg, sel)
    m2 = jnp.max(sel2, axis=0, keepdims=True)
    i2 = jnp.min(jnp.where(sel2 == m2, re, float(N_EXPERTS)), axis=0, keepdims=True)
    e2 = jnp.exp(m2 - m1)
    w1 = 1.0 / (1.0 + e2)
    ids_ref[...] = jnp.concatenate([i1, i2], axis=0).astype(I32)
    gates_t = jnp.concatenate([gweight * w1, gweight * (e2 * w1), jnp.zeros((LANES - 2, ts), F32)], axis=0)
    gcol_ref[...] = gates_t.T


def _attn_router(x, pos4, gmix, win, gn, sgn, sw, sb, wout, invf, gffn, wrh, wrl, br):
    bsz, seq, d = x.shape
    ts = ATTN_TILE
    ns = seq // ts
    t = bsz * seq
    const2 = lambda b, s: (0, 0)
    const3 = lambda b, s: (0, 0, 0)
    single = dict(pipeline_mode=pl.Buffered(1))
    in_specs = [
        pl.BlockSpec((None, ts, d), lambda b, s: (b, s, 0)),
        pl.BlockSpec((None, ts // CHUNK, 1, CHUNK), lambda b, s: (b, s, 0, 0)),
        pl.BlockSpec((1, d), const2),
        pl.BlockSpec(win.shape, const2, **single),
        pl.BlockSpec((1, RET_WIDTH), const2),
        pl.BlockSpec((1, SGU_WIDTH), const2),
        pl.BlockSpec(sw.shape, const3),
        pl.BlockSpec(sb.shape, const3),
        pl.BlockSpec(wout.shape, const2, **single),
        pl.BlockSpec(invf.shape, const2),
        pl.BlockSpec((1, d), const2),
        pl.BlockSpec(wrh.shape, const2),
        pl.BlockSpec(wrl.shape, const2),
        pl.BlockSpec(br.shape, const2),
    ]
    tok = lambda b, s: (b * ns + s, 0)
    out_specs = [
        pl.BlockSpec((ts, d), tok),
        pl.BlockSpec((ts * SUBLANES, LANES), tok),
        pl.BlockSpec((2, ts), lambda b, s: (0, b * ns + s)),
        pl.BlockSpec((ts, LANES), tok),
    ]
    out_shape = [
        jax.ShapeDtypeStruct((t, d), F32),
        jax.ShapeDtypeStruct((t * SUBLANES, LANES), F32),
        jax.ShapeDtypeStruct((2, t), I32),
        jax.ShapeDtypeStruct((t, LANES), F32),
    ]
    scratch = [
        pltpu.VMEM((N_SECTIONS, ts, RET_WIDTH), F32),
        pltpu.VMEM((ts, RET_WIDTH + SGU_WIDTH), BF16),
        pltpu.VMEM((RET_HEADS, HEAD_DIM, HEAD_DIM), F32),
        pltpu.VMEM((RET_HEADS, CHUNK, CHUNK), F32),
        pltpu.VMEM((RET_HEADS, CHUNK, HEAD_DIM), F32),
        pltpu.VMEM((RET_HEADS, CHUNK, HEAD_DIM), F32),
        pltpu.VMEM((SGU_GROUPS, CHUNK, CHUNK), BF16),
        pltpu.VMEM((SGU_GROUPS, CHUNK, GROUP_DIM), F32),
    ]
    return pl.pallas_call(
        _attn_router_kernel,
        grid=(bsz, ns),
        in_specs=in_specs,
        out_specs=out_specs,
        out_shape=out_shape,
        scratch_shapes=scratch,
        compiler_params=pltpu.CompilerParams(
            dimension_semantics=("arbitrary", "arbitrary"), vmem_limit_bytes=VMEM_LIMIT_BYTES),
        name="attn_router",
    )(x, pos4, gmix, win, gn, sgn, sw, sb, wout, invf, gffn, wrh, wrl, br)


def _rank_kernel(ids_ref, pos_ref, cnt_ref, run_ref, before_ref):
    phase = pl.program_id(0)
    i = pl.program_id(1)
    tr = ids_ref.shape[1]
    ids = ids_ref[...]
    re = lax.broadcasted_iota(I32, (N_EXPERTS, tr), 0)
    hit1 = re == ids[0:1]
    hit2 = re == ids[1:2]
    onehot = jnp.where(hit1 | hit2, 1.0, 0.0)
    tile_cnt = jnp.sum(onehot, axis=1, keepdims=True)

    @pl.when((phase == 0) & (i == 0))
    def _zero():
        cnt_ref[...] = jnp.zeros_like(cnt_ref)

    @pl.when(phase == 0)
    def _count():
        cnt_ref[...] += jnp.broadcast_to(tile_cnt, cnt_ref.shape)

    @pl.when((phase == 1) & (i == 0))
    def _offsets():
        c = cnt_ref[...]
        c_hi = jnp.floor(c * (1.0 / 256.0))
        c_lo = c - 256.0 * c_hi
        a = lax.broadcasted_iota(I32, (N_EXPERTS, N_EXPERTS), 0)
        bb = lax.broadcasted_iota(I32, (N_EXPERTS, N_EXPERTS), 1)
        lower = jnp.where(bb < a, 1.0, 0.0).astype(BF16)
        run_ref[...] = (256.0 * jnp.dot(lower, c_hi.astype(BF16), preferred_element_type=F32)
                        + jnp.dot(lower, c_lo.astype(BF16), preferred_element_type=F32))
        ta = lax.broadcasted_iota(I32, (tr, tr), 0)
        tb = lax.broadcasted_iota(I32, (tr, tr), 1)
        before_ref[...] = jnp.where(ta < tb, 1.0, 0.0).astype(BF16)

    @pl.when(phase == 1)
    def _rank():
        prefix = jnp.dot(onehot.astype(BF16), before_ref[...], preferred_element_type=F32)
        posmat = run_ref[:, 0:1] + prefix
        p1 = jnp.sum(jnp.where(hit1, posmat, 0.0), axis=0, keepdims=True)
        p2 = jnp.sum(jnp.where(hit2, posmat, 0.0), axis=0, keepdims=True)
        pos_ref[...] = jnp.concatenate([p1, p2], axis=0).astype(I32)
        run_ref[...] += jnp.broadcast_to(tile_cnt, run_ref.shape)


def _rank(ids):
    t = ids.shape[1]
    tr = RANK_TILE
    return pl.pallas_call(
        _rank_kernel,
        grid=(2, t // tr),
        in_specs=[pl.BlockSpec((2, tr), lambda p, i: (0, i))],
        out_specs=[pl.BlockSpec((2, tr), lambda p, i: (0, i * p)),
                   pl.BlockSpec((N_EXPERTS, LANES), lambda p, i: (0, 0))],
        out_shape=[jax.ShapeDtypeStruct((2, t), I32), jax.ShapeDtypeStruct((N_EXPERTS, LANES), F32)],
        scratch_shapes=[pltpu.VMEM((N_EXPERTS, LANES), F32), pltpu.VMEM((tr, tr), BF16)],
        compiler_params=pltpu.CompilerParams(dimension_semantics=("arbitrary", "arbitrary")),
        name="rank",
    )(ids)


def _invert_kernel(p1_ref, p2_ref, tok_ref):
    n = p1_ref.shape[0]
    base = pl.program_id(0) * n

    def body(j, carry):
        tok_ref[p1_ref[j]] = base + j
        tok_ref[p2_ref[j]] = base + j
        return carry

    lax.fori_loop(0, n, body, 0, unroll=8)


def _invert(p1, p2):
    t = p1.shape[0]
    n = INVERT_TILE
    smem = lambda: pl.BlockSpec((n,), lambda i: (i,), memory_space=pltpu.SMEM)
    return pl.pallas_call(
        _invert_kernel,
        grid=(t // n,),
        in_specs=[smem(), smem()],
        out_specs=pl.BlockSpec(memory_space=pltpu.SMEM),
        out_shape=jax.ShapeDtypeStruct((2 * t,), I32),
        compiler_params=pltpu.CompilerParams(dimension_semantics=("arbitrary",)),
        name="invert",
    )(p1, p2)


def _expert_kernel(ptile_ref, pexp_ref, plo_ref, phi_ref, np_ref, tok_ref, toknext_ref, h2_ref,
                   wg_ref, wu_ref, wd_ref, o_ref, xbuf_ref, acc_ref, sems, *, n_tiles):
    i = pl.program_id(0)
    n_steps = pl.num_programs(0)
    tm = acc_ref.shape[0]
    tile = ptile_ref[i]
    n_pairs = np_ref[0]
    valid = i < n_pairs
    first = (i == 0) | (tile != ptile_ref[jnp.maximum(i - 1, 0)])
    last = (i == n_pairs - 1) | (ptile_ref[jnp.minimum(i + 1, n_steps - 1)] != tile)
    parity = tile & 1
    per_group = tm // DMA_GROUPS

    def row_copy(src_row, j, sl):
        return pltpu.make_async_copy(h2_ref.at[_token_rows(src_row)], xbuf_ref.at[sl, _token_rows(j)], sems.at[sl])

    def start_group(idx_ref, sl, g):
        for j in range(g * per_group, (g + 1) * per_group):
            row_copy(idx_ref[j], j, sl).start(priority=j % 2)

    def wait_tile(sl):
        pltpu.make_async_copy(h2_ref.at[pl.ds(0, tm * SUBLANES)], xbuf_ref.at[sl], sems.at[sl]).wait()

    def compute(sl, is_first):
        prefetch = (lambda g: start_group(toknext_ref, 1 - sl, g)) if is_first else (lambda g: None)
        xt = _load_token_major(xbuf_ref, (sl,), 0, tm).astype(BF16)
        prefetch(0)
        prefetch(1)
        a = jnp.dot(xt, wg_ref[...], preferred_element_type=F32)
        prefetch(2)
        prefetch(3)
        bb = jnp.dot(xt, wu_ref[...], preferred_element_type=F32)
        prefetch(4)
        prefetch(5)
        row = lax.broadcasted_iota(I32, (tm, 1), 0)
        mine = (row >= plo_ref[i]) & (row < phi_ref[i])
        act = jnp.where(mine, a * _sigmoid(a) * bb, 0.0)
        o = jnp.dot(act.astype(BF16), wd_ref[...], preferred_element_type=F32)
        prefetch(6)
        prefetch(7)
        acc_ref[...] = o if is_first else acc_ref[...] + o

        @pl.when(last)
        def _():
            _store_token_major(o_ref, acc_ref[...])

    @pl.when(i == 0)
    def _():
        for g in range(DMA_GROUPS):
            start_group(tok_ref, 0, g)

    for sl in (0, 1):
        @pl.when(valid & first & (parity == sl))
        def _():
            wait_tile(sl)
            compute(sl, True)

        @pl.when(valid & jnp.logical_not(first) & (parity == sl))
        def _():
            compute(sl, False)

    @pl.when(i == n_pairs - 1)
    def _():
        wait_tile(1 - parity)


def _experts(ptile, pexp, plo, phi, npairs, tok_of, h2, wg, wu, wd):
    d = SUBLANES * LANES
    r = tok_of.shape[0]
    f = wg.shape[-1]
    tm = EXPERT_TILE
    n_tiles = r // tm
    n_steps = ptile.shape[0]
    grid_spec = pltpu.PrefetchScalarGridSpec(
        num_scalar_prefetch=5,
        grid=(n_steps,),
        in_specs=[
            pl.BlockSpec((tm,), lambda i, pt, pe, lo, hi, n: (pt[i],), memory_space=pltpu.SMEM),
            pl.BlockSpec((tm,), lambda i, pt, pe, lo, hi, n: (jnp.minimum(pt[i] + 1, n_tiles - 1),),
                         memory_space=pltpu.SMEM),
            pl.BlockSpec(memory_space=pl.ANY),
            pl.BlockSpec((None, d, f), lambda i, pt, pe, lo, hi, n: (pe[i], 0, 0)),
            pl.BlockSpec((None, d, f), lambda i, pt, pe, lo, hi, n: (pe[i], 0, 0)),
            pl.BlockSpec((None, f, d), lambda i, pt, pe, lo, hi, n: (pe[i], 0, 0)),
        ],
        out_specs=pl.BlockSpec((tm * SUBLANES, LANES), lambda i, pt, pe, lo, hi, n: (pt[i], 0)),
        scratch_shapes=[pltpu.VMEM((2, tm * SUBLANES, LANES), F32), pltpu.VMEM((tm, d), F32),
                        pltpu.SemaphoreType.DMA((2,))],
    )
    return pl.pallas_call(
        functools.partial(_expert_kernel, n_tiles=n_tiles),
        grid_spec=grid_spec,
        out_shape=jax.ShapeDtypeStruct((r * SUBLANES, LANES), F32),
        compiler_params=pltpu.CompilerParams(
            dimension_semantics=("arbitrary",), vmem_limit_bytes=VMEM_LIMIT_BYTES),
        name="experts",
    )(ptile, pexp, plo, phi, npairs, tok_of, tok_of, h2, wg, wu, wd)


def _pair_table(cnt, n_tiles):
    tm = EXPERT_TILE
    n_steps = n_tiles + N_EXPERTS - 1
    offs = jnp.cumsum(cnt) - cnt
    first = offs // tm
    last = jnp.where(cnt > 0, (offs + cnt - 1) // tm, first - 1)
    n_e = last - first + 1
    ends = jnp.cumsum(n_e)
    total = ends[-1]
    p = jnp.minimum(jnp.arange(n_steps, dtype=I32), total - 1)
    e = jnp.sum((p[:, None] >= ends[None, :]).astype(I32), axis=1)
    tile = first[e] + (p - (ends[e] - n_e[e]))
    lo = jnp.clip(offs[e] - tile * tm, 0, tm)
    hi = jnp.clip(offs[e] + cnt[e] - tile * tm, 0, tm)
    return tile.astype(I32), e.astype(I32), lo.astype(I32), hi.astype(I32), total.reshape(1).astype(I32)


def _combine_kernel(p1_ref, p2_ref, p1n_ref, p2n_ref, gcol_ref, x1_ref, gfin_ref, os_ref, out_ref, buf_ref, sems):
    i = pl.program_id(0)
    n_steps = pl.num_programs(0)
    tm = x1_ref.shape[0]
    per_group = tm // DMA_GROUPS

    def row_copy(src_row, j, sl, k):
        return pltpu.make_async_copy(os_ref.at[_token_rows(src_row)], buf_ref.at[sl, k, _token_rows(j)], sems.at[sl])

    def start_group(r1_ref, r2_ref, sl, g):
        for j in range(g * per_group, (g + 1) * per_group):
            row_copy(r1_ref[j], j, sl, 0).start(priority=0)
            row_copy(r2_ref[j], j, sl, 1).start(priority=1)

    def wait_tile(sl):
        for k in (0, 1):
            pltpu.make_async_copy(os_ref.at[pl.ds(0, tm * SUBLANES)], buf_ref.at[sl, k], sems.at[sl]).wait()

    @pl.when(i == 0)
    def _():
        for g in range(DMA_GROUPS):
            start_group(p1_ref, p2_ref, 0, g)

    def step(sl):
        wait_tile(sl)
        for g in range(DMA_GROUPS):
            rows = slice(g * per_group, (g + 1) * per_group)
            o1 = _load_token_major(buf_ref, (sl, 0), g * per_group, per_group)
            o2 = _load_token_major(buf_ref, (sl, 1), g * per_group, per_group)
            gates = gcol_ref[rows, :]
            y = gates[:, 0:1] * o1 + gates[:, 1:2] * o2
            out_ref[rows, :] = _rms(x1_ref[rows, :] + y, gfin_ref[...])
            start_group(p1n_ref, p2n_ref, 1 - sl, g)

    for sl in (0, 1):
        @pl.when((i & 1) == sl)
        def _():
            step(sl)

    @pl.when(i == n_steps - 1)
    def _():
        wait_tile(1 - (i & 1))


def _combine(p1, p2, gcol, x1, gfin, osorted):
    t, d = x1.shape
    tm = MOVE_TILE
    n_steps = t // tm
    cur = lambda: pl.BlockSpec((tm,), lambda i: (i,), memory_space=pltpu.SMEM)
    nxt = lambda: pl.BlockSpec((tm,), lambda i: (jnp.minimum(i + 1, n_steps - 1),), memory_space=pltpu.SMEM)
    return pl.pallas_call(
        _combine_kernel,
        grid=(n_steps,),
        in_specs=[cur(), cur(), nxt(), nxt(),
                  pl.BlockSpec((tm, LANES), lambda i: (i, 0)),
                  pl.BlockSpec((tm, d), lambda i: (i, 0)),
                  pl.BlockSpec((1, d), lambda i: (0, 0)),
                  pl.BlockSpec(memory_space=pl.ANY)],
        out_specs=pl.BlockSpec((tm, d), lambda i: (i, 0)),
        out_shape=jax.ShapeDtypeStruct((t, d), F32),
        scratch_shapes=[pltpu.VMEM((2, 2, tm * SUBLANES, LANES), F32), pltpu.SemaphoreType.DMA((2,))],
        compiler_params=pltpu.CompilerParams(dimension_semantics=("arbitrary",)),
        name="combine",
    )(p1, p2, p1, p2, gcol, x1, gfin, osorted)


def kernel(x, positions, norm_mix_gain, w_in, ret_gn_gain, sgu_norm_gain, sgu_w, sgu_b, w_out, norm_ffn_gain,
           w_router_group, b_router_group, w_router_expert, b_router_expert, w_expert_gate_up, w_expert_up,
           w_expert_down, final_norm_gain):
    bsz, seq, d = x.shape
    assert w_in.shape[0] == 1, "single-layer block"
    assert d == SUBLANES * LANES, "token-major layout needs one (8, 128) tile per token"
    assert seq % ATTN_TILE == 0 and (bsz * seq) % RANK_TILE == 0 and (2 * bsz * seq) % EXPERT_TILE == 0
    t = bsz * seq
    half = HEAD_DIM // 2

    pos4 = positions.reshape(bsz, seq // CHUNK, 1, CHUNK)
    invf = (ROPE_BASE ** (-jnp.arange(half, dtype=F32) * 2.0 / HEAD_DIM)).reshape(half, 1)
    wr = jnp.concatenate([
        jnp.transpose(w_router_expert[0], (0, 2, 1)).reshape(N_EXPERTS, d),
        w_router_group[0].T,
        jnp.zeros((ROUTER_ROWS - N_EXPERTS - N_GROUPS, d), F32)], axis=0)
    wrh = wr.astype(BF16)
    wrl = (wr - wrh.astype(F32)).astype(BF16)
    br = jnp.concatenate([b_router_expert[0].reshape(N_EXPERTS), b_router_group[0],
                          jnp.zeros((ROUTER_ROWS - N_EXPERTS - N_GROUPS,), F32)]).reshape(ROUTER_ROWS, 1)

    x1, h2, ids, gcol = _attn_router(
        x, pos4, norm_mix_gain[0].reshape(1, d), w_in[0].astype(BF16), ret_gn_gain[0].reshape(1, RET_WIDTH),
        sgu_norm_gain[0].reshape(1, SGU_WIDTH), sgu_w[0], sgu_b[0].reshape(SGU_GROUPS, CHUNK, 1),
        w_out[0].astype(BF16), invf, norm_ffn_gain[0].reshape(1, d), wrh, wrl, br)

    pos, cnt = _rank(ids)
    p1, p2 = pos[0], pos[1]
    tok_of = _invert(p1, p2)

    ff = w_expert_gate_up.shape[-1]
    ptile, pexp, plo, phi, npairs = _pair_table(cnt[:, 0].astype(I32), (2 * t) // EXPERT_TILE)
    osorted = _experts(ptile, pexp, plo, phi, npairs, tok_of, h2,
                       w_expert_gate_up[0].reshape(N_EXPERTS, d, ff).astype(BF16),
                       w_expert_up[0].reshape(N_EXPERTS, d, ff).astype(BF16),
                       w_expert_down[0].reshape(N_EXPERTS, ff, d).astype(BF16))

    out = _combine(p1, p2, gcol, x1, final_norm_gain.reshape(1, d), osorted)
    return out.reshape(bsz, seq, d)
```

```python
import functools
import math

import jax
import jax.numpy as jnp
from jax import lax
from jax.experimental import pallas as pl
from jax.experimental.pallas import tpu as pltpu

F32 = jnp.float32
BF16 = jnp.bfloat16
I32 = jnp.int32

RET_HEADS = 4
HEAD_DIM = 128
CHUNK = 128
SGU_GROUPS = 4
GROUP_DIM = 128
RET_WIDTH = RET_HEADS * HEAD_DIM
SGU_WIDTH = SGU_GROUPS * GROUP_DIM
N_SECTIONS = 6
N_GROUPS = 4
N_PER_GROUP = 8
N_EXPERTS = N_GROUPS * N_PER_GROUP
ROUTER_ROWS = 40
ROPE_BASE = 10000.0
EPS = 1e-6

LANES = 128
SUBLANES = 8
VMEM_LIMIT_BYTES = 56 * 1024 * 1024

ATTN_TILE = 512
RANK_TILE = 512
INVERT_TILE = 1024
MOVE_TILE = 512
EXPERT_TILE = 256
DMA_GROUPS = 8

_NT = (((1,), (1,)), ((), ()))
_TN = (((0,), (0,)), ((), ()))

_LOG_GAMMA = [math.log(1.0 - 2.0 ** (-5.0 - h)) for h in range(RET_HEADS)]
_CHUNK_DECAY = [math.exp(lg * CHUNK) for lg in _LOG_GAMMA]
_K_SCALE = HEAD_DIM ** -0.5


def _rms(x, gain):
    return x * lax.rsqrt(jnp.mean(x * x, axis=-1, keepdims=True) + EPS) * gain


def _gelu(x):
    return 0.5 * x * (1.0 + lax.erf(x * 0.7071067811865476))


def _sigmoid(x):
    return 1.0 / (1.0 + jnp.exp(-x))


def _token_rows(row):
    return pl.ds(pl.multiple_of(row * SUBLANES, SUBLANES), SUBLANES)


def _load_token_major(ref, lead, tok0, n_tok):
    return jnp.concatenate(
        [ref[lead + (pl.ds(tok0 * SUBLANES + sl, n_tok, stride=SUBLANES), slice(None))]
         for sl in range(SUBLANES)], axis=1)


def _store_token_major(ref, value):
    n_tok = value.shape[0]
    for sl in range(SUBLANES):
        ref[pl.ds(sl, n_tok, stride=SUBLANES), :] = value[:, sl * LANES:(sl + 1) * LANES]


def _attn_router_kernel(x_ref, pos_ref, gmix_ref, win_ref, gn_ref, sgn_ref, sw_ref, sb_ref, wout_ref,
                        invf_ref, gffn_ref, wrh_ref, wrl_ref, br_ref,
                        x1_ref, h2_ref, ids_ref, gcol_ref,
                        proj_ref, mix_ref, state_ref, dintra_ref, qdec_ref, kdec_ref, wc_ref, btab_ref):
    ts, d = x_ref.shape
    b = pl.program_id(0)
    s = pl.program_id(1)

    @pl.when((b == 0) & (s == 0))
    def _init_tables():
        i = lax.broadcasted_iota(I32, (CHUNK, CHUNK), 0)
        j = lax.broadcasted_iota(I32, (CHUNK, CHUNK), 1)
        diff = (i - j).astype(F32)
        fi = i.astype(F32)
        for h in range(RET_HEADS):
            lg = _LOG_GAMMA[h]
            dintra_ref[h] = jnp.where(i >= j, jnp.exp(lg * diff), 0.0) * _K_SCALE
            qdec_ref[h] = jnp.exp(lg * (fi + 1.0))
            kdec_ref[h] = jnp.exp(lg * (CHUNK - 1.0 - fi)) * _K_SCALE
        for g in range(SGU_GROUPS):
            wc_ref[g] = jnp.where(i >= j, sw_ref[g], 0.0).astype(BF16)
            btab_ref[g] = jnp.broadcast_to(sb_ref[g], (CHUNK, GROUP_DIM))

    @pl.when(s == 0)
    def _reset_state():
        state_ref[...] = jnp.zeros_like(state_ref)

    x = x_ref[...]
    h = _rms(x, gmix_ref[...]).astype(BF16)
    for sec in range(N_SECTIONS):
        proj_ref[sec] = jnp.dot(h, win_ref[:, sec * RET_WIDTH:(sec + 1) * RET_WIDTH],
                                preferred_element_type=F32)

    def chunk_body(c, carry):
        rows = pl.ds(pl.multiple_of(c * CHUNK, CHUNK), CHUNK)
        ang_t = invf_ref[...] * pos_ref[c].astype(F32)
        cos_t = jnp.cos(ang_t)
        sin_t = jnp.sin(ang_t)
        cosf = jnp.concatenate([cos_t, cos_t], axis=0).T
        sinf = jnp.concatenate([-sin_t, sin_t], axis=0).T
        for hd in range(RET_HEADS):
            hs = slice(hd * HEAD_DIM, (hd + 1) * HEAD_DIM)
            q = proj_ref[0, rows, hs]
            k = proj_ref[1, rows, hs]
            vb = proj_ref[2, rows, hs].astype(BF16)
            gate = proj_ref[3, rows, hs]
            qr = q * cosf + pltpu.roll(q, HEAD_DIM // 2, 1) * sinf
            kr = k * cosf + pltpu.roll(k, HEAD_DIM // 2, 1) * sinf
            scores = lax.dot_general(qr.astype(BF16), kr.astype(BF16), _NT,
                                     preferred_element_type=F32) * dintra_ref[hd]
            st = state_ref[hd]
            o = (jnp.dot(scores.astype(BF16), vb, preferred_element_type=F32)
                 + jnp.dot((qr * qdec_ref[hd]).astype(BF16), st.astype(BF16), preferred_element_type=F32))
            kv = lax.dot_general((kr * kdec_ref[hd]).astype(BF16), vb, _TN, preferred_element_type=F32)
            state_ref[hd] = st * _CHUNK_DECAY[hd] + kv
            d = o - jnp.mean(o, axis=-1, keepdims=True)
            on = d * lax.rsqrt(jnp.mean(d * d, axis=-1, keepdims=True) + EPS) * gn_ref[:, hs]
            mix_ref[rows, hs] = (gate * _sigmoid(gate) * on).astype(BF16)
        for g in range(SGU_GROUPS):
            gs = slice(g * GROUP_DIM, (g + 1) * GROUP_DIM)
            u = _gelu(proj_ref[4, rows, gs])
            v = _gelu(proj_ref[5, rows, gs])
            vn = v * lax.rsqrt(jnp.mean(v * v, axis=-1, keepdims=True) + EPS) * sgn_ref[:, gs]
            sg = jnp.dot(wc_ref[g], vn.astype(BF16), preferred_element_type=F32) + btab_ref[g]
            mix_ref[rows, pl.ds(RET_WIDTH + g * GROUP_DIM, GROUP_DIM)] = (u * sg).astype(BF16)
        return carry

    lax.fori_loop(0, ts // CHUNK, chunk_body, 0)

    x1 = x_ref[...] + jnp.dot(mix_ref[...], wout_ref[...], preferred_element_type=F32)
    x1_ref[...] = x1
    h2 = _rms(x1, gffn_ref[...])
    _store_token_major(h2_ref, h2)

    h2h = h2.astype(BF16)
    h2l = (h2 - h2h.astype(F32)).astype(BF16)
    wrh = wrh_ref[...]
    logits = (lax.dot_general(wrh, h2h, _NT, preferred_element_type=F32)
              + lax.dot_general(wrh, h2l, _NT, preferred_element_type=F32)
              + lax.dot_general(wrl_ref[...], h2h, _NT, preferred_element_type=F32)
              + br_ref[...])
    el = logits[0:N_EXPERTS]
    gl = logits[N_EXPERTS:N_EXPERTS + N_GROUPS]
    rg = lax.broadcasted_iota(I32, (N_GROUPS, ts), 0).astype(F32)
    gmax = jnp.max(gl, axis=0, keepdims=True)
    gidx = jnp.min(jnp.where(gl == gmax, rg, float(N_GROUPS)), axis=0, keepdims=True)
    gweight = 1.0 / jnp.sum(jnp.exp(gl - gmax), axis=0, keepdims=True)
    re_i = lax.broadcasted_iota(I32, (N_EXPERTS, ts), 0)
    re = re_i.astype(F32)
    in_group = (re_i // N_PER_GROUP).astype(F32) == gidx
    neg = -jnp.inf
    sel = jnp.where(in_group, el, neg)
    m1 = jnp.max(sel, axis=0, keepdims=True)
    i1 = jnp.min(jnp.where(sel == m1, re, float(N_EXPERTS)), axis=0, keepdims=True)
    sel2 = jnp.where(re == i1, neg, sel)
    m2 = jnp.max(sel2, axis=0, keepdims=True)
    i2 = jnp.min(jnp.where(sel2 == m2, re, float(N_EXPERTS)), axis=0, keepdims=True)
    e2 = jnp.exp(m2 - m1)
    w1 = 1.0 / (1.0 + e2)
    ids_ref[...] = jnp.concatenate([i1, i2], axis=0).astype(I32)
    gates_t = jnp.concatenate([gweight * w1, gweight * (e2 * w1), jnp.zeros((LANES - 2, ts), F32)], axis=0)
    gcol_ref[...] = gates_t.T


def _attn_router(x, pos4, gmix, win, gn, sgn, sw, sb, wout, invf, gffn, wrh, wrl, br):
    bsz, seq, d = x.shape
    ts = ATTN_TILE
    ns = seq // ts
    t = bsz * seq
    const2 = lambda b, s: (0, 0)
    const3 = lambda b, s: (0, 0, 0)
    single = dict(pipeline_mode=pl.Buffered(1))
    in_specs = [
        pl.BlockSpec((None, ts, d), lambda b, s: (b, s, 0)),
        pl.BlockSpec((None, ts // CHUNK, 1, CHUNK), lambda b, s: (b, s, 0, 0)),
        pl.BlockSpec((1, d), const2),
        pl.BlockSpec(win.shape, const2, **single),
        pl.BlockSpec((1, RET_WIDTH), const2),
        pl.BlockSpec((1, SGU_WIDTH), const2),
        pl.BlockSpec(sw.shape, const3),
        pl.BlockSpec(sb.shape, const3),
        pl.BlockSpec(wout.shape, const2, **single),
        pl.BlockSpec(invf.shape, const2),
        pl.BlockSpec((1, d), const2),
        pl.BlockSpec(wrh.shape, const2),
        pl.BlockSpec(wrl.shape, const2),
        pl.BlockSpec(br.shape, const2),
    ]
    tok = lambda b, s: (b * ns + s, 0)
    out_specs = [
        pl.BlockSpec((ts, d), tok),
        pl.BlockSpec((ts * SUBLANES, LANES), tok),
        pl.BlockSpec((2, ts), lambda b, s: (0, b * ns + s)),
        pl.BlockSpec((ts, LANES), tok),
    ]
    out_shape = [
        jax.ShapeDtypeStruct((t, d), F32),
        jax.ShapeDtypeStruct((t * SUBLANES, LANES), F32),
        jax.ShapeDtypeStruct((2, t), I32),
        jax.ShapeDtypeStruct((t, LANES), F32),
    ]
    scratch = [
        pltpu.VMEM((N_SECTIONS, ts, RET_WIDTH), F32),
        pltpu.VMEM((ts, RET_WIDTH + SGU_WIDTH), BF16),
        pltpu.VMEM((RET_HEADS, HEAD_DIM, HEAD_DIM), F32),
        pltpu.VMEM((RET_HEADS, CHUNK, CHUNK), F32),
        pltpu.VMEM((RET_HEADS, CHUNK, HEAD_DIM), F32),
        pltpu.VMEM((RET_HEADS, CHUNK, HEAD_DIM), F32),
        pltpu.VMEM((SGU_GROUPS, CHUNK, CHUNK), BF16),
        pltpu.VMEM((SGU_GROUPS, CHUNK, GROUP_DIM), F32),
    ]
    return pl.pallas_call(
        _attn_router_kernel,
        grid=(bsz, ns),
        in_specs=in_specs,
        out_specs=out_specs,
        out_shape=out_shape,
        scratch_shapes=scratch,
        compiler_params=pltpu.CompilerParams(
            dimension_semantics=("arbitrary", "arbitrary"), vmem_limit_bytes=VMEM_LIMIT_BYTES),
        name="attn_router",
    )(x, pos4, gmix, win, gn, sgn, sw, sb, wout, invf, gffn, wrh, wrl, br)


def _rank_kernel(ids_ref, pos_ref, cnt_ref, run_ref, before_ref):
    phase = pl.program_id(0)
    i = pl.program_id(1)
    tr = ids_ref.shape[1]
    ids = ids_ref[...]
    re = lax.broadcasted_iota(I32, (N_EXPERTS, tr), 0)
    hit1 = re == ids[0:1]
    hit2 = re == ids[1:2]
    onehot = jnp.where(hit1 | hit2, 1.0, 0.0)
    tile_cnt = jnp.sum(onehot, axis=1, keepdims=True)

    @pl.when((phase == 0) & (i == 0))
    def _zero():
        cnt_ref[...] = jnp.zeros_like(cnt_ref)

    @pl.when(phase == 0)
    def _count():
        cnt_ref[...] += jnp.broadcast_to(tile_cnt, cnt_ref.shape)

    @pl.when((phase == 1) & (i == 0))
    def _offsets():
        c = cnt_ref[...]
        c_hi = jnp.floor(c * (1.0 / 256.0))
        c_lo = c - 256.0 * c_hi
        a = lax.broadcasted_iota(I32, (N_EXPERTS, N_EXPERTS), 0)
        bb = lax.broadcasted_iota(I32, (N_EXPERTS, N_EXPERTS), 1)
        lower = jnp.where(bb < a, 1.0, 0.0).astype(BF16)
        run_ref[...] = (256.0 * jnp.dot(lower, c_hi.astype(BF16), preferred_element_type=F32)
                        + jnp.dot(lower, c_lo.astype(BF16), preferred_element_type=F32))
        ta = lax.broadcasted_iota(I32, (tr, tr), 0)
        tb = lax.broadcasted_iota(I32, (tr, tr), 1)
        before_ref[...] = jnp.where(ta < tb, 1.0, 0.0).astype(BF16)

    @pl.when(phase == 1)
    def _rank():
        prefix = jnp.dot(onehot.astype(BF16), before_ref[...], preferred_element_type=F32)
        posmat = run_ref[:, 0:1] + prefix
        p1 = jnp.sum(jnp.where(hit1, posmat, 0.0), axis=0, keepdims=True)
        p2 = jnp.sum(jnp.where(hit2, posmat, 0.0), axis=0, keepdims=True)
        pos_ref[...] = jnp.concatenate([p1, p2], axis=0).astype(I32)
        run_ref[...] += jnp.broadcast_to(tile_cnt, run_ref.shape)


def _rank(ids):
    t = ids.shape[1]
    tr = RANK_TILE
    return pl.pallas_call(
        _rank_kernel,
        grid=(2, t // tr),
        in_specs=[pl.BlockSpec((2, tr), lambda p, i: (0, i))],
        out_specs=[pl.BlockSpec((2, tr), lambda p, i: (0, i * p)),
                   pl.BlockSpec((N_EXPERTS, LANES), lambda p, i: (0, 0))],
        out_shape=[jax.ShapeDtypeStruct((2, t), I32), jax.ShapeDtypeStruct((N_EXPERTS, LANES), F32)],
        scratch_shapes=[pltpu.VMEM((N_EXPERTS, LANES), F32), pltpu.VMEM((tr, tr), BF16)],
        compiler_params=pltpu.CompilerParams(dimension_semantics=("arbitrary", "arbitrary")),
        name="rank",
    )(ids)


def _invert_kernel(p1_ref, p2_ref, tok_ref):
    n = p1_ref.shape[0]
    base = pl.program_id(0) * n

    n_tok = pl.num_programs(0) * n

    def body(j, carry):
        tok_ref[p1_ref[j]] = base + j
        tok_ref[p2_ref[j]] = base + j + n_tok
        return carry

    lax.fori_loop(0, n, body, 0, unroll=8)


def _invert(p1, p2):
    t = p1.shape[0]
    n = INVERT_TILE
    smem = lambda: pl.BlockSpec((n,), lambda i: (i,), memory_space=pltpu.SMEM)
    return pl.pallas_call(
        _invert_kernel,
        grid=(t // n,),
        in_specs=[smem(), smem()],
        out_specs=pl.BlockSpec(memory_space=pltpu.SMEM),
        out_shape=jax.ShapeDtypeStruct((2 * t,), I32),
        compiler_params=pltpu.CompilerParams(dimension_semantics=("arbitrary",)),
        name="invert",
    )(p1, p2)


def _expert_kernel(ptile_ref, pexp_ref, plo_ref, phi_ref, np_ref, tok_ref, toknext_ref, tokprev_ref, h2_ref,
                   wg_ref, wu_ref, wd_ref, y_ref, xbuf_ref, acc_ref, stage_ref, gsems, ssems, *, n_tiles, n_tok):
    i = pl.program_id(0)
    n_steps = pl.num_programs(0)
    tm = acc_ref.shape[0]
    tile = ptile_ref[i]
    n_pairs = np_ref[0]
    valid = i < n_pairs
    first = (i == 0) | (tile != ptile_ref[jnp.maximum(i - 1, 0)])
    last = (i == n_pairs - 1) | (ptile_ref[jnp.minimum(i + 1, n_steps - 1)] != tile)
    parity = tile & 1
    per_group = tm // DMA_GROUPS

    def gather_group(idx_ref, sl, g):
        for j in range(g * per_group, (g + 1) * per_group):
            v = idx_ref[j]
            src = v - jnp.where(v >= n_tok, n_tok, 0)
            pltpu.make_async_copy(h2_ref.at[_token_rows(src)], xbuf_ref.at[sl, _token_rows(j)],
                                  gsems.at[sl]).start(priority=j % 2)

    def scatter_group(idx_ref, sl, g):
        for j in range(g * per_group, (g + 1) * per_group):
            pltpu.make_async_copy(stage_ref.at[sl, _token_rows(j)], y_ref.at[_token_rows(idx_ref[j])],
                                  ssems.at[sl]).start(priority=j % 2)

    def wait_gather(sl):
        pltpu.make_async_copy(h2_ref.at[pl.ds(0, tm * SUBLANES)], xbuf_ref.at[sl], gsems.at[sl]).wait()

    def wait_scatter(sl):
        pltpu.make_async_copy(stage_ref.at[sl], y_ref.at[pl.ds(0, tm * SUBLANES)], ssems.at[sl]).wait()

    def compute(sl, is_first, has_prev):
        def dma_work(g):
            if is_first:
                gather_group(toknext_ref, 1 - sl, g)
                if has_prev:
                    scatter_group(tokprev_ref, 1 - sl, g)
        xt = _load_token_major(xbuf_ref, (sl,), 0, tm).astype(BF16)
        dma_work(0)
        dma_work(1)
        a = jnp.dot(xt, wg_ref[...], preferred_element_type=F32)
        dma_work(2)
        dma_work(3)
        bb = jnp.dot(xt, wu_ref[...], preferred_element_type=F32)
        dma_work(4)
        dma_work(5)
        row = lax.broadcasted_iota(I32, (tm, 1), 0)
        mine = (row >= plo_ref[i]) & (row < phi_ref[i])
        act = jnp.where(mine, a * _sigmoid(a) * bb, 0.0)
        o = jnp.dot(act.astype(BF16), wd_ref[...], preferred_element_type=F32)
        dma_work(6)
        dma_work(7)
        acc_ref[...] = o if is_first else acc_ref[...] + o

        @pl.when(last)
        def _():
            @pl.when(tile >= 2)
            def _():
                wait_scatter(sl)
            _store_token_major(stage_ref.at[sl], acc_ref[...])

    @pl.when(i == 0)
    def _():
        for g in range(DMA_GROUPS):
            gather_group(tok_ref, 0, g)
        wait_gather(0)
        compute(0, True, False)

    for sl in (0, 1):
        @pl.when(valid & first & (i > 0) & (parity == sl))
        def _():
            wait_gather(sl)
            compute(sl, True, True)

        @pl.when(valid & jnp.logical_not(first) & (parity == sl))
        def _():
            compute(sl, False, False)

    @pl.when(i == n_pairs - 1)
    def _():
        for g in range(DMA_GROUPS):
            scatter_group(tok_ref, parity, g)
        wait_scatter(parity)
        wait_gather(1 - parity)
        if n_tiles > 1:
            wait_scatter(1 - parity)


def _experts(ptile, pexp, plo, phi, npairs, tok_of, h2, wg, wu, wd):
    d = SUBLANES * LANES
    r = tok_of.shape[0]
    f = wg.shape[-1]
    tm = EXPERT_TILE
    n_tiles = r // tm
    n_steps = ptile.shape[0]
    tok_block = lambda fn: pl.BlockSpec((tm,), lambda i, pt, pe, lo, hi, n: (fn(pt[i]),), memory_space=pltpu.SMEM)
    grid_spec = pltpu.PrefetchScalarGridSpec(
        num_scalar_prefetch=5,
        grid=(n_steps,),
        in_specs=[
            tok_block(lambda tl: tl),
            tok_block(lambda tl: jnp.minimum(tl + 1, n_tiles - 1)),
            tok_block(lambda tl: jnp.maximum(tl - 1, 0)),
            pl.BlockSpec(memory_space=pl.ANY),
            pl.BlockSpec((None, d, f), lambda i, pt, pe, lo, hi, n: (pe[i], 0, 0)),
            pl.BlockSpec((None, d, f), lambda i, pt, pe, lo, hi, n: (pe[i], 0, 0)),
            pl.BlockSpec((None, f, d), lambda i, pt, pe, lo, hi, n: (pe[i], 0, 0)),
        ],
        out_specs=pl.BlockSpec(memory_space=pl.ANY),
        scratch_shapes=[pltpu.VMEM((2, tm * SUBLANES, LANES), F32), pltpu.VMEM((tm, d), F32),
                        pltpu.VMEM((2, tm * SUBLANES, LANES), F32),
                        pltpu.SemaphoreType.DMA((2,)), pltpu.SemaphoreType.DMA((2,))],
    )
    return pl.pallas_call(
        functools.partial(_expert_kernel, n_tiles=n_tiles, n_tok=r // 2),
        grid_spec=grid_spec,
        out_shape=jax.ShapeDtypeStruct((r * SUBLANES, LANES), F32),
        compiler_params=pltpu.CompilerParams(
            dimension_semantics=("arbitrary",), vmem_limit_bytes=VMEM_LIMIT_BYTES),
        name="experts",
    )(ptile, pexp, plo, phi, npairs, tok_of, tok_of, tok_of, h2, wg, wu, wd)


def _pair_table(cnt, n_tiles):
    tm = EXPERT_TILE
    n_steps = n_tiles + N_EXPERTS - 1
    offs = jnp.cumsum(cnt) - cnt
    first = offs // tm
    last = jnp.where(cnt > 0, (offs + cnt - 1) // tm, first - 1)
    n_e = last - first + 1
    ends = jnp.cumsum(n_e)
    total = ends[-1]
    p = jnp.minimum(jnp.arange(n_steps, dtype=I32), total - 1)
    e = jnp.sum((p[:, None] >= ends[None, :]).astype(I32), axis=1)
    tile = first[e] + (p - (ends[e] - n_e[e]))
    lo = jnp.clip(offs[e] - tile * tm, 0, tm)
    hi = jnp.clip(offs[e] + cnt[e] - tile * tm, 0, tm)
    return tile.astype(I32), e.astype(I32), lo.astype(I32), hi.astype(I32), total.reshape(1).astype(I32)


def _combine_kernel(y1_ref, y2_ref, gcol_ref, x1_ref, gfin_ref, out_ref):
    tm = x1_ref.shape[0]
    gates = gcol_ref[...]
    y = (gates[:, 0:1] * _load_token_major(y1_ref, (), 0, tm)
         + gates[:, 1:2] * _load_token_major(y2_ref, (), 0, tm))
    out_ref[...] = _rms(x1_ref[...] + y, gfin_ref[...])


def _combine(y, gcol, x1, gfin):
    t, d = x1.shape
    tm = MOVE_TILE
    n_steps = t // tm
    return pl.pallas_call(
        _combine_kernel,
        grid=(n_steps,),
        in_specs=[pl.BlockSpec((tm * SUBLANES, LANES), lambda i: (i, 0)),
                  pl.BlockSpec((tm * SUBLANES, LANES), lambda i: (i + n_steps, 0)),
                  pl.BlockSpec((tm, LANES), lambda i: (i, 0)),
                  pl.BlockSpec((tm, d), lambda i: (i, 0)),
                  pl.BlockSpec((1, d), lambda i: (0, 0))],
        out_specs=pl.BlockSpec((tm, d), lambda i: (i, 0)),
        out_shape=jax.ShapeDtypeStruct((t, d), F32),
        compiler_params=pltpu.CompilerParams(dimension_semantics=("arbitrary",)),
        name="combine",
    )(y, y, gcol, x1, gfin)


def kernel(x, positions, norm_mix_gain, w_in, ret_gn_gain, sgu_norm_gain, sgu_w, sgu_b, w_out, norm_ffn_gain,
           w_router_group, b_router_group, w_router_expert, b_router_expert, w_expert_gate_up, w_expert_up,
           w_expert_down, final_norm_gain):
    bsz, seq, d = x.shape
    assert w_in.shape[0] == 1, "single-layer block"
    assert d == SUBLANES * LANES, "token-major layout needs one (8, 128) tile per token"
    assert seq % ATTN_TILE == 0 and (bsz * seq) % RANK_TILE == 0 and (2 * bsz * seq) % EXPERT_TILE == 0
    t = bsz * seq
    half = HEAD_DIM // 2

    pos4 = positions.reshape(bsz, seq // CHUNK, 1, CHUNK)
    invf = (ROPE_BASE ** (-jnp.arange(half, dtype=F32) * 2.0 / HEAD_DIM)).reshape(half, 1)
    wr = jnp.concatenate([
        jnp.transpose(w_router_expert[0], (0, 2, 1)).reshape(N_EXPERTS, d),
        w_router_group[0].T,
        jnp.zeros((ROUTER_ROWS - N_EXPERTS - N_GROUPS, d), F32)], axis=0)
    wrh = wr.astype(BF16)
    wrl = (wr - wrh.astype(F32)).astype(BF16)
    br = jnp.concatenate([b_router_expert[0].reshape(N_EXPERTS), b_router_group[0],
                          jnp.zeros((ROUTER_ROWS - N_EXPERTS - N_GROUPS,), F32)]).reshape(ROUTER_ROWS, 1)

    x1, h2, ids, gcol = _attn_router(
        x, pos4, norm_mix_gain[0].reshape(1, d), w_in[0].astype(BF16), ret_gn_gain[0].reshape(1, RET_WIDTH),
        sgu_norm_gain[0].reshape(1, SGU_WIDTH), sgu_w[0], sgu_b[0].reshape(SGU_GROUPS, CHUNK, 1),
        w_out[0].astype(BF16), invf, norm_ffn_gain[0].reshape(1, d), wrh, wrl, br)

    pos, cnt = _rank(ids)
    p1, p2 = pos[0], pos[1]
    tok_of = _invert(p1, p2)

    ff = w_expert_gate_up.shape[-1]
    ptile, pexp, plo, phi, npairs = _pair_table(cnt[:, 0].astype(I32), (2 * t) // EXPERT_TILE)
    y = _experts(ptile, pexp, plo, phi, npairs, tok_of, h2,
                       w_expert_gate_up[0].reshape(N_EXPERTS, d, ff).astype(BF16),
                       w_expert_up[0].reshape(N_EXPERTS, d, ff).astype(BF16),
                       w_expert_down[0].reshape(N_EXPERTS, ff, d).astype(BF16))

    out = _combine(y, gcol, x1, final_norm_gain.reshape(1, d))
    return out.reshape(bsz, seq, d)
```

```python
import functools
import math

import jax
import jax.numpy as jnp
from jax import lax
from jax.experimental import pallas as pl
from jax.experimental.pallas import tpu as pltpu

F32 = jnp.float32
BF16 = jnp.bfloat16
I32 = jnp.int32

RET_HEADS = 4
HEAD_DIM = 128
CHUNK = 128
SGU_GROUPS = 4
GROUP_DIM = 128
RET_WIDTH = RET_HEADS * HEAD_DIM
SGU_WIDTH = SGU_GROUPS * GROUP_DIM
N_SECTIONS = 6
N_GROUPS = 4
N_PER_GROUP = 8
N_EXPERTS = N_GROUPS * N_PER_GROUP
ROUTER_ROWS = 40
ROPE_BASE = 10000.0
EPS = 1e-6

LANES = 128
SUBLANES = 8
VMEM_LIMIT_BYTES = 56 * 1024 * 1024

ATTN_TILE = 512
INVERT_TILE = 1024
MOVE_TILE = 256
EXPERT_TILE = 256
DMA_GROUPS = 8

_NT = (((1,), (1,)), ((), ()))
_TN = (((0,), (0,)), ((), ()))

_LOG_GAMMA = [math.log(1.0 - 2.0 ** (-5.0 - h)) for h in range(RET_HEADS)]
_CHUNK_DECAY = [math.exp(lg * CHUNK) for lg in _LOG_GAMMA]
_K_SCALE = HEAD_DIM ** -0.5


def _rms(x, gain):
    return x * lax.rsqrt(jnp.mean(x * x, axis=-1, keepdims=True) + EPS) * gain


def _gelu(x):
    return 0.5 * x * (1.0 + lax.erf(x * 0.7071067811865476))


def _sigmoid(x):
    return 1.0 / (1.0 + jnp.exp(-x))


def _token_rows(row):
    return pl.ds(pl.multiple_of(row * SUBLANES, SUBLANES), SUBLANES)


def _load_token_major(ref, lead, tok0, n_tok):
    return jnp.concatenate(
        [ref[lead + (pl.ds(tok0 * SUBLANES + sl, n_tok, stride=SUBLANES), slice(None))]
         for sl in range(SUBLANES)], axis=1)


def _store_token_major(ref, value):
    n_tok = value.shape[0]
    for sl in range(SUBLANES):
        ref[pl.ds(sl, n_tok, stride=SUBLANES), :] = value[:, sl * LANES:(sl + 1) * LANES]


def _attn_router_kernel(x_ref, pos_ref, gmix_ref, win_ref, gn_ref, sgn_ref, sw_ref, sb_ref, wout_ref,
                        invf_ref, gffn_ref, wrh_ref, wrl_ref, br_ref,
                        x1_ref, h2_ref, ids_ref, rank_ref, cnt_ref, gcol_ref,
                        proj_ref, mix_ref, state_ref, dintra_ref, qdec_ref, kdec_ref, wc_ref, btab_ref, before_ref):
    ts, d = x_ref.shape
    b = pl.program_id(0)
    s = pl.program_id(1)

    @pl.when((b == 0) & (s == 0))
    def _init_tables():
        i = lax.broadcasted_iota(I32, (CHUNK, CHUNK), 0)
        j = lax.broadcasted_iota(I32, (CHUNK, CHUNK), 1)
        diff = (i - j).astype(F32)
        fi = i.astype(F32)
        for h in range(RET_HEADS):
            lg = _LOG_GAMMA[h]
            dintra_ref[h] = jnp.where(i >= j, jnp.exp(lg * diff), 0.0) * _K_SCALE
            qdec_ref[h] = jnp.exp(lg * (fi + 1.0))
            kdec_ref[h] = jnp.exp(lg * (CHUNK - 1.0 - fi)) * _K_SCALE
        for g in range(SGU_GROUPS):
            wc_ref[g] = jnp.where(i >= j, sw_ref[g], 0.0).astype(BF16)
            btab_ref[g] = jnp.broadcast_to(sb_ref[g], (CHUNK, GROUP_DIM))
        ta = lax.broadcasted_iota(I32, (ts, ts), 0)
        tb = lax.broadcasted_iota(I32, (ts, ts), 1)
        before_ref[...] = jnp.where(ta < tb, 1.0, 0.0).astype(BF16)
        cnt_ref[...] = jnp.zeros_like(cnt_ref)

    @pl.when(s == 0)
    def _reset_state():
        state_ref[...] = jnp.zeros_like(state_ref)

    x = x_ref[...]
    h = _rms(x, gmix_ref[...]).astype(BF16)
    for sec in range(N_SECTIONS):
        proj_ref[sec] = jnp.dot(h, win_ref[:, sec * RET_WIDTH:(sec + 1) * RET_WIDTH],
                                preferred_element_type=F32)

    def chunk_body(c, carry):
        rows = pl.ds(pl.multiple_of(c * CHUNK, CHUNK), CHUNK)
        ang_t = invf_ref[...] * pos_ref[c].astype(F32)
        cos_t = jnp.cos(ang_t)
        sin_t = jnp.sin(ang_t)
        cosf = jnp.concatenate([cos_t, cos_t], axis=0).T
        sinf = jnp.concatenate([-sin_t, sin_t], axis=0).T
        for hd in range(RET_HEADS):
            hs = slice(hd * HEAD_DIM, (hd + 1) * HEAD_DIM)
            q = proj_ref[0, rows, hs]
            k = proj_ref[1, rows, hs]
            vb = proj_ref[2, rows, hs].astype(BF16)
            gate = proj_ref[3, rows, hs]
            qr = q * cosf + pltpu.roll(q, HEAD_DIM // 2, 1) * sinf
            kr = k * cosf + pltpu.roll(k, HEAD_DIM // 2, 1) * sinf
            scores = lax.dot_general(qr.astype(BF16), kr.astype(BF16), _NT,
                                     preferred_element_type=F32) * dintra_ref[hd]
            st = state_ref[hd]
            o = (jnp.dot(scores.astype(BF16), vb, preferred_element_type=F32)
                 + jnp.dot((qr * qdec_ref[hd]).astype(BF16), st.astype(BF16), preferred_element_type=F32))
            kv = lax.dot_general((kr * kdec_ref[hd]).astype(BF16), vb, _TN, preferred_element_type=F32)
            state_ref[hd] = st * _CHUNK_DECAY[hd] + kv
            d = o - jnp.mean(o, axis=-1, keepdims=True)
            on = d * lax.rsqrt(jnp.mean(d * d, axis=-1, keepdims=True) + EPS) * gn_ref[:, hs]
            mix_ref[rows, hs] = (gate * _sigmoid(gate) * on).astype(BF16)
        for g in range(SGU_GROUPS):
            gs = slice(g * GROUP_DIM, (g + 1) * GROUP_DIM)
            u = _gelu(proj_ref[4, rows, gs])
            v = _gelu(proj_ref[5, rows, gs])
            vn = v * lax.rsqrt(jnp.mean(v * v, axis=-1, keepdims=True) + EPS) * sgn_ref[:, gs]
            sg = jnp.dot(wc_ref[g], vn.astype(BF16), preferred_element_type=F32) + btab_ref[g]
            mix_ref[rows, pl.ds(RET_WIDTH + g * GROUP_DIM, GROUP_DIM)] = (u * sg).astype(BF16)
        return carry

    lax.fori_loop(0, ts // CHUNK, chunk_body, 0)

    x1 = x_ref[...] + jnp.dot(mix_ref[...], wout_ref[...], preferred_element_type=F32)
    x1_ref[...] = x1
    h2 = _rms(x1, gffn_ref[...])
    _store_token_major(h2_ref, h2)

    h2h = h2.astype(BF16)
    h2l = (h2 - h2h.astype(F32)).astype(BF16)
    wrh = wrh_ref[...]
    logits = (lax.dot_general(wrh, h2h, _NT, preferred_element_type=F32)
              + lax.dot_general(wrh, h2l, _NT, preferred_element_type=F32)
              + lax.dot_general(wrl_ref[...], h2h, _NT, preferred_element_type=F32)
              + br_ref[...])
    el = logits[0:N_EXPERTS]
    gl = logits[N_EXPERTS:N_EXPERTS + N_GROUPS]
    rg = lax.broadcasted_iota(I32, (N_GROUPS, ts), 0).astype(F32)
    gmax = jnp.max(gl, axis=0, keepdims=True)
    gidx = jnp.min(jnp.where(gl == gmax, rg, float(N_GROUPS)), axis=0, keepdims=True)
    gweight = 1.0 / jnp.sum(jnp.exp(gl - gmax), axis=0, keepdims=True)
    re_i = lax.broadcasted_iota(I32, (N_EXPERTS, ts), 0)
    re = re_i.astype(F32)
    in_group = (re_i // N_PER_GROUP).astype(F32) == gidx
    neg = -jnp.inf
    sel = jnp.where(in_group, el, neg)
    m1 = jnp.max(sel, axis=0, keepdims=True)
    i1 = jnp.min(jnp.where(sel == m1, re, float(N_EXPERTS)), axis=0, keepdims=True)
    sel2 = jnp.where(re == i1, neg, sel)
    m2 = jnp.max(sel2, axis=0, keepdims=True)
    i2 = jnp.min(jnp.where(sel2 == m2, re, float(N_EXPERTS)), axis=0, keepdims=True)
    e2 = jnp.exp(m2 - m1)
    w1 = 1.0 / (1.0 + e2)
    ids_ref[...] = jnp.concatenate([i1, i2], axis=0).astype(I32)

    hit1 = re == i1
    hit2 = re == i2
    onehot = jnp.where(hit1, 1.0, jnp.where(hit2, 1.0, 0.0))
    seen = cnt_ref[:, 0:1] + jnp.dot(onehot.astype(BF16), before_ref[...], preferred_element_type=F32)
    rank_ref[...] = jnp.concatenate([jnp.sum(jnp.where(hit1, seen, 0.0), axis=0, keepdims=True),
                                     jnp.sum(jnp.where(hit2, seen, 0.0), axis=0, keepdims=True)],
                                    axis=0).astype(I32)
    cnt_ref[...] += jnp.broadcast_to(jnp.sum(onehot, axis=1, keepdims=True), cnt_ref.shape)
    gates_t = jnp.concatenate([gweight * w1, gweight * (e2 * w1), jnp.zeros((LANES - 2, ts), F32)], axis=0)
    gcol_ref[...] = gates_t.T


def _attn_router(x, pos4, gmix, win, gn, sgn, sw, sb, wout, invf, gffn, wrh, wrl, br):
    bsz, seq, d = x.shape
    ts = ATTN_TILE
    ns = seq // ts
    t = bsz * seq
    const2 = lambda b, s: (0, 0)
    const3 = lambda b, s: (0, 0, 0)
    single = dict(pipeline_mode=pl.Buffered(1))
    in_specs = [
        pl.BlockSpec((None, ts, d), lambda b, s: (b, s, 0)),
        pl.BlockSpec((None, ts // CHUNK, 1, CHUNK), lambda b, s: (b, s, 0, 0)),
        pl.BlockSpec((1, d), const2),
        pl.BlockSpec(win.shape, const2, **single),
        pl.BlockSpec((1, RET_WIDTH), const2),
        pl.BlockSpec((1, SGU_WIDTH), const2),
        pl.BlockSpec(sw.shape, const3),
        pl.BlockSpec(sb.shape, const3),
        pl.BlockSpec(wout.shape, const2, **single),
        pl.BlockSpec(invf.shape, const2),
        pl.BlockSpec((1, d), const2),
        pl.BlockSpec(wrh.shape, const2),
        pl.BlockSpec(wrl.shape, const2),
        pl.BlockSpec(br.shape, const2),
    ]
    tok = lambda b, s: (b * ns + s, 0)
    out_specs = [
        pl.BlockSpec((ts, d), tok),
        pl.BlockSpec((ts * SUBLANES, LANES), tok),
        pl.BlockSpec((2, ts), lambda b, s: (0, b * ns + s)),
        pl.BlockSpec((2, ts), lambda b, s: (0, b * ns + s)),
        pl.BlockSpec((N_EXPERTS, LANES), const2),
        pl.BlockSpec((ts, LANES), tok),
    ]
    out_shape = [
        jax.ShapeDtypeStruct((t, d), F32),
        jax.ShapeDtypeStruct((t * SUBLANES, LANES), F32),
        jax.ShapeDtypeStruct((2, t), I32),
        jax.ShapeDtypeStruct((2, t), I32),
        jax.ShapeDtypeStruct((N_EXPERTS, LANES), F32),
        jax.ShapeDtypeStruct((t, LANES), F32),
    ]
    scratch = [
        pltpu.VMEM((N_SECTIONS, ts, RET_WIDTH), F32),
        pltpu.VMEM((ts, RET_WIDTH + SGU_WIDTH), BF16),
        pltpu.VMEM((RET_HEADS, HEAD_DIM, HEAD_DIM), F32),
        pltpu.VMEM((RET_HEADS, CHUNK, CHUNK), F32),
        pltpu.VMEM((RET_HEADS, CHUNK, HEAD_DIM), F32),
        pltpu.VMEM((RET_HEADS, CHUNK, HEAD_DIM), F32),
        pltpu.VMEM((SGU_GROUPS, CHUNK, CHUNK), BF16),
        pltpu.VMEM((SGU_GROUPS, CHUNK, GROUP_DIM), F32),
        pltpu.VMEM((ts, ts), BF16),
    ]
    return pl.pallas_call(
        _attn_router_kernel,
        grid=(bsz, ns),
        in_specs=in_specs,
        out_specs=out_specs,
        out_shape=out_shape,
        scratch_shapes=scratch,
        compiler_params=pltpu.CompilerParams(
            dimension_semantics=("arbitrary", "arbitrary"), vmem_limit_bytes=VMEM_LIMIT_BYTES),
        name="attn_router",
    )(x, pos4, gmix, win, gn, sgn, sw, sb, wout, invf, gffn, wrh, wrl, br)


def _invert_kernel(p1_ref, p2_ref, tok_ref):
    n = p1_ref.shape[0]
    base = pl.program_id(0) * n

    def body(j, carry):
        tok_ref[p1_ref[j]] = base + j
        tok_ref[p2_ref[j]] = base + j
        return carry

    lax.fori_loop(0, n, body, 0, unroll=8)


def _invert(p1, p2):
    t = p1.shape[0]
    n = INVERT_TILE
    smem = lambda: pl.BlockSpec((n,), lambda i: (i,), memory_space=pltpu.SMEM)
    return pl.pallas_call(
        _invert_kernel,
        grid=(t // n,),
        in_specs=[smem(), smem()],
        out_specs=pl.BlockSpec(memory_space=pltpu.SMEM),
        out_shape=jax.ShapeDtypeStruct((2 * t,), I32),
        compiler_params=pltpu.CompilerParams(dimension_semantics=("arbitrary",)),
        name="invert",
    )(p1, p2)


def _expert_kernel(ptile_ref, pexp_ref, plo_ref, phi_ref, np_ref, tok_ref, toknext_ref, h2_ref,
                   wg_ref, wu_ref, wd_ref, o_ref, xbuf_ref, acc_ref, wgb_ref, wub_ref, wdb_ref, sems, *, n_tiles):
    i = pl.program_id(0)
    n_steps = pl.num_programs(0)
    tm = acc_ref.shape[0]
    tile = ptile_ref[i]
    n_pairs = np_ref[0]
    valid = i < n_pairs
    first = (i == 0) | (tile != ptile_ref[jnp.maximum(i - 1, 0)])
    last = (i == n_pairs - 1) | (ptile_ref[jnp.minimum(i + 1, n_steps - 1)] != tile)
    parity = tile & 1
    per_group = tm // DMA_GROUPS

    def row_copy(src_row, j, sl):
        return pltpu.make_async_copy(h2_ref.at[_token_rows(src_row)], xbuf_ref.at[sl, _token_rows(j)], sems.at[sl])

    def start_group(idx_ref, sl, g):
        for j in range(g * per_group, (g + 1) * per_group):
            row_copy(idx_ref[j], j, sl).start(priority=j % 2)

    def wait_tile(sl):
        pltpu.make_async_copy(h2_ref.at[pl.ds(0, tm * SUBLANES)], xbuf_ref.at[sl], sems.at[sl]).wait()

    def compute(sl, is_first):
        prefetch = (lambda g: start_group(toknext_ref, 1 - sl, g)) if is_first else (lambda g: None)
        xt = _load_token_major(xbuf_ref, (sl,), 0, tm).astype(BF16)
        prefetch(0)
        prefetch(1)
        a = jnp.dot(xt, wgb_ref[...], preferred_element_type=F32)
        prefetch(2)
        prefetch(3)
        bb = jnp.dot(xt, wub_ref[...], preferred_element_type=F32)
        prefetch(4)
        prefetch(5)
        row = lax.broadcasted_iota(I32, (tm, 1), 0)
        mine = (row >= plo_ref[i]) & (row < phi_ref[i])
        act = jnp.where(mine, a * _sigmoid(a) * bb, 0.0)
        o = jnp.dot(act.astype(BF16), wdb_ref[...], preferred_element_type=F32)
        prefetch(6)
        prefetch(7)
        acc_ref[...] = o if is_first else acc_ref[...] + o

        @pl.when(last)
        def _():
            _store_token_major(o_ref, acc_ref[...])

    @pl.when(i == 0)
    def _():
        for g in range(DMA_GROUPS):
            start_group(tok_ref, 0, g)

    @pl.when(valid & ((i == 0) | (pexp_ref[i] != pexp_ref[jnp.maximum(i - 1, 0)])))
    def _():
        wgb_ref[...] = wg_ref[...].astype(BF16)
        wub_ref[...] = wu_ref[...].astype(BF16)
        wdb_ref[...] = wd_ref[...].astype(BF16)

    for sl in (0, 1):
        @pl.when(valid & first & (parity == sl))
        def _():
            wait_tile(sl)
            compute(sl, True)

        @pl.when(valid & jnp.logical_not(first) & (parity == sl))
        def _():
            compute(sl, False)

    @pl.when(i == n_pairs - 1)
    def _():
        wait_tile(1 - parity)


def _experts(ptile, pexp, plo, phi, npairs, tok_of, h2, wg, wu, wd):
    d = SUBLANES * LANES
    r = tok_of.shape[0]
    f = wg.shape[-1]
    tm = EXPERT_TILE
    n_tiles = r // tm
    n_steps = ptile.shape[0]
    grid_spec = pltpu.PrefetchScalarGridSpec(
        num_scalar_prefetch=5,
        grid=(n_steps,),
        in_specs=[
            pl.BlockSpec((tm,), lambda i, pt, pe, lo, hi, n: (pt[i],), memory_space=pltpu.SMEM),
            pl.BlockSpec((tm,), lambda i, pt, pe, lo, hi, n: (jnp.minimum(pt[i] + 1, n_tiles - 1),),
                         memory_space=pltpu.SMEM),
            pl.BlockSpec(memory_space=pl.ANY),
            pl.BlockSpec((None, d, f), lambda i, pt, pe, lo, hi, n: (pe[i], 0, 0)),
            pl.BlockSpec((None, d, f), lambda i, pt, pe, lo, hi, n: (pe[i], 0, 0)),
            pl.BlockSpec((None, f, d), lambda i, pt, pe, lo, hi, n: (pe[i], 0, 0)),
        ],
        out_specs=pl.BlockSpec((tm * SUBLANES, LANES), lambda i, pt, pe, lo, hi, n: (pt[i], 0)),
        scratch_shapes=[pltpu.VMEM((2, tm * SUBLANES, LANES), F32), pltpu.VMEM((tm, d), F32),
                        pltpu.VMEM((d, f), BF16), pltpu.VMEM((d, f), BF16), pltpu.VMEM((f, d), BF16),
                        pltpu.SemaphoreType.DMA((2,))],
    )
    return pl.pallas_call(
        functools.partial(_expert_kernel, n_tiles=n_tiles),
        grid_spec=grid_spec,
        out_shape=jax.ShapeDtypeStruct((r * SUBLANES, LANES), F32),
        compiler_params=pltpu.CompilerParams(
            dimension_semantics=("arbitrary",), vmem_limit_bytes=VMEM_LIMIT_BYTES),
        name="experts",
    )(ptile, pexp, plo, phi, npairs, tok_of, tok_of, h2, wg, wu, wd)


def _pair_table(cnt, n_tiles):
    tm = EXPERT_TILE
    n_steps = n_tiles + N_EXPERTS - 1
    offs = jnp.cumsum(cnt) - cnt
    first = offs // tm
    last = jnp.where(cnt > 0, (offs + cnt - 1) // tm, first - 1)
    n_e = last - first + 1
    ends = jnp.cumsum(n_e)
    total = ends[-1]
    p = jnp.minimum(jnp.arange(n_steps, dtype=I32), total - 1)
    e = jnp.sum((p[:, None] >= ends[None, :]).astype(I32), axis=1)
    tile = first[e] + (p - (ends[e] - n_e[e]))
    lo = jnp.clip(offs[e] - tile * tm, 0, tm)
    hi = jnp.clip(offs[e] + cnt[e] - tile * tm, 0, tm)
    return tile.astype(I32), e.astype(I32), lo.astype(I32), hi.astype(I32), total.reshape(1).astype(I32)


def _combine_kernel(p1_ref, p2_ref, p1n_ref, p2n_ref, gcol_ref, x1_ref, gfin_ref, os_ref, out_ref, buf_ref, sems):
    i = pl.program_id(0)
    n_steps = pl.num_programs(0)
    tm = x1_ref.shape[0]
    per_group = tm // DMA_GROUPS

    def row_copy(src_row, j, sl, k):
        return pltpu.make_async_copy(os_ref.at[_token_rows(src_row)], buf_ref.at[sl, k, _token_rows(j)], sems.at[sl])

    def start_group(r1_ref, r2_ref, sl, g):
        for j in range(g * per_group, (g + 1) * per_group):
            row_copy(r1_ref[j], j, sl, 0).start(priority=0)
            row_copy(r2_ref[j], j, sl, 1).start(priority=1)

    def wait_tile(sl):
        for k in (0, 1):
            pltpu.make_async_copy(os_ref.at[pl.ds(0, tm * SUBLANES)], buf_ref.at[sl, k], sems.at[sl]).wait()

    @pl.when(i == 0)
    def _():
        for g in range(DMA_GROUPS):
            start_group(p1_ref, p2_ref, 0, g)

    def step(sl):
        wait_tile(sl)
        for g in range(DMA_GROUPS):
            rows = slice(g * per_group, (g + 1) * per_group)
            o1 = _load_token_major(buf_ref, (sl, 0), g * per_group, per_group)
            o2 = _load_token_major(buf_ref, (sl, 1), g * per_group, per_group)
            gates = gcol_ref[rows, :]
            y = gates[:, 0:1] * o1 + gates[:, 1:2] * o2
            out_ref[rows, :] = _rms(x1_ref[rows, :] + y, gfin_ref[...])
            start_group(p1n_ref, p2n_ref, 1 - sl, g)

    for sl in (0, 1):
        @pl.when((i & 1) == sl)
        def _():
            step(sl)

    @pl.when(i == n_steps - 1)
    def _():
        wait_tile(1 - (i & 1))


def _combine(p1, p2, gcol, x1, gfin, osorted):
    t, d = x1.shape
    tm = MOVE_TILE
    n_steps = t // tm
    cur = lambda: pl.BlockSpec((tm,), lambda i: (i,), memory_space=pltpu.SMEM)
    nxt = lambda: pl.BlockSpec((tm,), lambda i: (jnp.minimum(i + 1, n_steps - 1),), memory_space=pltpu.SMEM)
    return pl.pallas_call(
        _combine_kernel,
        grid=(n_steps,),
        in_specs=[cur(), cur(), nxt(), nxt(),
                  pl.BlockSpec((tm, LANES), lambda i: (i, 0)),
                  pl.BlockSpec((tm, d), lambda i: (i, 0)),
                  pl.BlockSpec((1, d), lambda i: (0, 0)),
                  pl.BlockSpec(memory_space=pl.ANY)],
        out_specs=pl.BlockSpec((tm, d), lambda i: (i, 0)),
        out_shape=jax.ShapeDtypeStruct((t, d), F32),
        scratch_shapes=[pltpu.VMEM((2, 2, tm * SUBLANES, LANES), F32), pltpu.SemaphoreType.DMA((2,))],
        compiler_params=pltpu.CompilerParams(dimension_semantics=("arbitrary",)),
        name="combine",
    )(p1, p2, p1, p2, gcol, x1, gfin, osorted)


def kernel(x, positions, norm_mix_gain, w_in, ret_gn_gain, sgu_norm_gain, sgu_w, sgu_b, w_out, norm_ffn_gain,
           w_router_group, b_router_group, w_router_expert, b_router_expert, w_expert_gate_up, w_expert_up,
           w_expert_down, final_norm_gain):
    bsz, seq, d = x.shape
    assert w_in.shape[0] == 1, "single-layer block"
    assert d == SUBLANES * LANES, "token-major layout needs one (8, 128) tile per token"
    assert seq % ATTN_TILE == 0 and (2 * bsz * seq) % EXPERT_TILE == 0
    t = bsz * seq
    half = HEAD_DIM // 2

    pos4 = positions.reshape(bsz, seq // CHUNK, 1, CHUNK)
    invf = (ROPE_BASE ** (-jnp.arange(half, dtype=F32) * 2.0 / HEAD_DIM)).reshape(half, 1)
    wr = jnp.concatenate([
        jnp.transpose(w_router_expert[0], (0, 2, 1)).reshape(N_EXPERTS, d),
        w_router_group[0].T,
        jnp.zeros((ROUTER_ROWS - N_EXPERTS - N_GROUPS, d), F32)], axis=0)
    wrh = wr.astype(BF16)
    wrl = (wr - wrh.astype(F32)).astype(BF16)
    br = jnp.concatenate([b_router_expert[0].reshape(N_EXPERTS), b_router_group[0],
                          jnp.zeros((ROUTER_ROWS - N_EXPERTS - N_GROUPS,), F32)]).reshape(ROUTER_ROWS, 1)

    x1, h2, ids, rank, cnt, gcol = _attn_router(
        x, pos4, norm_mix_gain[0].reshape(1, d), w_in[0].astype(BF16), ret_gn_gain[0].reshape(1, RET_WIDTH),
        sgu_norm_gain[0].reshape(1, SGU_WIDTH), sgu_w[0], sgu_b[0].reshape(SGU_GROUPS, CHUNK, 1),
        w_out[0].astype(BF16), invf, norm_ffn_gain[0].reshape(1, d), wrh, wrl, br)

    cnt = cnt[:, 0].astype(I32)
    offs = jnp.cumsum(cnt) - cnt
    expert_axis = jnp.arange(N_EXPERTS, dtype=I32)[:, None, None]
    pos = rank + jnp.sum(jnp.where(ids[None] == expert_axis, offs[:, None, None], 0), axis=0)
    p1, p2 = pos[0], pos[1]
    tok_of = _invert(p1, p2)

    ff = w_expert_gate_up.shape[-1]
    ptile, pexp, plo, phi, npairs = _pair_table(cnt, (2 * t) // EXPERT_TILE)
    osorted = _experts(ptile, pexp, plo, phi, npairs, tok_of, h2,
                       w_expert_gate_up[0].reshape(N_EXPERTS, d, ff),
                       w_expert_up[0].reshape(N_EXPERTS, d, ff),
                       w_expert_down[0].reshape(N_EXPERTS, ff, d))

    out = _combine(p1, p2, gcol, x1, final_norm_gain.reshape(1, d), osorted)
    return out.reshape(bsz, seq, d)
```

```python
import functools
import math

import jax
import jax.numpy as jnp
from jax import lax
from jax.experimental import pallas as pl
from jax.experimental.pallas import tpu as pltpu

F32 = jnp.float32
BF16 = jnp.bfloat16
I32 = jnp.int32

RET_HEADS = 4
HEAD_DIM = 128
CHUNK = 128
SGU_GROUPS = 4
GROUP_DIM = 128
RET_WIDTH = RET_HEADS * HEAD_DIM
SGU_WIDTH = SGU_GROUPS * GROUP_DIM
N_SECTIONS = 6
N_GROUPS = 4
N_PER_GROUP = 8
N_EXPERTS = N_GROUPS * N_PER_GROUP
ROUTER_ROWS = 40
ROPE_BASE = 10000.0
EPS = 1e-6

LANES = 128
SUBLANES = 8
VMEM_LIMIT_BYTES = 56 * 1024 * 1024

ATTN_TILE = 512
INVERT_TILE = 1024
MOVE_TILE = 256
EXPERT_TILE = 256
DMA_GROUPS = 8

_NT = (((1,), (1,)), ((), ()))
_TN = (((0,), (0,)), ((), ()))

_LOG_GAMMA = [math.log(1.0 - 2.0 ** (-5.0 - h)) for h in range(RET_HEADS)]
_CHUNK_DECAY = [math.exp(lg * CHUNK) for lg in _LOG_GAMMA]
_K_SCALE = HEAD_DIM ** -0.5


def _rms(x, gain):
    return x * lax.rsqrt(jnp.mean(x * x, axis=-1, keepdims=True) + EPS) * gain


def _gelu(x):
    return 0.5 * x * (1.0 + lax.erf(x * 0.7071067811865476))


def _sigmoid(x):
    return 1.0 / (1.0 + jnp.exp(-x))


def _token_rows(row):
    return pl.ds(pl.multiple_of(row * SUBLANES, SUBLANES), SUBLANES)


def _load_token_major(ref, lead, tok0, n_tok):
    return jnp.concatenate(
        [ref[lead + (pl.ds(tok0 * SUBLANES + sl, n_tok, stride=SUBLANES), slice(None))]
         for sl in range(SUBLANES)], axis=1)


def _store_token_major(ref, value):
    n_tok = value.shape[0]
    for sl in range(SUBLANES):
        ref[pl.ds(sl, n_tok, stride=SUBLANES), :] = value[:, sl * LANES:(sl + 1) * LANES]


def _attn_router_kernel(x_ref, pos_ref, gmix_ref, win_ref, gn_ref, sgn_ref, sw_ref, sb_ref, wout_ref,
                        invf_ref, gffn_ref, wrh_ref, wrl_ref, br_ref,
                        x1_ref, h2_ref, ids_ref, rank_ref, cnt_ref, gcol_ref,
                        proj_ref, mix_ref, attn_ref, rot_ref, state_ref, dintra_ref, qdec_ref, kdec_ref, wc_ref,
                        btab_ref, before_ref):
    ts, d = x_ref.shape
    b = pl.program_id(0)
    s = pl.program_id(1)

    @pl.when((b == 0) & (s == 0))
    def _init_tables():
        i = lax.broadcasted_iota(I32, (CHUNK, CHUNK), 0)
        j = lax.broadcasted_iota(I32, (CHUNK, CHUNK), 1)
        diff = (i - j).astype(F32)
        fi = i.astype(F32)
        for h in range(RET_HEADS):
            lg = _LOG_GAMMA[h]
            dintra_ref[h] = jnp.where(i >= j, jnp.exp(lg * diff), 0.0) * _K_SCALE
            qdec_ref[h] = jnp.exp(lg * (fi + 1.0))
            kdec_ref[h] = jnp.exp(lg * (CHUNK - 1.0 - fi)) * _K_SCALE
        for g in range(SGU_GROUPS):
            wc_ref[g] = jnp.where(i >= j, sw_ref[g], 0.0).astype(BF16)
            btab_ref[g] = jnp.broadcast_to(sb_ref[g], (CHUNK, GROUP_DIM))
        ta = lax.broadcasted_iota(I32, (ts, ts), 0)
        tb = lax.broadcasted_iota(I32, (ts, ts), 1)
        before_ref[...] = jnp.where(ta < tb, 1.0, 0.0).astype(BF16)
        cnt_ref[...] = jnp.zeros_like(cnt_ref)

    @pl.when(s == 0)
    def _reset_state():
        state_ref[...] = jnp.zeros_like(state_ref)

    n_chunks = ts // CHUNK
    pair = 2 * HEAD_DIM
    x = x_ref[...]
    h = _rms(x, gmix_ref[...]).astype(BF16)

    def project(sec, col0, width):
        cols = slice(col0, col0 + width)
        proj_ref[sec, :, cols] = jnp.dot(h, win_ref[:, sec * RET_WIDTH + col0:sec * RET_WIDTH + col0 + width],
                                         preferred_element_type=F32)

    def spatial_gating(c):
        rows = slice(c * CHUNK, (c + 1) * CHUNK)
        for g in range(SGU_GROUPS):
            gs = slice(g * GROUP_DIM, (g + 1) * GROUP_DIM)
            u = _gelu(proj_ref[4, rows, gs])
            v = _gelu(proj_ref[5, rows, gs])
            vn = v * lax.rsqrt(jnp.mean(v * v, axis=-1, keepdims=True) + EPS) * sgn_ref[:, gs]
            sg = jnp.dot(wc_ref[g], vn.astype(BF16), preferred_element_type=F32) + btab_ref[g]
            mix_ref[rows, RET_WIDTH + g * GROUP_DIM:RET_WIDTH + (g + 1) * GROUP_DIM] = (u * sg).astype(BF16)

    def rotary_tables(c):
        ang_t = invf_ref[...] * pos_ref[c].astype(F32)
        cos_t = jnp.cos(ang_t)
        sin_t = jnp.sin(ang_t)
        rot_ref[0, c] = jnp.concatenate([cos_t, cos_t], axis=0).T
        rot_ref[1, c] = jnp.concatenate([-sin_t, sin_t], axis=0).T

    def retention(c, hd):
        rows = slice(c * CHUNK, (c + 1) * CHUNK)
        hs = slice(hd * HEAD_DIM, (hd + 1) * HEAD_DIM)
        cosf = rot_ref[0, c]
        sinf = rot_ref[1, c]
        q = proj_ref[0, rows, hs]
        k = proj_ref[1, rows, hs]
        vb = proj_ref[2, rows, hs].astype(BF16)
        gate = proj_ref[3, rows, hs]
        qr = q * cosf + pltpu.roll(q, HEAD_DIM // 2, 1) * sinf
        kr = k * cosf + pltpu.roll(k, HEAD_DIM // 2, 1) * sinf
        scores = lax.dot_general(qr.astype(BF16), kr.astype(BF16), _NT,
                                 preferred_element_type=F32) * dintra_ref[hd]
        st = state_ref[hd]
        o = (jnp.dot(scores.astype(BF16), vb, preferred_element_type=F32)
             + jnp.dot((qr * qdec_ref[hd]).astype(BF16), st.astype(BF16), preferred_element_type=F32))
        kv = lax.dot_general((kr * kdec_ref[hd]).astype(BF16), vb, _TN, preferred_element_type=F32)
        state_ref[hd] = st * _CHUNK_DECAY[hd] + kv
        dv = o - jnp.mean(o, axis=-1, keepdims=True)
        on = dv * lax.rsqrt(jnp.mean(dv * dv, axis=-1, keepdims=True) + EPS) * gn_ref[:, hs]
        mix_ref[rows, hs] = (gate * _sigmoid(gate) * on).astype(BF16)

    def out_projection(col0, width):
        return jnp.dot(mix_ref[:, col0:col0 + width], wout_ref[col0:col0 + width, :], preferred_element_type=F32)

    project(4, 0, SGU_WIDTH)
    project(5, 0, SGU_WIDTH)
    for sec in range(4):
        project(sec, 0, pair)
    for c in range(n_chunks):
        spatial_gating(c)
    attn_ref[...] = out_projection(RET_WIDTH, SGU_WIDTH)
    for sec in range(4):
        project(sec, pair, pair)
    for c in range(n_chunks):
        rotary_tables(c)
    for c in range(n_chunks):
        retention(c, 0)
        retention(c, 1)
    attn_ref[...] += out_projection(0, pair)
    for c in range(n_chunks):
        retention(c, 2)
        retention(c, 3)
    x1 = x_ref[...] + (attn_ref[...] + out_projection(pair, pair))
    x1_ref[...] = x1
    h2 = _rms(x1, gffn_ref[...])
    _store_token_major(h2_ref, h2)

    h2h = h2.astype(BF16)
    h2l = (h2 - h2h.astype(F32)).astype(BF16)
    wrh = wrh_ref[...]
    logits = (lax.dot_general(wrh, h2h, _NT, preferred_element_type=F32)
              + lax.dot_general(wrh, h2l, _NT, preferred_element_type=F32)
              + lax.dot_general(wrl_ref[...], h2h, _NT, preferred_element_type=F32)
              + br_ref[...])
    el = logits[0:N_EXPERTS]
    gl = logits[N_EXPERTS:N_EXPERTS + N_GROUPS]
    rg = lax.broadcasted_iota(I32, (N_GROUPS, ts), 0).astype(F32)
    gmax = jnp.max(gl, axis=0, keepdims=True)
    gidx = jnp.min(jnp.where(gl == gmax, rg, float(N_GROUPS)), axis=0, keepdims=True)
    gweight = 1.0 / jnp.sum(jnp.exp(gl - gmax), axis=0, keepdims=True)
    re_i = lax.broadcasted_iota(I32, (N_EXPERTS, ts), 0)
    re = re_i.astype(F32)
    in_group = (re_i // N_PER_GROUP).astype(F32) == gidx
    neg = -jnp.inf
    sel = jnp.where(in_group, el, neg)
    m1 = jnp.max(sel, axis=0, keepdims=True)
    i1 = jnp.min(jnp.where(sel == m1, re, float(N_EXPERTS)), axis=0, keepdims=True)
    sel2 = jnp.where(re == i1, neg, sel)
    m2 = jnp.max(sel2, axis=0, keepdims=True)
    i2 = jnp.min(jnp.where(sel2 == m2, re, float(N_EXPERTS)), axis=0, keepdims=True)
    e2 = jnp.exp(m2 - m1)
    w1 = 1.0 / (1.0 + e2)
    ids_ref[...] = jnp.concatenate([i1, i2], axis=0).astype(I32)

    hit1 = re == i1
    hit2 = re == i2
    onehot = jnp.where(hit1, 1.0, jnp.where(hit2, 1.0, 0.0))
    seen = cnt_ref[:, 0:1] + jnp.dot(onehot.astype(BF16), before_ref[...], preferred_element_type=F32)
    rank_ref[...] = jnp.concatenate([jnp.sum(jnp.where(hit1, seen, 0.0), axis=0, keepdims=True),
                                     jnp.sum(jnp.where(hit2, seen, 0.0), axis=0, keepdims=True)],
                                    axis=0).astype(I32)
    cnt_ref[...] += jnp.broadcast_to(jnp.sum(onehot, axis=1, keepdims=True), cnt_ref.shape)
    gates_t = jnp.concatenate([gweight * w1, gweight * (e2 * w1), jnp.zeros((LANES - 2, ts), F32)], axis=0)
    gcol_ref[...] = gates_t.T


def _attn_router(x, pos4, gmix, win, gn, sgn, sw, sb, wout, invf, gffn, wrh, wrl, br):
    bsz, seq, d = x.shape
    ts = ATTN_TILE
    ns = seq // ts
    t = bsz * seq
    const2 = lambda b, s: (0, 0)
    const3 = lambda b, s: (0, 0, 0)
    single = dict(pipeline_mode=pl.Buffered(1))
    in_specs = [
        pl.BlockSpec((None, ts, d), lambda b, s: (b, s, 0)),
        pl.BlockSpec((None, ts // CHUNK, 1, CHUNK), lambda b, s: (b, s, 0, 0)),
        pl.BlockSpec((1, d), const2),
        pl.BlockSpec(win.shape, const2, **single),
        pl.BlockSpec((1, RET_WIDTH), const2),
        pl.BlockSpec((1, SGU_WIDTH), const2),
        pl.BlockSpec(sw.shape, const3),
        pl.BlockSpec(sb.shape, const3),
        pl.BlockSpec(wout.shape, const2, **single),
        pl.BlockSpec(invf.shape, const2),
        pl.BlockSpec((1, d), const2),
        pl.BlockSpec(wrh.shape, const2),
        pl.BlockSpec(wrl.shape, const2),
        pl.BlockSpec(br.shape, const2),
    ]
    tok = lambda b, s: (b * ns + s, 0)
    out_specs = [
        pl.BlockSpec((ts, d), tok),
        pl.BlockSpec((ts * SUBLANES, LANES), tok),
        pl.BlockSpec((2, ts), lambda b, s: (0, b * ns + s)),
        pl.BlockSpec((2, ts), lambda b, s: (0, b * ns + s)),
        pl.BlockSpec((N_EXPERTS, LANES), const2),
        pl.BlockSpec((ts, LANES), tok),
    ]
    out_shape = [
        jax.ShapeDtypeStruct((t, d), F32),
        jax.ShapeDtypeStruct((t * SUBLANES, LANES), F32),
        jax.ShapeDtypeStruct((2, t), I32),
        jax.ShapeDtypeStruct((2, t), I32),
        jax.ShapeDtypeStruct((N_EXPERTS, LANES), F32),
        jax.ShapeDtypeStruct((t, LANES), F32),
    ]
    scratch = [
        pltpu.VMEM((N_SECTIONS, ts, RET_WIDTH), F32),
        pltpu.VMEM((ts, RET_WIDTH + SGU_WIDTH), BF16),
        pltpu.VMEM((ts, d), F32),
        pltpu.VMEM((2, ts // CHUNK, CHUNK, HEAD_DIM), F32),
        pltpu.VMEM((RET_HEADS, HEAD_DIM, HEAD_DIM), F32),
        pltpu.VMEM((RET_HEADS, CHUNK, CHUNK), F32),
        pltpu.VMEM((RET_HEADS, CHUNK, HEAD_DIM), F32),
        pltpu.VMEM((RET_HEADS, CHUNK, HEAD_DIM), F32),
        pltpu.VMEM((SGU_GROUPS, CHUNK, CHUNK), BF16),
        pltpu.VMEM((SGU_GROUPS, CHUNK, GROUP_DIM), F32),
        pltpu.VMEM((ts, ts), BF16),
    ]
    return pl.pallas_call(
        _attn_router_kernel,
        grid=(bsz, ns),
        in_specs=in_specs,
        out_specs=out_specs,
        out_shape=out_shape,
        scratch_shapes=scratch,
        compiler_params=pltpu.CompilerParams(
            dimension_semantics=("arbitrary", "arbitrary"), vmem_limit_bytes=VMEM_LIMIT_BYTES),
        name="attn_router",
    )(x, pos4, gmix, win, gn, sgn, sw, sb, wout, invf, gffn, wrh, wrl, br)


def _invert_kernel(p1_ref, p2_ref, tok_ref):
    n = p1_ref.shape[0]
    base = pl.program_id(0) * n

    def body(j, carry):
        tok_ref[p1_ref[j]] = base + j
        tok_ref[p2_ref[j]] = base + j
        return carry

    lax.fori_loop(0, n, body, 0, unroll=8)


def _invert(p1, p2):
    t = p1.shape[0]
    n = INVERT_TILE
    smem = lambda: pl.BlockSpec((n,), lambda i: (i,), memory_space=pltpu.SMEM)
    return pl.pallas_call(
        _invert_kernel,
        grid=(t // n,),
        in_specs=[smem(), smem()],
        out_specs=pl.BlockSpec(memory_space=pltpu.SMEM),
        out_shape=jax.ShapeDtypeStruct((2 * t,), I32),
        compiler_params=pltpu.CompilerParams(dimension_semantics=("arbitrary",)),
        name="invert",
    )(p1, p2)


def _expert_kernel(ptile_ref, pexp_ref, plo_ref, phi_ref, np_ref, tok_ref, toknext_ref, h2_ref,
                   wg_ref, wu_ref, wd_ref, o_ref, xbuf_ref, acc_ref, wgb_ref, wub_ref, wdb_ref, sems, *, n_tiles):
    i = pl.program_id(0)
    n_steps = pl.num_programs(0)
    tm = acc_ref.shape[0]
    tile = ptile_ref[i]
    n_pairs = np_ref[0]
    valid = i < n_pairs
    first = (i == 0) | (tile != ptile_ref[jnp.maximum(i - 1, 0)])
    last = (i == n_pairs - 1) | (ptile_ref[jnp.minimum(i + 1, n_steps - 1)] != tile)
    parity = tile & 1
    per_group = tm // DMA_GROUPS

    def row_copy(src_row, j, sl):
        return pltpu.make_async_copy(h2_ref.at[_token_rows(src_row)], xbuf_ref.at[sl, _token_rows(j)], sems.at[sl])

    def start_group(idx_ref, sl, g):
        for j in range(g * per_group, (g + 1) * per_group):
            row_copy(idx_ref[j], j, sl).start(priority=j % 2)

    def wait_tile(sl):
        pltpu.make_async_copy(h2_ref.at[pl.ds(0, tm * SUBLANES)], xbuf_ref.at[sl], sems.at[sl]).wait()

    def compute(sl, is_first):
        prefetch = (lambda g: start_group(toknext_ref, 1 - sl, g)) if is_first else (lambda g: None)
        xt = _load_token_major(xbuf_ref, (sl,), 0, tm).astype(BF16)
        prefetch(0)
        prefetch(1)
        a = jnp.dot(xt, wgb_ref[...], preferred_element_type=F32)
        prefetch(2)
        prefetch(3)
        bb = jnp.dot(xt, wub_ref[...], preferred_element_type=F32)
        prefetch(4)
        prefetch(5)
        row = lax.broadcasted_iota(I32, (tm, 1), 0)
        mine = (row >= plo_ref[i]) & (row < phi_ref[i])
        act = jnp.where(mine, a * _sigmoid(a) * bb, 0.0)
        o = jnp.dot(act.astype(BF16), wdb_ref[...], preferred_element_type=F32)
        prefetch(6)
        prefetch(7)
        acc_ref[...] = o if is_first else acc_ref[...] + o

        @pl.when(last)
        def _():
            _store_token_major(o_ref, acc_ref[...])

    @pl.when(i == 0)
    def _():
        for g in range(DMA_GROUPS):
            start_group(tok_ref, 0, g)

    @pl.when(valid & ((i == 0) | (pexp_ref[i] != pexp_ref[jnp.maximum(i - 1, 0)])))
    def _():
        wgb_ref[...] = wg_ref[...].astype(BF16)
        wub_ref[...] = wu_ref[...].astype(BF16)
        wdb_ref[...] = wd_ref[...].astype(BF16)

    for sl in (0, 1):
        @pl.when(valid & first & (parity == sl))
        def _():
            wait_tile(sl)
            compute(sl, True)

        @pl.when(valid & jnp.logical_not(first) & (parity == sl))
        def _():
            compute(sl, False)

    @pl.when(i == n_pairs - 1)
    def _():
        wait_tile(1 - parity)


def _experts(ptile, pexp, plo, phi, npairs, tok_of, h2, wg, wu, wd):
    d = SUBLANES * LANES
    r = tok_of.shape[0]
    f = wg.shape[-1]
    tm = EXPERT_TILE
    n_tiles = r // tm
    n_steps = ptile.shape[0]
    grid_spec = pltpu.PrefetchScalarGridSpec(
        num_scalar_prefetch=5,
        grid=(n_steps,),
        in_specs=[
            pl.BlockSpec((tm,), lambda i, pt, pe, lo, hi, n: (pt[i],), memory_space=pltpu.SMEM),
            pl.BlockSpec((tm,), lambda i, pt, pe, lo, hi, n: (jnp.minimum(pt[i] + 1, n_tiles - 1),),
                         memory_space=pltpu.SMEM),
            pl.BlockSpec(memory_space=pl.ANY),
            pl.BlockSpec((None, d, f), lambda i, pt, pe, lo, hi, n: (pe[i], 0, 0)),
            pl.BlockSpec((None, d, f), lambda i, pt, pe, lo, hi, n: (pe[i], 0, 0)),
            pl.BlockSpec((None, f, d), lambda i, pt, pe, lo, hi, n: (pe[i], 0, 0)),
        ],
        out_specs=pl.BlockSpec((tm * SUBLANES, LANES), lambda i, pt, pe, lo, hi, n: (pt[i], 0)),
        scratch_shapes=[pltpu.VMEM((2, tm * SUBLANES, LANES), F32), pltpu.VMEM((tm, d), F32),
                        pltpu.VMEM((d, f), BF16), pltpu.VMEM((d, f), BF16), pltpu.VMEM((f, d), BF16),
                        pltpu.SemaphoreType.DMA((2,))],
    )
    return pl.pallas_call(
        functools.partial(_expert_kernel, n_tiles=n_tiles),
        grid_spec=grid_spec,
        out_shape=jax.ShapeDtypeStruct((r * SUBLANES, LANES), F32),
        compiler_params=pltpu.CompilerParams(
            dimension_semantics=("arbitrary",), vmem_limit_bytes=VMEM_LIMIT_BYTES),
        name="experts",
    )(ptile, pexp, plo, phi, npairs, tok_of, tok_of, h2, wg, wu, wd)


def _pair_table(cnt, n_tiles):
    tm = EXPERT_TILE
    n_steps = n_tiles + N_EXPERTS - 1
    offs = jnp.cumsum(cnt) - cnt
    first = offs // tm
    last = jnp.where(cnt > 0, (offs + cnt - 1) // tm, first - 1)
    n_e = last - first + 1
    ends = jnp.cumsum(n_e)
    total = ends[-1]
    p = jnp.minimum(jnp.arange(n_steps, dtype=I32), total - 1)
    e = jnp.sum((p[:, None] >= ends[None, :]).astype(I32), axis=1)
    tile = first[e] + (p - (ends[e] - n_e[e]))
    lo = jnp.clip(offs[e] - tile * tm, 0, tm)
    hi = jnp.clip(offs[e] + cnt[e] - tile * tm, 0, tm)
    return tile.astype(I32), e.astype(I32), lo.astype(I32), hi.astype(I32), total.reshape(1).astype(I32)


def _combine_kernel(p1_ref, p2_ref, p1n_ref, p2n_ref, gcol_ref, x1_ref, gfin_ref, os_ref, out_ref, buf_ref, sems):
    i = pl.program_id(0)
    n_steps = pl.num_programs(0)
    tm = x1_ref.shape[0]
    per_group = tm // DMA_GROUPS

    def row_copy(src_row, j, sl, k):
        return pltpu.make_async_copy(os_ref.at[_token_rows(src_row)], buf_ref.at[sl, k, _token_rows(j)], sems.at[sl])

    def start_group(r1_ref, r2_ref, sl, g):
        for j in range(g * per_group, (g + 1) * per_group):
            row_copy(r1_ref[j], j, sl, 0).start(priority=0)
            row_copy(r2_ref[j], j, sl, 1).start(priority=1)

    def wait_tile(sl):
        for k in (0, 1):
            pltpu.make_async_copy(os_ref.at[pl.ds(0, tm * SUBLANES)], buf_ref.at[sl, k], sems.at[sl]).wait()

    @pl.when(i == 0)
    def _():
        for g in range(DMA_GROUPS):
            start_group(p1_ref, p2_ref, 0, g)

    def step(sl):
        wait_tile(sl)
        for g in range(DMA_GROUPS):
            rows = slice(g * per_group, (g + 1) * per_group)
            o1 = _load_token_major(buf_ref, (sl, 0), g * per_group, per_group)
            o2 = _load_token_major(buf_ref, (sl, 1), g * per_group, per_group)
            gates = gcol_ref[rows, :]
            y = gates[:, 0:1] * o1 + gates[:, 1:2] * o2
            out_ref[rows, :] = _rms(x1_ref[rows, :] + y, gfin_ref[...])
            start_group(p1n_ref, p2n_ref, 1 - sl, g)

    for sl in (0, 1):
        @pl.when((i & 1) == sl)
        def _():
            step(sl)

    @pl.when(i == n_steps - 1)
    def _():
        wait_tile(1 - (i & 1))


def _combine(p1, p2, gcol, x1, gfin, osorted):
    t, d = x1.shape
    tm = MOVE_TILE
    n_steps = t // tm
    cur = lambda: pl.BlockSpec((tm,), lambda i: (i,), memory_space=pltpu.SMEM)
    nxt = lambda: pl.BlockSpec((tm,), lambda i: (jnp.minimum(i + 1, n_steps - 1),), memory_space=pltpu.SMEM)
    return pl.pallas_call(
        _combine_kernel,
        grid=(n_steps,),
        in_specs=[cur(), cur(), nxt(), nxt(),
                  pl.BlockSpec((tm, LANES), lambda i: (i, 0)),
                  pl.BlockSpec((tm, d), lambda i: (i, 0)),
                  pl.BlockSpec((1, d), lambda i: (0, 0)),
                  pl.BlockSpec(memory_space=pl.ANY)],
        out_specs=pl.BlockSpec((tm, d), lambda i: (i, 0)),
        out_shape=jax.ShapeDtypeStruct((t, d), F32),
        scratch_shapes=[pltpu.VMEM((2, 2, tm * SUBLANES, LANES), F32), pltpu.SemaphoreType.DMA((2,))],
        compiler_params=pltpu.CompilerParams(dimension_semantics=("arbitrary",)),
        name="combine",
    )(p1, p2, p1, p2, gcol, x1, gfin, osorted)


def kernel(x, positions, norm_mix_gain, w_in, ret_gn_gain, sgu_norm_gain, sgu_w, sgu_b, w_out, norm_ffn_gain,
           w_router_group, b_router_group, w_router_expert, b_router_expert, w_expert_gate_up, w_expert_up,
           w_expert_down, final_norm_gain):
    bsz, seq, d = x.shape
    assert w_in.shape[0] == 1, "single-layer block"
    assert d == SUBLANES * LANES, "token-major layout needs one (8, 128) tile per token"
    assert seq % ATTN_TILE == 0 and (2 * bsz * seq) % EXPERT_TILE == 0
    t = bsz * seq
    half = HEAD_DIM // 2

    pos4 = positions.reshape(bsz, seq // CHUNK, 1, CHUNK)
    invf = (ROPE_BASE ** (-jnp.arange(half, dtype=F32) * 2.0 / HEAD_DIM)).reshape(half, 1)
    wr = jnp.concatenate([
        jnp.transpose(w_router_expert[0], (0, 2, 1)).reshape(N_EXPERTS, d),
        w_router_group[0].T,
        jnp.zeros((ROUTER_ROWS - N_EXPERTS - N_GROUPS, d), F32)], axis=0)
    wrh = wr.astype(BF16)
    wrl = (wr - wrh.astype(F32)).astype(BF16)
    br = jnp.concatenate([b_router_expert[0].reshape(N_EXPERTS), b_router_group[0],
                          jnp.zeros((ROUTER_ROWS - N_EXPERTS - N_GROUPS,), F32)]).reshape(ROUTER_ROWS, 1)

    x1, h2, ids, rank, cnt, gcol = _attn_router(
        x, pos4, norm_mix_gain[0].reshape(1, d), w_in[0].astype(BF16), ret_gn_gain[0].reshape(1, RET_WIDTH),
        sgu_norm_gain[0].reshape(1, SGU_WIDTH), sgu_w[0], sgu_b[0].reshape(SGU_GROUPS, CHUNK, 1),
        w_out[0].astype(BF16), invf, norm_ffn_gain[0].reshape(1, d), wrh, wrl, br)

    cnt = cnt[:, 0].astype(I32)
    offs = jnp.cumsum(cnt) - cnt
    expert_axis = jnp.arange(N_EXPERTS, dtype=I32)[:, None, None]
    pos = rank + jnp.sum(jnp.where(ids[None] == expert_axis, offs[:, None, None], 0), axis=0)
    p1, p2 = pos[0], pos[1]
    tok_of = _invert(p1, p2)

    ff = w_expert_gate_up.shape[-1]
    ptile, pexp, plo, phi, npairs = _pair_table(cnt, (2 * t) // EXPERT_TILE)
    osorted = _experts(ptile, pexp, plo, phi, npairs, tok_of, h2,
                       w_expert_gate_up[0].reshape(N_EXPERTS, d, ff),
                       w_expert_up[0].reshape(N_EXPERTS, d, ff),
                       w_expert_down[0].reshape(N_EXPERTS, ff, d))

    out = _combine(p1, p2, gcol, x1, final_norm_gain.reshape(1, d), osorted)
    return out.reshape(bsz, seq, d)
```

```python
import functools
import math

import jax
import jax.numpy as jnp
from jax import lax
from jax.experimental import pallas as pl
from jax.experimental.pallas import tpu as pltpu

F32 = jnp.float32
BF16 = jnp.bfloat16
I32 = jnp.int32

RET_HEADS = 4
HEAD_DIM = 128
CHUNK = 128
SGU_GROUPS = 4
GROUP_DIM = 128
RET_WIDTH = RET_HEADS * HEAD_DIM
SGU_WIDTH = SGU_GROUPS * GROUP_DIM
N_SECTIONS = 6
N_GROUPS = 4
N_PER_GROUP = 8
N_EXPERTS = N_GROUPS * N_PER_GROUP
ROUTER_ROWS = 40
ROPE_BASE = 10000.0
EPS = 1e-6

LANES = 128
SUBLANES = 8
VMEM_LIMIT_BYTES = 56 * 1024 * 1024

ATTN_TILE = 512
MOVE_TILE = 256
EXPERT_TILE = 256
DMA_GROUPS = 8


def _rank_capacity(n_tok):
    return n_tok + EXPERT_TILE


def _rank_rows(n_tok):
    return _rank_capacity(n_tok) + (2 * ATTN_TILE) // N_EXPERTS


_NT = (((1,), (1,)), ((), ()))
_TN = (((0,), (0,)), ((), ()))

_LOG_GAMMA = [math.log(1.0 - 2.0 ** (-5.0 - h)) for h in range(RET_HEADS)]
_CHUNK_DECAY = [math.exp(lg * CHUNK) for lg in _LOG_GAMMA]
_K_SCALE = HEAD_DIM ** -0.5


def _rms(x, gain):
    return x * lax.rsqrt(jnp.mean(x * x, axis=-1, keepdims=True) + EPS) * gain


def _gelu(x):
    return 0.5 * x * (1.0 + lax.erf(x * 0.7071067811865476))


def _sigmoid(x):
    return 1.0 / (1.0 + jnp.exp(-x))


def _token_rows(row):
    return pl.ds(pl.multiple_of(row * SUBLANES, SUBLANES), SUBLANES)


def _load_token_major(ref, lead, tok0, n_tok):
    return jnp.concatenate(
        [ref[lead + (pl.ds(tok0 * SUBLANES + sl, n_tok, stride=SUBLANES), slice(None))]
         for sl in range(SUBLANES)], axis=1)


def _store_token_major(ref, value):
    n_tok = value.shape[0]
    for sl in range(SUBLANES):
        ref[pl.ds(sl, n_tok, stride=SUBLANES), :] = value[:, sl * LANES:(sl + 1) * LANES]


def _attn_router_kernel(x_ref, pos_ref, gmix_ref, win_ref, gn_ref, sgn_ref, sw_ref, sb_ref, wout_ref,
                        invf_ref, gffn_ref, wrh_ref, wrl_ref, br_ref,
                        x1_ref, key_ref, cnt_ref, gcol_ref, xs_ref,
                        proj_ref, mix_ref, attn_ref, rot_ref, state_ref, dintra_ref, qdec_ref, kdec_ref, wc_ref,
                        btab_ref, before_ref, stage_ref, keyv_ref, keys_ref, ssem, ksem, *, placeholder_row0):
    ts, d = x_ref.shape
    b = pl.program_id(0)
    s = pl.program_id(1)
    li = b * pl.num_programs(1) + s
    n_lin = pl.num_programs(0) * pl.num_programs(1)
    slot_cur = li & 1
    slot_prev = 1 - slot_cur
    tok_per_group = ts // DMA_GROUPS

    def scatter_group(g):
        for t in range(g * tok_per_group, (g + 1) * tok_per_group):
            for k in (0, 1):
                row = jnp.where(li > 0, keys_ref[slot_prev, k, t], placeholder_row0 + 2 * t + k)
                pltpu.make_async_copy(stage_ref.at[slot_prev, _token_rows(t)], xs_ref.at[_token_rows(row)],
                                      ssem.at[slot_prev]).start(priority=k)

    def wait_scatter(sl):
        for _ in (0, 1):
            pltpu.make_async_copy(stage_ref.at[sl], xs_ref.at[pl.ds(0, ts * SUBLANES)], ssem.at[sl]).wait()

    def keys_copy(sl):
        return pltpu.make_async_copy(keyv_ref.at[sl], keys_ref.at[sl], ksem.at[sl])

    @pl.when((b == 0) & (s == 0))
    def _init_tables():
        i = lax.broadcasted_iota(I32, (CHUNK, CHUNK), 0)
        j = lax.broadcasted_iota(I32, (CHUNK, CHUNK), 1)
        diff = (i - j).astype(F32)
        fi = i.astype(F32)
        for h in range(RET_HEADS):
            lg = _LOG_GAMMA[h]
            dintra_ref[h] = jnp.where(i >= j, jnp.exp(lg * diff), 0.0) * _K_SCALE
            qdec_ref[h] = jnp.exp(lg * (fi + 1.0))
            kdec_ref[h] = jnp.exp(lg * (CHUNK - 1.0 - fi)) * _K_SCALE
        for g in range(SGU_GROUPS):
            wc_ref[g] = jnp.where(i >= j, sw_ref[g], 0.0).astype(BF16)
            btab_ref[g] = jnp.broadcast_to(sb_ref[g], (CHUNK, GROUP_DIM))
        ta = lax.broadcasted_iota(I32, (ts, ts), 0)
        tb = lax.broadcasted_iota(I32, (ts, ts), 1)
        before_ref[...] = jnp.where(ta < tb, 1.0, 0.0).astype(BF16)
        cnt_ref[...] = jnp.zeros_like(cnt_ref)
        stage_ref[1] = jnp.zeros(stage_ref.shape[1:], F32)
        keyv_ref[1] = jnp.zeros(keyv_ref.shape[1:], I32)
        pltpu.sync_copy(keyv_ref.at[1], keys_ref.at[1])

    @pl.when(li > 0)
    def _():
        keys_copy(slot_prev).wait()

    @pl.when(s == 0)
    def _reset_state():
        state_ref[...] = jnp.zeros_like(state_ref)

    n_chunks = ts // CHUNK
    pair = 2 * HEAD_DIM
    x = x_ref[...]
    h = _rms(x, gmix_ref[...]).astype(BF16)

    def project(sec, col0, width):
        cols = slice(col0, col0 + width)
        proj_ref[sec, :, cols] = jnp.dot(h, win_ref[:, sec * RET_WIDTH + col0:sec * RET_WIDTH + col0 + width],
                                         preferred_element_type=F32)

    def spatial_gating(c):
        rows = slice(c * CHUNK, (c + 1) * CHUNK)
        for g in range(SGU_GROUPS):
            gs = slice(g * GROUP_DIM, (g + 1) * GROUP_DIM)
            u = _gelu(proj_ref[4, rows, gs])
            v = _gelu(proj_ref[5, rows, gs])
            vn = v * lax.rsqrt(jnp.mean(v * v, axis=-1, keepdims=True) + EPS) * sgn_ref[:, gs]
            sg = jnp.dot(wc_ref[g], vn.astype(BF16), preferred_element_type=F32) + btab_ref[g]
            mix_ref[rows, RET_WIDTH + g * GROUP_DIM:RET_WIDTH + (g + 1) * GROUP_DIM] = (u * sg).astype(BF16)

    def rotary_tables(c):
        ang_t = invf_ref[...] * pos_ref[c].astype(F32)
        cos_t = jnp.cos(ang_t)
        sin_t = jnp.sin(ang_t)
        rot_ref[0, c] = jnp.concatenate([cos_t, cos_t], axis=0).T
        rot_ref[1, c] = jnp.concatenate([-sin_t, sin_t], axis=0).T

    def retention(c, hd):
        rows = slice(c * CHUNK, (c + 1) * CHUNK)
        hs = slice(hd * HEAD_DIM, (hd + 1) * HEAD_DIM)
        cosf = rot_ref[0, c]
        sinf = rot_ref[1, c]
        q = proj_ref[0, rows, hs]
        k = proj_ref[1, rows, hs]
        vb = proj_ref[2, rows, hs].astype(BF16)
        gate = proj_ref[3, rows, hs]
        qr = q * cosf + pltpu.roll(q, HEAD_DIM // 2, 1) * sinf
        kr = k * cosf + pltpu.roll(k, HEAD_DIM // 2, 1) * sinf
        scores = lax.dot_general(qr.astype(BF16), kr.astype(BF16), _NT,
                                 preferred_element_type=F32) * dintra_ref[hd]
        st = state_ref[hd]
        o = (jnp.dot(scores.astype(BF16), vb, preferred_element_type=F32)
             + jnp.dot((qr * qdec_ref[hd]).astype(BF16), st.astype(BF16), preferred_element_type=F32))
        kv = lax.dot_general((kr * kdec_ref[hd]).astype(BF16), vb, _TN, preferred_element_type=F32)
        state_ref[hd] = st * _CHUNK_DECAY[hd] + kv
        dv = o - jnp.mean(o, axis=-1, keepdims=True)
        on = dv * lax.rsqrt(jnp.mean(dv * dv, axis=-1, keepdims=True) + EPS) * gn_ref[:, hs]
        mix_ref[rows, hs] = (gate * _sigmoid(gate) * on).astype(BF16)

    def out_projection(col0, width):
        return jnp.dot(mix_ref[:, col0:col0 + width], wout_ref[col0:col0 + width, :], preferred_element_type=F32)

    project(4, 0, SGU_WIDTH)
    project(5, 0, SGU_WIDTH)
    scatter_group(0)
    for sec in range(4):
        project(sec, 0, pair)
    scatter_group(1)
    for c in range(n_chunks):
        spatial_gating(c)
    scatter_group(2)
    attn_ref[...] = out_projection(RET_WIDTH, SGU_WIDTH)
    for sec in range(4):
        project(sec, pair, pair)
    scatter_group(3)
    for c in range(n_chunks):
        rotary_tables(c)
    scatter_group(4)
    for c in range(n_chunks):
        retention(c, 0)
        retention(c, 1)
    scatter_group(5)
    attn_ref[...] += out_projection(0, pair)
    scatter_group(6)
    for c in range(n_chunks):
        retention(c, 2)
        retention(c, 3)
    scatter_group(7)
    x1 = x_ref[...] + (attn_ref[...] + out_projection(pair, pair))
    x1_ref[...] = x1
    h2 = _rms(x1, gffn_ref[...])

    h2h = h2.astype(BF16)
    h2l = (h2 - h2h.astype(F32)).astype(BF16)
    wrh = wrh_ref[...]
    logits = (lax.dot_general(wrh, h2h, _NT, preferred_element_type=F32)
              + lax.dot_general(wrh, h2l, _NT, preferred_element_type=F32)
              + lax.dot_general(wrl_ref[...], h2h, _NT, preferred_element_type=F32)
              + br_ref[...])
    el = logits[0:N_EXPERTS]
    gl = logits[N_EXPERTS:N_EXPERTS + N_GROUPS]
    rg = lax.broadcasted_iota(I32, (N_GROUPS, ts), 0).astype(F32)
    gmax = jnp.max(gl, axis=0, keepdims=True)
    gidx = jnp.min(jnp.where(gl == gmax, rg, float(N_GROUPS)), axis=0, keepdims=True)
    gweight = 1.0 / jnp.sum(jnp.exp(gl - gmax), axis=0, keepdims=True)
    re_i = lax.broadcasted_iota(I32, (N_EXPERTS, ts), 0)
    re = re_i.astype(F32)
    in_group = (re_i // N_PER_GROUP).astype(F32) == gidx
    neg = -jnp.inf
    sel = jnp.where(in_group, el, neg)
    m1 = jnp.max(sel, axis=0, keepdims=True)
    i1 = jnp.min(jnp.where(sel == m1, re, float(N_EXPERTS)), axis=0, keepdims=True)
    sel2 = jnp.where(re == i1, neg, sel)
    m2 = jnp.max(sel2, axis=0, keepdims=True)
    i2 = jnp.min(jnp.where(sel2 == m2, re, float(N_EXPERTS)), axis=0, keepdims=True)
    e2 = jnp.exp(m2 - m1)
    w1 = 1.0 / (1.0 + e2)

    hit1 = re == i1
    hit2 = re == i2
    onehot = jnp.where(hit1, 1.0, jnp.where(hit2, 1.0, 0.0))
    seen = cnt_ref[:, 0:1] + jnp.dot(onehot.astype(BF16), before_ref[...], preferred_element_type=F32)
    keys = jnp.concatenate(
        [jnp.sum(jnp.where(hit1, seen, 0.0), axis=0, keepdims=True) * float(N_EXPERTS) + i1,
         jnp.sum(jnp.where(hit2, seen, 0.0), axis=0, keepdims=True) * float(N_EXPERTS) + i2], axis=0).astype(I32)
    key_ref[...] = keys
    cnt_ref[...] += jnp.broadcast_to(jnp.sum(onehot, axis=1, keepdims=True), cnt_ref.shape)
    gates_t = jnp.concatenate([gweight * w1, gweight * (e2 * w1), jnp.zeros((LANES - 2, ts), F32)], axis=0)
    gcol_ref[...] = gates_t.T

    @pl.when(li > 0)
    def _():
        wait_scatter(slot_cur)
    _store_token_major(stage_ref.at[slot_cur], h2)
    keyv_ref[slot_cur] = keys
    keys_copy(slot_cur).start()

    @pl.when(li == n_lin - 1)
    def _():
        keys_copy(slot_cur).wait()

        def last_rows(t, carry):
            for k in (0, 1):
                pltpu.make_async_copy(stage_ref.at[slot_cur, _token_rows(t)],
                                      xs_ref.at[_token_rows(keys_ref[slot_cur, k, t])],
                                      ssem.at[slot_cur]).start(priority=k)
            return carry
        lax.fori_loop(0, ts, last_rows, 0, unroll=4)
        wait_scatter(slot_prev)
        wait_scatter(slot_cur)


def _attn_router(x, pos4, gmix, win, gn, sgn, sw, sb, wout, invf, gffn, wrh, wrl, br):
    bsz, seq, d = x.shape
    ts = ATTN_TILE
    ns = seq // ts
    t = bsz * seq
    const2 = lambda b, s: (0, 0)
    const3 = lambda b, s: (0, 0, 0)
    single = dict(pipeline_mode=pl.Buffered(1))
    in_specs = [
        pl.BlockSpec((None, ts, d), lambda b, s: (b, s, 0)),
        pl.BlockSpec((None, ts // CHUNK, 1, CHUNK), lambda b, s: (b, s, 0, 0)),
        pl.BlockSpec((1, d), const2),
        pl.BlockSpec(win.shape, const2, **single),
        pl.BlockSpec((1, RET_WIDTH), const2),
        pl.BlockSpec((1, SGU_WIDTH), const2),
        pl.BlockSpec(sw.shape, const3),
        pl.BlockSpec(sb.shape, const3),
        pl.BlockSpec(wout.shape, const2, **single),
        pl.BlockSpec(invf.shape, const2),
        pl.BlockSpec((1, d), const2),
        pl.BlockSpec(wrh.shape, const2),
        pl.BlockSpec(wrl.shape, const2),
        pl.BlockSpec(br.shape, const2),
    ]
    tok = lambda b, s: (b * ns + s, 0)
    out_specs = [
        pl.BlockSpec((ts, d), tok),
        pl.BlockSpec((2, ts), lambda b, s: (0, b * ns + s)),
        pl.BlockSpec((N_EXPERTS, LANES), const2),
        pl.BlockSpec((ts, LANES), tok),
        pl.BlockSpec(memory_space=pl.ANY),
    ]
    out_shape = [
        jax.ShapeDtypeStruct((t, d), F32),
        jax.ShapeDtypeStruct((2, t), I32),
        jax.ShapeDtypeStruct((N_EXPERTS, LANES), F32),
        jax.ShapeDtypeStruct((t, LANES), F32),
        jax.ShapeDtypeStruct((_rank_rows(t) * N_EXPERTS * SUBLANES, LANES), F32),
    ]
    scratch = [
        pltpu.VMEM((N_SECTIONS, ts, RET_WIDTH), F32),
        pltpu.VMEM((ts, RET_WIDTH + SGU_WIDTH), BF16),
        pltpu.VMEM((ts, d), F32),
        pltpu.VMEM((2, ts // CHUNK, CHUNK, HEAD_DIM), F32),
        pltpu.VMEM((RET_HEADS, HEAD_DIM, HEAD_DIM), F32),
        pltpu.VMEM((RET_HEADS, CHUNK, CHUNK), F32),
        pltpu.VMEM((RET_HEADS, CHUNK, HEAD_DIM), F32),
        pltpu.VMEM((RET_HEADS, CHUNK, HEAD_DIM), F32),
        pltpu.VMEM((SGU_GROUPS, CHUNK, CHUNK), BF16),
        pltpu.VMEM((SGU_GROUPS, CHUNK, GROUP_DIM), F32),
        pltpu.VMEM((ts, ts), BF16),
        pltpu.VMEM((2, ts * SUBLANES, LANES), F32),
        pltpu.VMEM((2, 2, ts), I32),
        pltpu.SMEM((2, 2, ts), I32),
        pltpu.SemaphoreType.DMA((2,)),
        pltpu.SemaphoreType.DMA((2,)),
    ]
    return pl.pallas_call(
        functools.partial(_attn_router_kernel, placeholder_row0=_rank_capacity(t) * N_EXPERTS),
        grid=(bsz, ns),
        in_specs=in_specs,
        out_specs=out_specs,
        out_shape=out_shape,
        scratch_shapes=scratch,
        compiler_params=pltpu.CompilerParams(
            dimension_semantics=("arbitrary", "arbitrary"), vmem_limit_bytes=VMEM_LIMIT_BYTES),
        name="attn_router",
    )(x, pos4, gmix, win, gn, sgn, sw, sb, wout, invf, gffn, wrh, wrl, br)


def _tail_zero_kernel(cnt_ref, xs_in_ref, xs_ref, zero_ref, sem):
    del xs_in_ref
    zero_ref[...] = jnp.zeros_like(zero_ref)

    def tail(e, cnt):
        return pltpu.make_async_copy(zero_ref, xs_ref.at[pl.ds(cnt, EXPERT_TILE), e], sem)

    for e in range(N_EXPERTS):
        tail(e, cnt_ref[e]).start()
    for e in range(N_EXPERTS):
        tail(e, 0).wait()


def _tail_zero(cnt, xs4):
    grid_spec = pltpu.PrefetchScalarGridSpec(
        num_scalar_prefetch=1,
        grid=(1,),
        in_specs=[pl.BlockSpec(memory_space=pl.ANY)],
        out_specs=pl.BlockSpec(memory_space=pl.ANY),
        scratch_shapes=[pltpu.VMEM((EXPERT_TILE, SUBLANES, LANES), F32), pltpu.SemaphoreType.DMA],
    )
    return pl.pallas_call(
        _tail_zero_kernel,
        grid_spec=grid_spec,
        out_shape=jax.ShapeDtypeStruct(xs4.shape, xs4.dtype),
        input_output_aliases={1: 0},
        name="tail_zero",
    )(cnt, xs4)


def _expert_kernel(texp_ref, trank_ref, nt_ref, xs_ref, wg_ref, wu_ref, wd_ref, ys_ref, wgb_ref, wub_ref, wdb_ref):
    i = pl.program_id(0)
    tm = xs_ref.shape[0]
    valid = i < nt_ref[0]

    @pl.when(valid & ((i == 0) | (texp_ref[i] != texp_ref[jnp.maximum(i - 1, 0)])))
    def _():
        wgb_ref[...] = wg_ref[...].astype(BF16)
        wub_ref[...] = wu_ref[...].astype(BF16)
        wdb_ref[...] = wd_ref[...].astype(BF16)

    @pl.when(valid)
    def _():
        xt = _load_token_major(xs_ref.reshape(tm * SUBLANES, LANES), (), 0, tm).astype(BF16)
        a = jnp.dot(xt, wgb_ref[...], preferred_element_type=F32)
        bb = jnp.dot(xt, wub_ref[...], preferred_element_type=F32)
        act = a * _sigmoid(a) * bb
        o = jnp.dot(act.astype(BF16), wdb_ref[...], preferred_element_type=F32)
        _store_token_major(ys_ref.reshape(tm * SUBLANES, LANES), o)


def _experts(texp, trank, ntiles, xs4, wg, wu, wd):
    d = SUBLANES * LANES
    f = wg.shape[-1]
    tm = EXPERT_TILE
    n_steps = texp.shape[0]
    rows = lambda i, te, tr, n: (tr[i], te[i], 0, 0)
    grid_spec = pltpu.PrefetchScalarGridSpec(
        num_scalar_prefetch=3,
        grid=(n_steps,),
        in_specs=[
            pl.BlockSpec((tm, None, SUBLANES, LANES), rows),
            pl.BlockSpec((None, d, f), lambda i, te, tr, n: (te[i], 0, 0)),
            pl.BlockSpec((None, d, f), lambda i, te, tr, n: (te[i], 0, 0)),
            pl.BlockSpec((None, f, d), lambda i, te, tr, n: (te[i], 0, 0)),
        ],
        out_specs=pl.BlockSpec((tm, None, SUBLANES, LANES), rows),
        scratch_shapes=[pltpu.VMEM((d, f), BF16), pltpu.VMEM((d, f), BF16), pltpu.VMEM((f, d), BF16)],
    )
    return pl.pallas_call(
        _expert_kernel,
        grid_spec=grid_spec,
        out_shape=jax.ShapeDtypeStruct(xs4.shape, F32),
        compiler_params=pltpu.CompilerParams(
            dimension_semantics=("arbitrary",), vmem_limit_bytes=VMEM_LIMIT_BYTES),
        name="experts",
    )(texp, trank, ntiles, xs4, wg, wu, wd)


def _tile_table(cnt, n_steps):
    n_e = (cnt + EXPERT_TILE - 1) // EXPERT_TILE
    ends = jnp.cumsum(n_e)
    total = ends[-1]
    p = jnp.minimum(jnp.arange(n_steps, dtype=I32), total - 1)
    e = jnp.sum((p[:, None] >= ends[None, :]).astype(I32), axis=1)
    k = p - (ends[e] - n_e[e])
    return e.astype(I32), k.astype(I32), total.reshape(1).astype(I32)


def _combine_kernel(p1_ref, p2_ref, p1n_ref, p2n_ref, gcol_ref, x1_ref, gfin_ref, os_ref, out_ref, buf_ref, sems):
    i = pl.program_id(0)
    n_steps = pl.num_programs(0)
    tm = x1_ref.shape[0]
    per_group = tm // DMA_GROUPS

    def row_copy(src_row, j, sl, k):
        return pltpu.make_async_copy(os_ref.at[_token_rows(src_row)], buf_ref.at[sl, k, _token_rows(j)], sems.at[sl])

    def start_group(r1_ref, r2_ref, sl, g):
        for j in range(g * per_group, (g + 1) * per_group):
            row_copy(r1_ref[j], j, sl, 0).start(priority=0)
            row_copy(r2_ref[j], j, sl, 1).start(priority=1)

    def wait_tile(sl):
        for k in (0, 1):
            pltpu.make_async_copy(os_ref.at[pl.ds(0, tm * SUBLANES)], buf_ref.at[sl, k], sems.at[sl]).wait()

    @pl.when(i == 0)
    def _():
        for g in range(DMA_GROUPS):
            start_group(p1_ref, p2_ref, 0, g)

    def step(sl):
        wait_tile(sl)
        for g in range(DMA_GROUPS):
            rows = slice(g * per_group, (g + 1) * per_group)
            o1 = _load_token_major(buf_ref, (sl, 0), g * per_group, per_group)
            o2 = _load_token_major(buf_ref, (sl, 1), g * per_group, per_group)
            gates = gcol_ref[rows, :]
            y = gates[:, 0:1] * o1 + gates[:, 1:2] * o2
            out_ref[rows, :] = _rms(x1_ref[rows, :] + y, gfin_ref[...])
            start_group(p1n_ref, p2n_ref, 1 - sl, g)

    for sl in (0, 1):
        @pl.when((i & 1) == sl)
        def _():
            step(sl)

    @pl.when(i == n_steps - 1)
    def _():
        wait_tile(1 - (i & 1))


def _combine(p1, p2, gcol, x1, gfin, osorted):
    t, d = x1.shape
    tm = MOVE_TILE
    n_steps = t // tm
    cur = lambda: pl.BlockSpec((tm,), lambda i: (i,), memory_space=pltpu.SMEM)
    nxt = lambda: pl.BlockSpec((tm,), lambda i: (jnp.minimum(i + 1, n_steps - 1),), memory_space=pltpu.SMEM)
    return pl.pallas_call(
        _combine_kernel,
        grid=(n_steps,),
        in_specs=[cur(), cur(), nxt(), nxt(),
                  pl.BlockSpec((tm, LANES), lambda i: (i, 0)),
                  pl.BlockSpec((tm, d), lambda i: (i, 0)),
                  pl.BlockSpec((1, d), lambda i: (0, 0)),
                  pl.BlockSpec(memory_space=pl.ANY)],
        out_specs=pl.BlockSpec((tm, d), lambda i: (i, 0)),
        out_shape=jax.ShapeDtypeStruct((t, d), F32),
        scratch_shapes=[pltpu.VMEM((2, 2, tm * SUBLANES, LANES), F32), pltpu.SemaphoreType.DMA((2,))],
        compiler_params=pltpu.CompilerParams(dimension_semantics=("arbitrary",)),
        name="combine",
    )(p1, p2, p1, p2, gcol, x1, gfin, osorted)


def kernel(x, positions, norm_mix_gain, w_in, ret_gn_gain, sgu_norm_gain, sgu_w, sgu_b, w_out, norm_ffn_gain,
           w_router_group, b_router_group, w_router_expert, b_router_expert, w_expert_gate_up, w_expert_up,
           w_expert_down, final_norm_gain):
    bsz, seq, d = x.shape
    assert w_in.shape[0] == 1, "single-layer block"
    assert d == SUBLANES * LANES, "token-major layout needs one (8, 128) tile per token"
    assert seq % ATTN_TILE == 0 and (2 * bsz * seq) % EXPERT_TILE == 0
    t = bsz * seq
    half = HEAD_DIM // 2

    pos4 = positions.reshape(bsz, seq // CHUNK, 1, CHUNK)
    invf = (ROPE_BASE ** (-jnp.arange(half, dtype=F32) * 2.0 / HEAD_DIM)).reshape(half, 1)
    wr = jnp.concatenate([
        jnp.transpose(w_router_expert[0], (0, 2, 1)).reshape(N_EXPERTS, d),
        w_router_group[0].T,
        jnp.zeros((ROUTER_ROWS - N_EXPERTS - N_GROUPS, d), F32)], axis=0)
    wrh = wr.astype(BF16)
    wrl = (wr - wrh.astype(F32)).astype(BF16)
    br = jnp.concatenate([b_router_expert[0].reshape(N_EXPERTS), b_router_group[0],
                          jnp.zeros((ROUTER_ROWS - N_EXPERTS - N_GROUPS,), F32)]).reshape(ROUTER_ROWS, 1)

    x1, keys, cnt, gcol, xs = _attn_router(
        x, pos4, norm_mix_gain[0].reshape(1, d), w_in[0].astype(BF16), ret_gn_gain[0].reshape(1, RET_WIDTH),
        sgu_norm_gain[0].reshape(1, SGU_WIDTH), sgu_w[0], sgu_b[0].reshape(SGU_GROUPS, CHUNK, 1),
        w_out[0].astype(BF16), invf, norm_ffn_gain[0].reshape(1, d), wrh, wrl, br)

    cnt = cnt[:, 0].astype(I32)
    xs4 = _tail_zero(cnt, xs.reshape(_rank_rows(t), N_EXPERTS, SUBLANES, LANES))
    ff = w_expert_gate_up.shape[-1]
    texp, trank, ntiles = _tile_table(cnt, (2 * t) // EXPERT_TILE + N_EXPERTS)
    ys4 = _experts(texp, trank, ntiles, xs4,
                   w_expert_gate_up[0].reshape(N_EXPERTS, d, ff),
                   w_expert_up[0].reshape(N_EXPERTS, d, ff),
                   w_expert_down[0].reshape(N_EXPERTS, ff, d))

    out = _combine(keys[0], keys[1], gcol, x1, final_norm_gain.reshape(1, d), ys4.reshape(xs.shape))
    return out.reshape(bsz, seq, d)
```

```python
import functools
import math

import jax
import jax.numpy as jnp
from jax import lax
from jax.experimental import pallas as pl
from jax.experimental.pallas import tpu as pltpu

F32 = jnp.float32
BF16 = jnp.bfloat16
I32 = jnp.int32

RET_HEADS = 4
HEAD_DIM = 128
CHUNK = 128
SGU_GROUPS = 4
GROUP_DIM = 128
RET_WIDTH = RET_HEADS * HEAD_DIM
SGU_WIDTH = SGU_GROUPS * GROUP_DIM
N_SECTIONS = 6
N_GROUPS = 4
N_PER_GROUP = 8
N_EXPERTS = N_GROUPS * N_PER_GROUP
ROUTER_ROWS = 40
ROPE_BASE = 10000.0
EPS = 1e-6

LANES = 128
SUBLANES = 8
VMEM_LIMIT_BYTES = 56 * 1024 * 1024

ATTN_TILE = 512
MOVE_TILE = 256
EXPERT_TILE = 256
DMA_GROUPS = 8
_TAIL_PIECES = tuple(EXPERT_TILE >> (b + 1) for b in range(EXPERT_TILE.bit_length() - 1))


def _max_blocks(n_tok):
    return (2 * n_tok) // EXPERT_TILE + N_EXPERTS


def _total_blocks(n_tok):
    return _max_blocks(n_tok) + (2 * ATTN_TILE) // EXPERT_TILE


_NT = (((1,), (1,)), ((), ()))
_TN = (((0,), (0,)), ((), ()))

_LOG_GAMMA = [math.log(1.0 - 2.0 ** (-5.0 - h)) for h in range(RET_HEADS)]
_CHUNK_DECAY = [math.exp(lg * CHUNK) for lg in _LOG_GAMMA]
_K_SCALE = HEAD_DIM ** -0.5


def _rms(x, gain):
    return x * lax.rsqrt(jnp.mean(x * x, axis=-1, keepdims=True) + EPS) * gain


def _gelu(x):
    return 0.5 * x * (1.0 + lax.erf(x * 0.7071067811865476))


def _sigmoid(x):
    return 1.0 / (1.0 + jnp.exp(-x))


def _token_rows(row):
    return pl.ds(pl.multiple_of(row * SUBLANES, SUBLANES), SUBLANES)


def _load_token_major(ref, lead, tok0, n_tok):
    return jnp.concatenate(
        [ref[lead + (pl.ds(tok0 * SUBLANES + sl, n_tok, stride=SUBLANES), slice(None))]
         for sl in range(SUBLANES)], axis=1)


def _store_token_major(ref, value):
    n_tok = value.shape[0]
    for sl in range(SUBLANES):
        ref[pl.ds(sl, n_tok, stride=SUBLANES), :] = value[:, sl * LANES:(sl + 1) * LANES]


def _attn_router_kernel(x_ref, pos_ref, gmix_ref, win_ref, gn_ref, sgn_ref, sw_ref, sb_ref, wout_ref,
                        invf_ref, gffn_ref, wrh_ref, wrl_ref, br_ref,
                        x1_ref, key_ref, cnt_ref, blocks_ref, gcol_ref, xs_ref,
                        proj_ref, mix_ref, attn_ref, rot_ref, state_ref, dintra_ref, qdec_ref, kdec_ref, wc_ref,
                        btab_ref, before_ref, open_ref, nalloc_ref, stage_ref, keyv_ref, keys_ref, zero_ref,
                        metav_ref, meta_ref, ssem, ksem, fsem, *, placeholder_row0, max_blocks):
    ts, d = x_ref.shape
    b = pl.program_id(0)
    s = pl.program_id(1)
    li = b * pl.num_programs(1) + s
    n_lin = pl.num_programs(0) * pl.num_programs(1)
    slot_cur = li & 1
    slot_prev = 1 - slot_cur
    tok_per_group = ts // DMA_GROUPS

    def scatter_group(g):
        for t in range(g * tok_per_group, (g + 1) * tok_per_group):
            for k in (0, 1):
                pltpu.make_async_copy(stage_ref.at[slot_prev, _token_rows(t)],
                                      xs_ref.at[_token_rows(keys_ref[slot_prev, k, t])],
                                      ssem.at[slot_prev]).start(priority=k)

    def wait_scatter(sl):
        for _ in (0, 1):
            pltpu.make_async_copy(stage_ref.at[sl], xs_ref.at[pl.ds(0, ts * SUBLANES)], ssem.at[sl]).wait()

    def keys_copy(sl):
        return pltpu.make_async_copy(keyv_ref.at[sl], keys_ref.at[sl], ksem.at[sl])

    @pl.when((b == 0) & (s == 0))
    def _init_tables():
        i = lax.broadcasted_iota(I32, (CHUNK, CHUNK), 0)
        j = lax.broadcasted_iota(I32, (CHUNK, CHUNK), 1)
        diff = (i - j).astype(F32)
        fi = i.astype(F32)
        for h in range(RET_HEADS):
            lg = _LOG_GAMMA[h]
            dintra_ref[h] = jnp.where(i >= j, jnp.exp(lg * diff), 0.0) * _K_SCALE
            qdec_ref[h] = jnp.exp(lg * (fi + 1.0))
            kdec_ref[h] = jnp.exp(lg * (CHUNK - 1.0 - fi)) * _K_SCALE
        for g in range(SGU_GROUPS):
            wc_ref[g] = jnp.where(i >= j, sw_ref[g], 0.0).astype(BF16)
            btab_ref[g] = jnp.broadcast_to(sb_ref[g], (CHUNK, GROUP_DIM))
        ta = lax.broadcasted_iota(I32, (ts, ts), 0)
        tb = lax.broadcasted_iota(I32, (ts, ts), 1)
        before_ref[...] = jnp.where(ta < tb, 1.0, 0.0).astype(BF16)
        cnt_ref[...] = jnp.zeros_like(cnt_ref)
        blocks_ref[...] = jnp.zeros_like(blocks_ref)
        open_ref[...] = jnp.zeros_like(open_ref)
        nalloc_ref[...] = jnp.zeros_like(nalloc_ref)
        stage_ref[1] = jnp.zeros(stage_ref.shape[1:], F32)
        kk = lax.broadcasted_iota(I32, (2, ts), 0)
        tt = lax.broadcasted_iota(I32, (2, ts), 1)
        keyv_ref[1] = placeholder_row0 + 2 * tt + kk
        pltpu.sync_copy(keyv_ref.at[1], keys_ref.at[1])

    @pl.when(li > 0)
    def _():
        keys_copy(slot_prev).wait()

    @pl.when(s == 0)
    def _reset_state():
        state_ref[...] = jnp.zeros_like(state_ref)

    n_chunks = ts // CHUNK
    pair = 2 * HEAD_DIM
    x = x_ref[...]
    h = _rms(x, gmix_ref[...]).astype(BF16)

    def project(sec, col0, width):
        cols = slice(col0, col0 + width)
        proj_ref[sec, :, cols] = jnp.dot(h, win_ref[:, sec * RET_WIDTH + col0:sec * RET_WIDTH + col0 + width],
                                         preferred_element_type=F32)

    def spatial_gating(c):
        rows = slice(c * CHUNK, (c + 1) * CHUNK)
        for g in range(SGU_GROUPS):
            gs = slice(g * GROUP_DIM, (g + 1) * GROUP_DIM)
            u = _gelu(proj_ref[4, rows, gs])
            v = _gelu(proj_ref[5, rows, gs])
            vn = v * lax.rsqrt(jnp.mean(v * v, axis=-1, keepdims=True) + EPS) * sgn_ref[:, gs]
            sg = jnp.dot(wc_ref[g], vn.astype(BF16), preferred_element_type=F32) + btab_ref[g]
            mix_ref[rows, RET_WIDTH + g * GROUP_DIM:RET_WIDTH + (g + 1) * GROUP_DIM] = (u * sg).astype(BF16)

    def rotary_tables(c):
        ang_t = invf_ref[...] * pos_ref[c].astype(F32)
        cos_t = jnp.cos(ang_t)
        sin_t = jnp.sin(ang_t)
        rot_ref[0, c] = jnp.concatenate([cos_t, cos_t], axis=0).T
        rot_ref[1, c] = jnp.concatenate([-sin_t, sin_t], axis=0).T

    def retention(c, hd):
        rows = slice(c * CHUNK, (c + 1) * CHUNK)
        hs = slice(hd * HEAD_DIM, (hd + 1) * HEAD_DIM)
        cosf = rot_ref[0, c]
        sinf = rot_ref[1, c]
        q = proj_ref[0, rows, hs]
        k = proj_ref[1, rows, hs]
        vb = proj_ref[2, rows, hs].astype(BF16)
        gate = proj_ref[3, rows, hs]
        qr = q * cosf + pltpu.roll(q, HEAD_DIM // 2, 1) * sinf
        kr = k * cosf + pltpu.roll(k, HEAD_DIM // 2, 1) * sinf
        scores = lax.dot_general(qr.astype(BF16), kr.astype(BF16), _NT,
                                 preferred_element_type=F32) * dintra_ref[hd]
        st = state_ref[hd]
        o = (jnp.dot(scores.astype(BF16), vb, preferred_element_type=F32)
             + jnp.dot((qr * qdec_ref[hd]).astype(BF16), st.astype(BF16), preferred_element_type=F32))
        kv = lax.dot_general((kr * kdec_ref[hd]).astype(BF16), vb, _TN, preferred_element_type=F32)
        state_ref[hd] = st * _CHUNK_DECAY[hd] + kv
        dv = o - jnp.mean(o, axis=-1, keepdims=True)
        on = dv * lax.rsqrt(jnp.mean(dv * dv, axis=-1, keepdims=True) + EPS) * gn_ref[:, hs]
        mix_ref[rows, hs] = (gate * _sigmoid(gate) * on).astype(BF16)

    def out_projection(col0, width):
        return jnp.dot(mix_ref[:, col0:col0 + width], wout_ref[col0:col0 + width, :], preferred_element_type=F32)

    project(4, 0, SGU_WIDTH)
    project(5, 0, SGU_WIDTH)
    scatter_group(0)
    for sec in range(4):
        project(sec, 0, pair)
    scatter_group(1)
    for c in range(n_chunks):
        spatial_gating(c)
    scatter_group(2)
    attn_ref[...] = out_projection(RET_WIDTH, SGU_WIDTH)
    for sec in range(4):
        project(sec, pair, pair)
    scatter_group(3)
    for c in range(n_chunks):
        rotary_tables(c)
    scatter_group(4)
    for c in range(n_chunks):
        retention(c, 0)
        retention(c, 1)
    scatter_group(5)
    attn_ref[...] += out_projection(0, pair)
    scatter_group(6)
    for c in range(n_chunks):
        retention(c, 2)
        retention(c, 3)
    scatter_group(7)
    x1 = x_ref[...] + (attn_ref[...] + out_projection(pair, pair))
    x1_ref[...] = x1
    h2 = _rms(x1, gffn_ref[...])

    h2h = h2.astype(BF16)
    h2l = (h2 - h2h.astype(F32)).astype(BF16)
    wrh = wrh_ref[...]
    logits = (lax.dot_general(wrh, h2h, _NT, preferred_element_type=F32)
              + lax.dot_general(wrh, h2l, _NT, preferred_element_type=F32)
              + lax.dot_general(wrl_ref[...], h2h, _NT, preferred_element_type=F32)
              + br_ref[...])
    el = logits[0:N_EXPERTS]
    gl = logits[N_EXPERTS:N_EXPERTS + N_GROUPS]
    rg = lax.broadcasted_iota(I32, (N_GROUPS, ts), 0).astype(F32)
    gmax = jnp.max(gl, axis=0, keepdims=True)
    gidx = jnp.min(jnp.where(gl == gmax, rg, float(N_GROUPS)), axis=0, keepdims=True)
    gweight = 1.0 / jnp.sum(jnp.exp(gl - gmax), axis=0, keepdims=True)
    re_i = lax.broadcasted_iota(I32, (N_EXPERTS, ts), 0)
    re = re_i.astype(F32)
    in_group = (re_i // N_PER_GROUP).astype(F32) == gidx
    neg = -jnp.inf
    sel = jnp.where(in_group, el, neg)
    m1 = jnp.max(sel, axis=0, keepdims=True)
    i1 = jnp.min(jnp.where(sel == m1, re, float(N_EXPERTS)), axis=0, keepdims=True)
    sel2 = jnp.where(re == i1, neg, sel)
    m2 = jnp.max(sel2, axis=0, keepdims=True)
    i2 = jnp.min(jnp.where(sel2 == m2, re, float(N_EXPERTS)), axis=0, keepdims=True)
    e2 = jnp.exp(m2 - m1)
    w1 = 1.0 / (1.0 + e2)

    hit1 = re == i1
    hit2 = re == i2
    onehot = jnp.where(hit1, 1.0, jnp.where(hit2, 1.0, 0.0))
    seen = cnt_ref[:, 0:1] + jnp.dot(onehot.astype(BF16), before_ref[...], preferred_element_type=F32)
    inv_blk = 1.0 / EXPERT_TILE
    c0 = cnt_ref[:, 0:1]
    tile_cnt = jnp.sum(onehot, axis=1, keepdims=True)
    c1 = c0 + tile_cnt
    up0 = jnp.floor((c0 + (EXPERT_TILE - 1.0)) * inv_blk)
    n_new = jnp.floor((c1 + (EXPERT_TILE - 1.0)) * inv_blk) - up0
    ea = lax.broadcasted_iota(I32, (N_EXPERTS, N_EXPERTS), 0)
    eb = lax.broadcasted_iota(I32, (N_EXPERTS, N_EXPERTS), 1)
    earlier = jnp.where(eb < ea, 1.0, 0.0).astype(BF16)
    n_new_b = jnp.broadcast_to(n_new, (N_EXPERTS, LANES))
    new1 = nalloc_ref[:, 0:1] + jnp.dot(earlier, n_new_b.astype(BF16), preferred_element_type=F32)[:, 0:1]
    new2 = new1 + 1.0
    open_blk = open_ref[:, 0:1]
    rank_blk = jnp.floor(seen * inv_blk)
    which = rank_blk - up0
    block_of = jnp.where(which < 0.0, open_blk, jnp.where(which == 0.0, new1, new2))
    row_of = block_of * float(EXPERT_TILE) + (seen - rank_blk * float(EXPERT_TILE))
    keys = jnp.concatenate([jnp.sum(jnp.where(hit1, row_of, 0.0), axis=0, keepdims=True),
                            jnp.sum(jnp.where(hit2, row_of, 0.0), axis=0, keepdims=True)], axis=0).astype(I32)
    key_ref[...] = keys
    lane = lax.broadcasted_iota(I32, (N_EXPERTS, LANES), 1).astype(F32)
    blocks = blocks_ref[...]
    blocks = jnp.where(lane == up0, jnp.where(n_new >= 1.0, new1, blocks), blocks)
    blocks = jnp.where(lane == up0 + 1.0, jnp.where(n_new == 2.0, new2, blocks), blocks)
    blocks_ref[...] = blocks
    open_ref[...] = jnp.broadcast_to(jnp.where(n_new == 0.0, open_blk, jnp.where(n_new == 1.0, new1, new2)),
                                     open_ref.shape)
    nalloc_ref[...] += jnp.sum(n_new_b, axis=0, keepdims=True)
    cnt_ref[...] += jnp.broadcast_to(tile_cnt, cnt_ref.shape)
    gates_t = jnp.concatenate([gweight * w1, gweight * (e2 * w1), jnp.zeros((LANES - 2, ts), F32)], axis=0)
    gcol_ref[...] = gates_t.T

    @pl.when(li > 0)
    def _():
        wait_scatter(slot_cur)
    _store_token_major(stage_ref.at[slot_cur], h2)
    keyv_ref[slot_cur] = keys
    keys_copy(slot_cur).start()

    @pl.when(li == n_lin - 1)
    def _():
        keys_copy(slot_cur).wait()

        def last_rows(t, carry):
            for k in (0, 1):
                pltpu.make_async_copy(stage_ref.at[slot_cur, _token_rows(t)],
                                      xs_ref.at[_token_rows(keys_ref[slot_cur, k, t])],
                                      ssem.at[slot_cur]).start(priority=k)
            return carry
        lax.fori_loop(0, ts, last_rows, 0, unroll=4)

        total = cnt_ref[...]
        used = total - jnp.floor(total * inv_blk) * float(EXPERT_TILE)
        meta_lane = lax.broadcasted_iota(I32, (N_EXPERTS, LANES), 1)
        metav_ref[...] = jnp.where(
            meta_lane == 0, open_ref[...] * float(EXPERT_TILE) + used,
            jnp.where(meta_lane == 1, jnp.where(used > 0.0, float(EXPERT_TILE) - used, 0.0),
                      nalloc_ref[...])).astype(I32)
        pltpu.sync_copy(metav_ref, meta_ref)
        zero_ref[...] = jnp.zeros_like(zero_ref)

        def piece(row, n_rows):
            return pltpu.make_async_copy(
                zero_ref.at[pl.ds(0, n_rows * SUBLANES)],
                xs_ref.at[pl.ds(pl.multiple_of(row * SUBLANES, SUBLANES), n_rows * SUBLANES)], fsem)

        def for_each_piece(action):
            for e in range(N_EXPERTS):
                row = meta_ref[e, 0]
                length = meta_ref[e, 1]
                for n_rows in _TAIL_PIECES:
                    @pl.when((length & n_rows) != 0)
                    def _():
                        action(piece(row, n_rows))
                    row = row + (length & n_rows)
            for blk in range(max_blocks - N_EXPERTS, max_blocks):
                @pl.when(blk >= meta_ref[0, 2])
                def _():
                    action(piece(blk * EXPERT_TILE, EXPERT_TILE))

        for_each_piece(lambda cp: cp.start())
        for_each_piece(lambda cp: cp.wait())
        wait_scatter(slot_prev)
        wait_scatter(slot_cur)


def _attn_router(x, pos4, gmix, win, gn, sgn, sw, sb, wout, invf, gffn, wrh, wrl, br):
    bsz, seq, d = x.shape
    ts = ATTN_TILE
    ns = seq // ts
    t = bsz * seq
    const2 = lambda b, s: (0, 0)
    const3 = lambda b, s: (0, 0, 0)
    single = dict(pipeline_mode=pl.Buffered(1))
    in_specs = [
        pl.BlockSpec((None, ts, d), lambda b, s: (b, s, 0)),
        pl.BlockSpec((None, ts // CHUNK, 1, CHUNK), lambda b, s: (b, s, 0, 0)),
        pl.BlockSpec((1, d), const2),
        pl.BlockSpec(win.shape, const2, **single),
        pl.BlockSpec((1, RET_WIDTH), const2),
        pl.BlockSpec((1, SGU_WIDTH), const2),
        pl.BlockSpec(sw.shape, const3),
        pl.BlockSpec(sb.shape, const3),
        pl.BlockSpec(wout.shape, const2, **single),
        pl.BlockSpec(invf.shape, const2),
        pl.BlockSpec((1, d), const2),
        pl.BlockSpec(wrh.shape, const2),
        pl.BlockSpec(wrl.shape, const2),
        pl.BlockSpec(br.shape, const2),
    ]
    tok = lambda b, s: (b * ns + s, 0)
    out_specs = [
        pl.BlockSpec((ts, d), tok),
        pl.BlockSpec((2, ts), lambda b, s: (0, b * ns + s)),
        pl.BlockSpec((N_EXPERTS, LANES), const2),
        pl.BlockSpec((N_EXPERTS, LANES), const2),
        pl.BlockSpec((ts, LANES), tok),
        pl.BlockSpec(memory_space=pl.ANY),
    ]
    out_shape = [
        jax.ShapeDtypeStruct((t, d), F32),
        jax.ShapeDtypeStruct((2, t), I32),
        jax.ShapeDtypeStruct((N_EXPERTS, LANES), F32),
        jax.ShapeDtypeStruct((N_EXPERTS, LANES), F32),
        jax.ShapeDtypeStruct((t, LANES), F32),
        jax.ShapeDtypeStruct((_total_blocks(t) * EXPERT_TILE * SUBLANES, LANES), F32),
    ]
    scratch = [
        pltpu.VMEM((N_SECTIONS, ts, RET_WIDTH), F32),
        pltpu.VMEM((ts, RET_WIDTH + SGU_WIDTH), BF16),
        pltpu.VMEM((ts, d), F32),
        pltpu.VMEM((2, ts // CHUNK, CHUNK, HEAD_DIM), F32),
        pltpu.VMEM((RET_HEADS, HEAD_DIM, HEAD_DIM), F32),
        pltpu.VMEM((RET_HEADS, CHUNK, CHUNK), F32),
        pltpu.VMEM((RET_HEADS, CHUNK, HEAD_DIM), F32),
        pltpu.VMEM((RET_HEADS, CHUNK, HEAD_DIM), F32),
        pltpu.VMEM((SGU_GROUPS, CHUNK, CHUNK), BF16),
        pltpu.VMEM((SGU_GROUPS, CHUNK, GROUP_DIM), F32),
        pltpu.VMEM((ts, ts), BF16),
        pltpu.VMEM((N_EXPERTS, LANES), F32),
        pltpu.VMEM((N_EXPERTS, LANES), F32),
        pltpu.VMEM((2, ts * SUBLANES, LANES), F32),
        pltpu.VMEM((2, 2, ts), I32),
        pltpu.SMEM((2, 2, ts), I32),
        pltpu.VMEM((EXPERT_TILE * SUBLANES, LANES), F32),
        pltpu.VMEM((N_EXPERTS, LANES), I32),
        pltpu.SMEM((N_EXPERTS, LANES), I32),
        pltpu.SemaphoreType.DMA((2,)),
        pltpu.SemaphoreType.DMA((2,)),
        pltpu.SemaphoreType.DMA,
    ]
    return pl.pallas_call(
        functools.partial(_attn_router_kernel, placeholder_row0=_max_blocks(t) * EXPERT_TILE,
                          max_blocks=_max_blocks(t)),
        grid=(bsz, ns),
        in_specs=in_specs,
        out_specs=out_specs,
        out_shape=out_shape,
        scratch_shapes=scratch,
        compiler_params=pltpu.CompilerParams(
            dimension_semantics=("arbitrary", "arbitrary"), vmem_limit_bytes=VMEM_LIMIT_BYTES),
        name="attn_router",
    )(x, pos4, gmix, win, gn, sgn, sw, sb, wout, invf, gffn, wrh, wrl, br)


def _expert_kernel(texp_ref, tin_ref, tout_ref, nt_ref, xs_ref, wg_ref, wu_ref, wd_ref, ys_ref,
                   wgb_ref, wub_ref, wdb_ref):
    i = pl.program_id(0)
    tm = xs_ref.shape[0] // SUBLANES
    valid = i < nt_ref[0]

    @pl.when(valid & ((i == 0) | (texp_ref[i] != texp_ref[jnp.maximum(i - 1, 0)])))
    def _():
        wgb_ref[...] = wg_ref[...].astype(BF16)
        wub_ref[...] = wu_ref[...].astype(BF16)
        wdb_ref[...] = wd_ref[...].astype(BF16)

    @pl.when(valid)
    def _():
        xt = _load_token_major(xs_ref, (), 0, tm).astype(BF16)
        a = jnp.dot(xt, wgb_ref[...], preferred_element_type=F32)
        bb = jnp.dot(xt, wub_ref[...], preferred_element_type=F32)
        act = a * _sigmoid(a) * bb
        o = jnp.dot(act.astype(BF16), wdb_ref[...], preferred_element_type=F32)
        _store_token_major(ys_ref, o)

    @pl.when(jnp.logical_not(valid))
    def _():
        ys_ref[...] = jnp.zeros_like(ys_ref)


def _experts(texp, tin, tout, ntiles, xs, wg, wu, wd):
    d = SUBLANES * LANES
    f = wg.shape[-1]
    rows = EXPERT_TILE * SUBLANES
    n_steps = texp.shape[0]
    grid_spec = pltpu.PrefetchScalarGridSpec(
        num_scalar_prefetch=4,
        grid=(n_steps,),
        in_specs=[
            pl.BlockSpec((rows, LANES), lambda i, te, ti, to, n: (ti[i], 0)),
            pl.BlockSpec((None, d, f), lambda i, te, ti, to, n: (te[i], 0, 0)),
            pl.BlockSpec((None, d, f), lambda i, te, ti, to, n: (te[i], 0, 0)),
            pl.BlockSpec((None, f, d), lambda i, te, ti, to, n: (te[i], 0, 0)),
        ],
        out_specs=pl.BlockSpec((rows, LANES), lambda i, te, ti, to, n: (to[i], 0)),
        scratch_shapes=[pltpu.VMEM((d, f), BF16), pltpu.VMEM((d, f), BF16), pltpu.VMEM((f, d), BF16)],
    )
    return pl.pallas_call(
        _expert_kernel,
        grid_spec=grid_spec,
        out_shape=jax.ShapeDtypeStruct(xs.shape, F32),
        compiler_params=pltpu.CompilerParams(
            dimension_semantics=("arbitrary",), vmem_limit_bytes=VMEM_LIMIT_BYTES),
        name="experts",
    )(texp, tin, tout, ntiles, xs, wg, wu, wd)


def _block_plan(cnt, blocks, n_steps):
    n_e = (cnt + EXPERT_TILE - 1) // EXPERT_TILE
    ends = jnp.cumsum(n_e)
    total = ends[-1]
    step = jnp.arange(n_steps, dtype=I32)
    p = jnp.minimum(step, total - 1)
    e = jnp.sum((p[:, None] >= ends[None, :]).astype(I32), axis=1)
    k = p - (ends[e] - n_e[e])
    blk_in = blocks[e, k]
    blk_out = jnp.where(step < total, blk_in, step)
    return e.astype(I32), blk_in.astype(I32), blk_out.astype(I32), total.reshape(1).astype(I32)


def _combine_kernel(p1_ref, p2_ref, p1n_ref, p2n_ref, gcol_ref, x1_ref, gfin_ref, os_ref, out_ref, buf_ref, sems):
    i = pl.program_id(0)
    n_steps = pl.num_programs(0)
    tm = x1_ref.shape[0]
    per_group = tm // DMA_GROUPS

    def row_copy(src_row, j, sl, k):
        return pltpu.make_async_copy(os_ref.at[_token_rows(src_row)], buf_ref.at[sl, k, _token_rows(j)], sems.at[sl])

    def start_group(r1_ref, r2_ref, sl, g):
        for j in range(g * per_group, (g + 1) * per_group):
            row_copy(r1_ref[j], j, sl, 0).start(priority=0)
            row_copy(r2_ref[j], j, sl, 1).start(priority=1)

    def wait_tile(sl):
        for k in (0, 1):
            pltpu.make_async_copy(os_ref.at[pl.ds(0, tm * SUBLANES)], buf_ref.at[sl, k], sems.at[sl]).wait()

    @pl.when(i == 0)
    def _():
        for g in range(DMA_GROUPS):
            start_group(p1_ref, p2_ref, 0, g)

    def step(sl):
        wait_tile(sl)
        for g in range(DMA_GROUPS):
            rows = slice(g * per_group, (g + 1) * per_group)
            o1 = _load_token_major(buf_ref, (sl, 0), g * per_group, per_group)
            o2 = _load_token_major(buf_ref, (sl, 1), g * per_group, per_group)
            gates = gcol_ref[rows, :]
            y = gates[:, 0:1] * o1 + gates[:, 1:2] * o2
            out_ref[rows, :] = _rms(x1_ref[rows, :] + y, gfin_ref[...])
            start_group(p1n_ref, p2n_ref, 1 - sl, g)

    for sl in (0, 1):
        @pl.when((i & 1) == sl)
        def _():
            step(sl)

    @pl.when(i == n_steps - 1)
    def _():
        wait_tile(1 - (i & 1))


def _combine(p1, p2, gcol, x1, gfin, osorted):
    t, d = x1.shape
    tm = MOVE_TILE
    n_steps = t // tm
    cur = lambda: pl.BlockSpec((tm,), lambda i: (i,), memory_space=pltpu.SMEM)
    nxt = lambda: pl.BlockSpec((tm,), lambda i: (jnp.minimum(i + 1, n_steps - 1),), memory_space=pltpu.SMEM)
    return pl.pallas_call(
        _combine_kernel,
        grid=(n_steps,),
        in_specs=[cur(), cur(), nxt(), nxt(),
                  pl.BlockSpec((tm, LANES), lambda i: (i, 0)),
                  pl.BlockSpec((tm, d), lambda i: (i, 0)),
                  pl.BlockSpec((1, d), lambda i: (0, 0)),
                  pl.BlockSpec(memory_space=pl.ANY)],
        out_specs=pl.BlockSpec((tm, d), lambda i: (i, 0)),
        out_shape=jax.ShapeDtypeStruct((t, d), F32),
        scratch_shapes=[pltpu.VMEM((2, 2, tm * SUBLANES, LANES), F32), pltpu.SemaphoreType.DMA((2,))],
        compiler_params=pltpu.CompilerParams(dimension_semantics=("arbitrary",)),
        name="combine",
    )(p1, p2, p1, p2, gcol, x1, gfin, osorted)


def kernel(x, positions, norm_mix_gain, w_in, ret_gn_gain, sgu_norm_gain, sgu_w, sgu_b, w_out, norm_ffn_gain,
           w_router_group, b_router_group, w_router_expert, b_router_expert, w_expert_gate_up, w_expert_up,
           w_expert_down, final_norm_gain):
    bsz, seq, d = x.shape
    assert w_in.shape[0] == 1, "single-layer block"
    assert d == SUBLANES * LANES, "token-major layout needs one (8, 128) tile per token"
    assert seq % ATTN_TILE == 0 and (2 * bsz * seq) % EXPERT_TILE == 0
    assert bsz * seq // EXPERT_TILE + 1 <= LANES, "an expert's block list must fit one row of the block table"
    t = bsz * seq
    half = HEAD_DIM // 2

    pos4 = positions.reshape(bsz, seq // CHUNK, 1, CHUNK)
    invf = (ROPE_BASE ** (-jnp.arange(half, dtype=F32) * 2.0 / HEAD_DIM)).reshape(half, 1)
    wr = jnp.concatenate([
        jnp.transpose(w_router_expert[0], (0, 2, 1)).reshape(N_EXPERTS, d),
        w_router_group[0].T,
        jnp.zeros((ROUTER_ROWS - N_EXPERTS - N_GROUPS, d), F32)], axis=0)
    wrh = wr.astype(BF16)
    wrl = (wr - wrh.astype(F32)).astype(BF16)
    br = jnp.concatenate([b_router_expert[0].reshape(N_EXPERTS), b_router_group[0],
                          jnp.zeros((ROUTER_ROWS - N_EXPERTS - N_GROUPS,), F32)]).reshape(ROUTER_ROWS, 1)

    x1, keys, cnt, blocks, gcol, xs = _attn_router(
        x, pos4, norm_mix_gain[0].reshape(1, d), w_in[0].astype(BF16), ret_gn_gain[0].reshape(1, RET_WIDTH),
        sgu_norm_gain[0].reshape(1, SGU_WIDTH), sgu_w[0], sgu_b[0].reshape(SGU_GROUPS, CHUNK, 1),
        w_out[0].astype(BF16), invf, norm_ffn_gain[0].reshape(1, d), wrh, wrl, br)

    texp, tin, tout, nalloc = _block_plan(cnt[:, 0].astype(I32), blocks.astype(I32), _total_blocks(t))
    ff = w_expert_gate_up.shape[-1]
    ys = _experts(texp, tin, tout, nalloc, xs,
                  w_expert_gate_up[0].reshape(N_EXPERTS, d, ff),
                  w_expert_up[0].reshape(N_EXPERTS, d, ff),
                  w_expert_down[0].reshape(N_EXPERTS, ff, d))

    out = _combine(keys[0], keys[1], gcol, x1, final_norm_gain.reshape(1, d), ys)
    return out.reshape(bsz, seq, d)
```

```python
import functools
import math

import jax
import jax.numpy as jnp
from jax import lax
from jax.experimental import pallas as pl
from jax.experimental.pallas import tpu as pltpu

F32 = jnp.float32
BF16 = jnp.bfloat16
I32 = jnp.int32

RET_HEADS = 4
HEAD_DIM = 128
CHUNK = 128
SGU_GROUPS = 4
GROUP_DIM = 128
RET_WIDTH = RET_HEADS * HEAD_DIM
SGU_WIDTH = SGU_GROUPS * GROUP_DIM
N_SECTIONS = 6
N_GROUPS = 4
N_PER_GROUP = 8
N_EXPERTS = N_GROUPS * N_PER_GROUP
ROUTER_ROWS = 40
ROPE_BASE = 10000.0
EPS = 1e-6

LANES = 128
SUBLANES = 8
TOKEN_SUBLANES = SUBLANES
TOKEN_DTYPE = F32
D_MODEL = TOKEN_SUBLANES * LANES
VMEM_LIMIT_BYTES = 56 * 1024 * 1024

ATTN_TILE = 512
MOVE_TILE = 256
EXPERT_TILE = 256
DMA_GROUPS = 8
INPUT_SLOTS = 3
_TAIL_PIECES = tuple(EXPERT_TILE >> (b + 1) for b in range(EXPERT_TILE.bit_length() - 1))


def _max_blocks(n_tok):
    return (2 * n_tok) // EXPERT_TILE + N_EXPERTS


def _total_blocks(n_tok):
    return _max_blocks(n_tok) + (2 * ATTN_TILE) // EXPERT_TILE


_NT = (((1,), (1,)), ((), ()))
_TN = (((0,), (0,)), ((), ()))

_LOG_GAMMA = [math.log(1.0 - 2.0 ** (-5.0 - h)) for h in range(RET_HEADS)]
_CHUNK_DECAY = [math.exp(lg * CHUNK) for lg in _LOG_GAMMA]
_K_SCALE = HEAD_DIM ** -0.5


def _rms(x, gain):
    return x * lax.rsqrt(jnp.mean(x * x, axis=-1, keepdims=True) + EPS) * gain


def _gelu(x):
    return 0.5 * x * (1.0 + lax.erf(x * 0.7071067811865476))


def _sigmoid(x):
    return 1.0 / (1.0 + jnp.exp(-x))


def _token_rows(token):
    return pl.ds(pl.multiple_of(token * TOKEN_SUBLANES, TOKEN_SUBLANES), TOKEN_SUBLANES)


def _load_token_major(ref, lead, tok0, n_tok):
    return jnp.concatenate(
        [ref[lead + (pl.ds(tok0 * TOKEN_SUBLANES + sl, n_tok, stride=TOKEN_SUBLANES), slice(None))]
         for sl in range(TOKEN_SUBLANES)], axis=1)


def _store_token_major(ref, value):
    n_tok = value.shape[0]
    for sl in range(TOKEN_SUBLANES):
        ref[pl.ds(sl, n_tok, stride=TOKEN_SUBLANES), :] = value[:, sl * LANES:(sl + 1) * LANES]


def _attn_router_kernel(x_ref, pos_ref, gmix_ref, win_ref, gn_ref, sgn_ref, sw_ref, sb_ref, wout_ref,
                        invf_ref, gffn_ref, wrh_ref, wrl_ref, br_ref,
                        x1_ref, key_ref, cnt_ref, blocks_ref, gcol_ref, xs_ref,
                        proj_ref, mix_ref, attn_ref, rot_ref, state_ref, dintra_ref, qdec_ref, kdec_ref, wc_ref,
                        btab_ref, before_ref, open_ref, nalloc_ref, stage_ref, keyv_ref, keys_ref, zero_ref,
                        metav_ref, meta_ref, ssem, ksem, fsem, *, placeholder_row0, max_blocks):
    ts, d = x_ref.shape
    b = pl.program_id(0)
    s = pl.program_id(1)
    li = b * pl.num_programs(1) + s
    n_lin = pl.num_programs(0) * pl.num_programs(1)
    slot_cur = li & 1
    slot_prev = 1 - slot_cur
    tok_per_group = ts // DMA_GROUPS

    def scatter_group(g):
        for t in range(g * tok_per_group, (g + 1) * tok_per_group):
            for k in (0, 1):
                pltpu.make_async_copy(stage_ref.at[slot_prev, _token_rows(t)],
                                      xs_ref.at[_token_rows(keys_ref[slot_prev, k, t])],
                                      ssem.at[slot_prev]).start(priority=k)

    def wait_scatter(sl):
        for _ in (0, 1):
            pltpu.make_async_copy(stage_ref.at[sl], xs_ref.at[pl.ds(0, ts * TOKEN_SUBLANES)], ssem.at[sl]).wait()

    def keys_copy(sl):
        return pltpu.make_async_copy(keyv_ref.at[sl], keys_ref.at[sl], ksem.at[sl])

    @pl.when((b == 0) & (s == 0))
    def _init_tables():
        i = lax.broadcasted_iota(I32, (CHUNK, CHUNK), 0)
        j = lax.broadcasted_iota(I32, (CHUNK, CHUNK), 1)
        diff = (i - j).astype(F32)
        fi = i.astype(F32)
        for h in range(RET_HEADS):
            lg = _LOG_GAMMA[h]
            dintra_ref[h] = jnp.where(i >= j, jnp.exp(lg * diff), 0.0) * _K_SCALE
            qdec_ref[h] = jnp.exp(lg * (fi + 1.0))
            kdec_ref[h] = jnp.exp(lg * (CHUNK - 1.0 - fi)) * _K_SCALE
        for g in range(SGU_GROUPS):
            wc_ref[g] = jnp.where(i >= j, sw_ref[g], 0.0).astype(BF16)
            btab_ref[g] = jnp.broadcast_to(sb_ref[g], (CHUNK, GROUP_DIM))
        ta = lax.broadcasted_iota(I32, (ts, ts), 0)
        tb = lax.broadcasted_iota(I32, (ts, ts), 1)
        before_ref[...] = jnp.where(ta < tb, 1.0, 0.0).astype(BF16)
        cnt_ref[...] = jnp.zeros_like(cnt_ref)
        blocks_ref[...] = jnp.zeros_like(blocks_ref)
        open_ref[...] = jnp.zeros_like(open_ref)
        nalloc_ref[...] = jnp.zeros_like(nalloc_ref)
        stage_ref[1] = jnp.zeros(stage_ref.shape[1:], TOKEN_DTYPE)
        kk = lax.broadcasted_iota(I32, (2, ts), 0)
        tt = lax.broadcasted_iota(I32, (2, ts), 1)
        keyv_ref[1] = placeholder_row0 + 2 * tt + kk
        pltpu.sync_copy(keyv_ref.at[1], keys_ref.at[1])

    @pl.when(li > 0)
    def _():
        keys_copy(slot_prev).wait()

    @pl.when(s == 0)
    def _reset_state():
        state_ref[...] = jnp.zeros_like(state_ref)

    n_chunks = ts // CHUNK
    pair = 2 * HEAD_DIM
    x = x_ref[...]
    h = _rms(x, gmix_ref[...]).astype(BF16)

    def project(sec, col0, width):
        cols = slice(col0, col0 + width)
        proj_ref[sec, :, cols] = jnp.dot(h, win_ref[:, sec * RET_WIDTH + col0:sec * RET_WIDTH + col0 + width],
                                         preferred_element_type=F32)

    def spatial_gating(c):
        rows = slice(c * CHUNK, (c + 1) * CHUNK)
        for g in range(SGU_GROUPS):
            gs = slice(g * GROUP_DIM, (g + 1) * GROUP_DIM)
            u = _gelu(proj_ref[4, rows, gs])
            v = _gelu(proj_ref[5, rows, gs])
            vn = v * lax.rsqrt(jnp.mean(v * v, axis=-1, keepdims=True) + EPS) * sgn_ref[:, gs]
            sg = jnp.dot(wc_ref[g], vn.astype(BF16), preferred_element_type=F32) + btab_ref[g]
            mix_ref[rows, RET_WIDTH + g * GROUP_DIM:RET_WIDTH + (g + 1) * GROUP_DIM] = (u * sg).astype(BF16)

    def rotary_tables(c):
        ang_t = invf_ref[...] * pos_ref[c].astype(F32)
        cos_t = jnp.cos(ang_t)
        sin_t = jnp.sin(ang_t)
        rot_ref[0, c] = jnp.concatenate([cos_t, cos_t], axis=0).T
        rot_ref[1, c] = jnp.concatenate([-sin_t, sin_t], axis=0).T

    def retention(c, hd):
        rows = slice(c * CHUNK, (c + 1) * CHUNK)
        hs = slice(hd * HEAD_DIM, (hd + 1) * HEAD_DIM)
        cosf = rot_ref[0, c]
        sinf = rot_ref[1, c]
        q = proj_ref[0, rows, hs]
        k = proj_ref[1, rows, hs]
        vb = proj_ref[2, rows, hs].astype(BF16)
        gate = proj_ref[3, rows, hs]
        qr = q * cosf + pltpu.roll(q, HEAD_DIM // 2, 1) * sinf
        kr = k * cosf + pltpu.roll(k, HEAD_DIM // 2, 1) * sinf
        scores = lax.dot_general(qr.astype(BF16), kr.astype(BF16), _NT,
                                 preferred_element_type=F32) * dintra_ref[hd]
        st = state_ref[hd]
        o = (jnp.dot(scores.astype(BF16), vb, preferred_element_type=F32)
             + jnp.dot((qr * qdec_ref[hd]).astype(BF16), st.astype(BF16), preferred_element_type=F32))
        kv = lax.dot_general((kr * kdec_ref[hd]).astype(BF16), vb, _TN, preferred_element_type=F32)
        state_ref[hd] = st * _CHUNK_DECAY[hd] + kv
        dv = o - jnp.mean(o, axis=-1, keepdims=True)
        on = dv * lax.rsqrt(jnp.mean(dv * dv, axis=-1, keepdims=True) + EPS) * gn_ref[:, hs]
        mix_ref[rows, hs] = (gate * _sigmoid(gate) * on).astype(BF16)

    def out_projection(col0, width):
        return jnp.dot(mix_ref[:, col0:col0 + width], wout_ref[col0:col0 + width, :], preferred_element_type=F32)

    project(4, 0, SGU_WIDTH)
    project(5, 0, SGU_WIDTH)
    scatter_group(0)
    for sec in range(4):
        project(sec, 0, pair)
    scatter_group(1)
    for c in range(n_chunks):
        spatial_gating(c)
    scatter_group(2)
    attn_ref[...] = out_projection(RET_WIDTH, SGU_WIDTH)
    for sec in range(4):
        project(sec, pair, pair)
    scatter_group(3)
    for c in range(n_chunks):
        rotary_tables(c)
    scatter_group(4)
    for c in range(n_chunks):
        retention(c, 0)
        retention(c, 1)
    scatter_group(5)
    attn_ref[...] += out_projection(0, pair)
    scatter_group(6)
    for c in range(n_chunks):
        retention(c, 2)
        retention(c, 3)
    scatter_group(7)
    x1 = x_ref[...] + (attn_ref[...] + out_projection(pair, pair))
    x1_ref[...] = x1
    h2 = _rms(x1, gffn_ref[...])

    h2h = h2.astype(BF16)
    h2l = (h2 - h2h.astype(F32)).astype(BF16)
    wrh = wrh_ref[...]
    logits = (lax.dot_general(wrh, h2h, _NT, preferred_element_type=F32)
              + lax.dot_general(wrh, h2l, _NT, preferred_element_type=F32)
              + lax.dot_general(wrl_ref[...], h2h, _NT, preferred_element_type=F32)
              + br_ref[...])
    el = logits[0:N_EXPERTS]
    gl = logits[N_EXPERTS:N_EXPERTS + N_GROUPS]
    rg = lax.broadcasted_iota(I32, (N_GROUPS, ts), 0).astype(F32)
    gmax = jnp.max(gl, axis=0, keepdims=True)
    gidx = jnp.min(jnp.where(gl == gmax, rg, float(N_GROUPS)), axis=0, keepdims=True)
    gweight = 1.0 / jnp.sum(jnp.exp(gl - gmax), axis=0, keepdims=True)
    re_i = lax.broadcasted_iota(I32, (N_EXPERTS, ts), 0)
    re = re_i.astype(F32)
    in_group = (re_i // N_PER_GROUP).astype(F32) == gidx
    neg = -jnp.inf
    sel = jnp.where(in_group, el, neg)
    m1 = jnp.max(sel, axis=0, keepdims=True)
    i1 = jnp.min(jnp.where(sel == m1, re, float(N_EXPERTS)), axis=0, keepdims=True)
    sel2 = jnp.where(re == i1, neg, sel)
    m2 = jnp.max(sel2, axis=0, keepdims=True)
    i2 = jnp.min(jnp.where(sel2 == m2, re, float(N_EXPERTS)), axis=0, keepdims=True)
    e2 = jnp.exp(m2 - m1)
    w1 = 1.0 / (1.0 + e2)

    hit1 = re == i1
    hit2 = re == i2
    onehot = jnp.where(hit1, 1.0, jnp.where(hit2, 1.0, 0.0))
    seen = cnt_ref[:, 0:1] + jnp.dot(onehot.astype(BF16), before_ref[...], preferred_element_type=F32)
    inv_blk = 1.0 / EXPERT_TILE
    c0 = cnt_ref[:, 0:1]
    tile_cnt = jnp.sum(onehot, axis=1, keepdims=True)
    c1 = c0 + tile_cnt
    up0 = jnp.floor((c0 + (EXPERT_TILE - 1.0)) * inv_blk)
    n_new = jnp.floor((c1 + (EXPERT_TILE - 1.0)) * inv_blk) - up0
    ea = lax.broadcasted_iota(I32, (N_EXPERTS, N_EXPERTS), 0)
    eb = lax.broadcasted_iota(I32, (N_EXPERTS, N_EXPERTS), 1)
    earlier = jnp.where(eb < ea, 1.0, 0.0).astype(BF16)
    n_new_b = jnp.broadcast_to(n_new, (N_EXPERTS, LANES))
    new1 = nalloc_ref[:, 0:1] + jnp.dot(earlier, n_new_b.astype(BF16), preferred_element_type=F32)[:, 0:1]
    new2 = new1 + 1.0
    open_blk = open_ref[:, 0:1]
    rank_blk = jnp.floor(seen * inv_blk)
    which = rank_blk - up0
    block_of = jnp.where(which < 0.0, open_blk, jnp.where(which == 0.0, new1, new2))
    row_of = block_of * float(EXPERT_TILE) + (seen - rank_blk * float(EXPERT_TILE))
    keys = jnp.concatenate([jnp.sum(jnp.where(hit1, row_of, 0.0), axis=0, keepdims=True),
                            jnp.sum(jnp.where(hit2, row_of, 0.0), axis=0, keepdims=True)], axis=0).astype(I32)
    key_ref[...] = keys
    lane = lax.broadcasted_iota(I32, (N_EXPERTS, LANES), 1).astype(F32)
    blocks = blocks_ref[...]
    blocks = jnp.where(lane == up0, jnp.where(n_new >= 1.0, new1, blocks), blocks)
    blocks = jnp.where(lane == up0 + 1.0, jnp.where(n_new == 2.0, new2, blocks), blocks)
    blocks_ref[...] = blocks
    open_ref[...] = jnp.broadcast_to(jnp.where(n_new == 0.0, open_blk, jnp.where(n_new == 1.0, new1, new2)),
                                     open_ref.shape)
    nalloc_ref[...] += jnp.sum(n_new_b, axis=0, keepdims=True)
    cnt_ref[...] += jnp.broadcast_to(tile_cnt, cnt_ref.shape)
    gates_t = jnp.concatenate([gweight * w1, gweight * (e2 * w1), jnp.zeros((LANES - 2, ts), F32)], axis=0)
    gcol_ref[...] = gates_t.T

    @pl.when(li > 0)
    def _():
        wait_scatter(slot_cur)
    _store_token_major(stage_ref.at[slot_cur], h2)
    keyv_ref[slot_cur] = keys
    keys_copy(slot_cur).start()

    @pl.when(li == n_lin - 1)
    def _():
        keys_copy(slot_cur).wait()

        def last_rows(t, carry):
            for k in (0, 1):
                pltpu.make_async_copy(stage_ref.at[slot_cur, _token_rows(t)],
                                      xs_ref.at[_token_rows(keys_ref[slot_cur, k, t])],
                                      ssem.at[slot_cur]).start(priority=k)
            return carry
        lax.fori_loop(0, ts, last_rows, 0, unroll=4)

        total = cnt_ref[...]
        used = total - jnp.floor(total * inv_blk) * float(EXPERT_TILE)
        meta_lane = lax.broadcasted_iota(I32, (N_EXPERTS, LANES), 1)
        metav_ref[...] = jnp.where(
            meta_lane == 0, open_ref[...] * float(EXPERT_TILE) + used,
            jnp.where(meta_lane == 1, jnp.where(used > 0.0, float(EXPERT_TILE) - used, 0.0),
                      nalloc_ref[...])).astype(I32)
        pltpu.sync_copy(metav_ref, meta_ref)
        zero_ref[...] = jnp.zeros_like(zero_ref)

        def piece(row, n_rows):
            return pltpu.make_async_copy(
                zero_ref.at[pl.ds(0, n_rows * TOKEN_SUBLANES)],
                xs_ref.at[pl.ds(pl.multiple_of(row * TOKEN_SUBLANES, TOKEN_SUBLANES), n_rows * TOKEN_SUBLANES)],
                fsem)

        def for_each_piece(action):
            for e in range(N_EXPERTS):
                row = meta_ref[e, 0]
                length = meta_ref[e, 1]
                for n_rows in _TAIL_PIECES:
                    @pl.when((length & n_rows) != 0)
                    def _():
                        action(piece(row, n_rows))
                    row = row + (length & n_rows)
            for blk in range(max_blocks - N_EXPERTS, max_blocks):
                @pl.when(blk >= meta_ref[0, 2])
                def _():
                    action(piece(blk * EXPERT_TILE, EXPERT_TILE))

        for_each_piece(lambda cp: cp.start())
        for_each_piece(lambda cp: cp.wait())
        wait_scatter(slot_prev)
        wait_scatter(slot_cur)


def _attn_router(x, pos4, gmix, win, gn, sgn, sw, sb, wout, invf, gffn, wrh, wrl, br):
    bsz, seq, d = x.shape
    ts = ATTN_TILE
    ns = seq // ts
    t = bsz * seq
    const2 = lambda b, s: (0, 0)
    const3 = lambda b, s: (0, 0, 0)
    single = dict(pipeline_mode=pl.Buffered(1))
    in_specs = [
        pl.BlockSpec((None, ts, d), lambda b, s: (b, s, 0)),
        pl.BlockSpec((None, ts // CHUNK, 1, CHUNK), lambda b, s: (b, s, 0, 0)),
        pl.BlockSpec((1, d), const2),
        pl.BlockSpec(win.shape, const2, **single),
        pl.BlockSpec((1, RET_WIDTH), const2),
        pl.BlockSpec((1, SGU_WIDTH), const2),
        pl.BlockSpec(sw.shape, const3),
        pl.BlockSpec(sb.shape, const3),
        pl.BlockSpec(wout.shape, const2, **single),
        pl.BlockSpec(invf.shape, const2),
        pl.BlockSpec((1, d), const2),
        pl.BlockSpec(wrh.shape, const2),
        pl.BlockSpec(wrl.shape, const2),
        pl.BlockSpec(br.shape, const2),
    ]
    tok = lambda b, s: (b * ns + s, 0)
    out_specs = [
        pl.BlockSpec((ts, d), tok),
        pl.BlockSpec((2, ts), lambda b, s: (0, b * ns + s)),
        pl.BlockSpec((N_EXPERTS, LANES), const2),
        pl.BlockSpec((N_EXPERTS, LANES), const2),
        pl.BlockSpec((ts, LANES), tok),
        pl.BlockSpec(memory_space=pl.ANY),
    ]
    out_shape = [
        jax.ShapeDtypeStruct((t, d), F32),
        jax.ShapeDtypeStruct((2, t), I32),
        jax.ShapeDtypeStruct((N_EXPERTS, LANES), F32),
        jax.ShapeDtypeStruct((N_EXPERTS, LANES), F32),
        jax.ShapeDtypeStruct((t, LANES), F32),
        jax.ShapeDtypeStruct((_total_blocks(t) * EXPERT_TILE * TOKEN_SUBLANES, LANES), TOKEN_DTYPE),
    ]
    scratch = [
        pltpu.VMEM((N_SECTIONS, ts, RET_WIDTH), F32),
        pltpu.VMEM((ts, RET_WIDTH + SGU_WIDTH), BF16),
        pltpu.VMEM((ts, d), F32),
        pltpu.VMEM((2, ts // CHUNK, CHUNK, HEAD_DIM), F32),
        pltpu.VMEM((RET_HEADS, HEAD_DIM, HEAD_DIM), F32),
        pltpu.VMEM((RET_HEADS, CHUNK, CHUNK), F32),
        pltpu.VMEM((RET_HEADS, CHUNK, HEAD_DIM), F32),
        pltpu.VMEM((RET_HEADS, CHUNK, HEAD_DIM), F32),
        pltpu.VMEM((SGU_GROUPS, CHUNK, CHUNK), BF16),
        pltpu.VMEM((SGU_GROUPS, CHUNK, GROUP_DIM), F32),
        pltpu.VMEM((ts, ts), BF16),
        pltpu.VMEM((N_EXPERTS, LANES), F32),
        pltpu.VMEM((N_EXPERTS, LANES), F32),
        pltpu.VMEM((2, ts * TOKEN_SUBLANES, LANES), TOKEN_DTYPE),
        pltpu.VMEM((2, 2, ts), I32),
        pltpu.SMEM((2, 2, ts), I32),
        pltpu.VMEM((EXPERT_TILE * TOKEN_SUBLANES, LANES), TOKEN_DTYPE),
        pltpu.VMEM((N_EXPERTS, LANES), I32),
        pltpu.SMEM((N_EXPERTS, LANES), I32),
        pltpu.SemaphoreType.DMA((2,)),
        pltpu.SemaphoreType.DMA((2,)),
        pltpu.SemaphoreType.DMA,
    ]
    return pl.pallas_call(
        functools.partial(_attn_router_kernel, placeholder_row0=_max_blocks(t) * EXPERT_TILE,
                          max_blocks=_max_blocks(t)),
        grid=(bsz, ns),
        in_specs=in_specs,
        out_specs=out_specs,
        out_shape=out_shape,
        scratch_shapes=scratch,
        compiler_params=pltpu.CompilerParams(
            dimension_semantics=("arbitrary", "arbitrary"), vmem_limit_bytes=VMEM_LIMIT_BYTES),
        name="attn_router",
    )(x, pos4, gmix, win, gn, sgn, sw, sb, wout, invf, gffn, wrh, wrl, br)


def _expert_kernel(texp_ref, tin_ref, tout_ref, nt_ref, xs_ref, wg_ref, wu_ref, wd_ref, ys_ref,
                   xbuf_ref, wgb_ref, wub_ref, wdb_ref, sems):
    i = pl.program_id(0)
    n_steps = pl.num_programs(0)
    rows = xbuf_ref.shape[1]
    tm = rows // TOKEN_SUBLANES
    valid = i < nt_ref[0]
    ahead = INPUT_SLOTS - 1

    def fetch(step):
        src = xs_ref.at[pl.ds(pl.multiple_of(tin_ref[step] * rows, rows), rows)]
        slot = lax.rem(step, INPUT_SLOTS)
        return pltpu.make_async_copy(src, xbuf_ref.at[slot], sems.at[slot])

    @pl.when(i == 0)
    def _():
        for step in range(ahead):
            fetch(step).start()

    @pl.when(i + ahead < n_steps)
    def _():
        fetch(i + ahead).start()

    @pl.when(valid & ((i == 0) | (texp_ref[i] != texp_ref[jnp.maximum(i - 1, 0)])))
    def _():
        wgb_ref[...] = wg_ref[...].astype(BF16)
        wub_ref[...] = wu_ref[...].astype(BF16)
        wdb_ref[...] = wd_ref[...].astype(BF16)

    fetch(i).wait()

    @pl.when(valid)
    def _():
        xt = _load_token_major(xbuf_ref, (lax.rem(i, INPUT_SLOTS),), 0, tm).astype(BF16)
        a = jnp.dot(xt, wgb_ref[...], preferred_element_type=F32)
        bb = jnp.dot(xt, wub_ref[...], preferred_element_type=F32)
        act = a * _sigmoid(a) * bb
        o = jnp.dot(act.astype(BF16), wdb_ref[...], preferred_element_type=F32)
        _store_token_major(ys_ref, o)

    @pl.when(jnp.logical_not(valid))
    def _():
        ys_ref[...] = jnp.zeros_like(ys_ref)


def _experts(texp, tin, tout, ntiles, xs, wg, wu, wd):
    d = D_MODEL
    f = wg.shape[-1]
    rows = EXPERT_TILE * TOKEN_SUBLANES
    n_steps = texp.shape[0]
    assert n_steps >= INPUT_SLOTS
    grid_spec = pltpu.PrefetchScalarGridSpec(
        num_scalar_prefetch=4,
        grid=(n_steps,),
        in_specs=[
            pl.BlockSpec(memory_space=pl.ANY),
            pl.BlockSpec((None, d, f), lambda i, te, ti, to, n: (te[i], 0, 0)),
            pl.BlockSpec((None, d, f), lambda i, te, ti, to, n: (te[i], 0, 0)),
            pl.BlockSpec((None, f, d), lambda i, te, ti, to, n: (te[i], 0, 0)),
        ],
        out_specs=pl.BlockSpec((rows, LANES), lambda i, te, ti, to, n: (to[i], 0)),
        scratch_shapes=[pltpu.VMEM((INPUT_SLOTS, rows, LANES), TOKEN_DTYPE),
                        pltpu.VMEM((d, f), BF16), pltpu.VMEM((d, f), BF16), pltpu.VMEM((f, d), BF16),
                        pltpu.SemaphoreType.DMA((INPUT_SLOTS,))],
    )
    return pl.pallas_call(
        _expert_kernel,
        grid_spec=grid_spec,
        out_shape=jax.ShapeDtypeStruct(xs.shape, xs.dtype),
        compiler_params=pltpu.CompilerParams(
            dimension_semantics=("arbitrary",), vmem_limit_bytes=VMEM_LIMIT_BYTES),
        name="experts",
    )(texp, tin, tout, ntiles, xs, wg, wu, wd)


def _block_plan(cnt, blocks, n_steps):
    n_e = (cnt + EXPERT_TILE - 1) // EXPERT_TILE
    ends = jnp.cumsum(n_e)
    total = ends[-1]
    step = jnp.arange(n_steps, dtype=I32)
    p = jnp.minimum(step, total - 1)
    e = jnp.sum((p[:, None] >= ends[None, :]).astype(I32), axis=1)
    is_e = e[:, None] == jnp.arange(N_EXPERTS, dtype=I32)[None, :]
    k = p - jnp.sum(jnp.where(is_e, (ends - n_e)[None, :], 0), axis=1)
    rows = jnp.sum(jnp.where(is_e[:, :, None], blocks[None, :, :], 0), axis=1)
    blk_in = jnp.sum(jnp.where(k[:, None] == jnp.arange(LANES, dtype=I32)[None, :], rows, 0), axis=1)
    blk_out = jnp.where(step < total, blk_in, step)
    return e.astype(I32), blk_in.astype(I32), blk_out.astype(I32), total.reshape(1).astype(I32)


def _combine_kernel(p1_ref, p2_ref, p1n_ref, p2n_ref, gcol_ref, x1_ref, gfin_ref, os_ref, out_ref, buf_ref, sems):
    i = pl.program_id(0)
    n_steps = pl.num_programs(0)
    tm = x1_ref.shape[0]
    per_group = tm // DMA_GROUPS

    def row_copy(src_row, j, sl, k):
        return pltpu.make_async_copy(os_ref.at[_token_rows(src_row)], buf_ref.at[sl, k, _token_rows(j)], sems.at[sl])

    def start_group(r1_ref, r2_ref, sl, g):
        for j in range(g * per_group, (g + 1) * per_group):
            row_copy(r1_ref[j], j, sl, 0).start(priority=0)
            row_copy(r2_ref[j], j, sl, 1).start(priority=1)

    def wait_tile(sl):
        for k in (0, 1):
            pltpu.make_async_copy(os_ref.at[pl.ds(0, tm * TOKEN_SUBLANES)], buf_ref.at[sl, k], sems.at[sl]).wait()

    @pl.when(i == 0)
    def _():
        for g in range(DMA_GROUPS):
            start_group(p1_ref, p2_ref, 0, g)

    def step(sl):
        wait_tile(sl)
        for g in range(DMA_GROUPS):
            rows = slice(g * per_group, (g + 1) * per_group)
            o1 = _load_token_major(buf_ref, (sl, 0), g * per_group, per_group)
            o2 = _load_token_major(buf_ref, (sl, 1), g * per_group, per_group)
            gates = gcol_ref[rows, :]
            y = gates[:, 0:1] * o1 + gates[:, 1:2] * o2
            out_ref[rows, :] = _rms(x1_ref[rows, :] + y, gfin_ref[...])
            start_group(p1n_ref, p2n_ref, 1 - sl, g)

    for sl in (0, 1):
        @pl.when((i & 1) == sl)
        def _():
            step(sl)

    @pl.when(i == n_steps - 1)
    def _():
        wait_tile(1 - (i & 1))


def _combine(p1, p2, gcol, x1, gfin, osorted):
    t, d = x1.shape
    tm = MOVE_TILE
    n_steps = t // tm
    cur = lambda: pl.BlockSpec((tm,), lambda i: (i,), memory_space=pltpu.SMEM)
    nxt = lambda: pl.BlockSpec((tm,), lambda i: (jnp.minimum(i + 1, n_steps - 1),), memory_space=pltpu.SMEM)
    return pl.pallas_call(
        _combine_kernel,
        grid=(n_steps,),
        in_specs=[cur(), cur(), nxt(), nxt(),
                  pl.BlockSpec((tm, LANES), lambda i: (i, 0)),
                  pl.BlockSpec((tm, d), lambda i: (i, 0)),
                  pl.BlockSpec((1, d), lambda i: (0, 0)),
                  pl.BlockSpec(memory_space=pl.ANY)],
        out_specs=pl.BlockSpec((tm, d), lambda i: (i, 0)),
        out_shape=jax.ShapeDtypeStruct((t, d), F32),
        scratch_shapes=[pltpu.VMEM((2, 2, tm * TOKEN_SUBLANES, LANES), TOKEN_DTYPE), pltpu.SemaphoreType.DMA((2,))],
        compiler_params=pltpu.CompilerParams(dimension_semantics=("arbitrary",)),
        name="combine",
    )(p1, p2, p1, p2, gcol, x1, gfin, osorted)


def kernel(x, positions, norm_mix_gain, w_in, ret_gn_gain, sgu_norm_gain, sgu_w, sgu_b, w_out, norm_ffn_gain,
           w_router_group, b_router_group, w_router_expert, b_router_expert, w_expert_gate_up, w_expert_up,
           w_expert_down, final_norm_gain):
    bsz, seq, d = x.shape
    assert w_in.shape[0] == 1, "single-layer block"
    assert d == D_MODEL, "the packed token-major layout is written for 1024 features"
    assert seq % ATTN_TILE == 0 and (2 * bsz * seq) % EXPERT_TILE == 0
    assert bsz * seq // EXPERT_TILE + 1 <= LANES, "an expert's block list must fit one row of the block table"
    t = bsz * seq
    half = HEAD_DIM // 2

    pos4 = positions.reshape(bsz, seq // CHUNK, 1, CHUNK)
    invf = (ROPE_BASE ** (-jnp.arange(half, dtype=F32) * 2.0 / HEAD_DIM)).reshape(half, 1)
    wr = jnp.concatenate([
        jnp.transpose(w_router_expert[0], (0, 2, 1)).reshape(N_EXPERTS, d),
        w_router_group[0].T,
        jnp.zeros((ROUTER_ROWS - N_EXPERTS - N_GROUPS, d), F32)], axis=0)
    wrh = wr.astype(BF16)
    wrl = (wr - wrh.astype(F32)).astype(BF16)
    br = jnp.concatenate([b_router_expert[0].reshape(N_EXPERTS), b_router_group[0],
                          jnp.zeros((ROUTER_ROWS - N_EXPERTS - N_GROUPS,), F32)]).reshape(ROUTER_ROWS, 1)

    x1, keys, cnt, blocks, gcol, xs = _attn_router(
        x, pos4, norm_mix_gain[0].reshape(1, d), w_in[0].astype(BF16), ret_gn_gain[0].reshape(1, RET_WIDTH),
        sgu_norm_gain[0].reshape(1, SGU_WIDTH), sgu_w[0], sgu_b[0].reshape(SGU_GROUPS, CHUNK, 1),
        w_out[0].astype(BF16), invf, norm_ffn_gain[0].reshape(1, d), wrh, wrl, br)

    texp, tin, tout, nalloc = _block_plan(cnt[:, 0].astype(I32), blocks.astype(I32), _total_blocks(t))
    ff = w_expert_gate_up.shape[-1]
    ys = _experts(texp, tin, tout, nalloc, xs,
                  w_expert_gate_up[0].reshape(N_EXPERTS, d, ff),
                  w_expert_up[0].reshape(N_EXPERTS, d, ff),
                  w_expert_down[0].reshape(N_EXPERTS, ff, d))

    out = _combine(keys[0], keys[1], gcol, x1, final_norm_gain.reshape(1, d), ys)
    return out.reshape(bsz, seq, d)
```

```python
import functools
import math

import jax
import jax.numpy as jnp
from jax import lax
from jax.experimental import pallas as pl
from jax.experimental.pallas import tpu as pltpu

F32 = jnp.float32
BF16 = jnp.bfloat16
I32 = jnp.int32

RET_HEADS = 4
HEAD_DIM = 128
CHUNK = 128
SGU_GROUPS = 4
GROUP_DIM = 128
RET_WIDTH = RET_HEADS * HEAD_DIM
SGU_WIDTH = SGU_GROUPS * GROUP_DIM
N_SECTIONS = 6
N_GROUPS = 4
N_PER_GROUP = 8
N_EXPERTS = N_GROUPS * N_PER_GROUP
ROUTER_ROWS = 40
ROPE_BASE = 10000.0
EPS = 1e-6

LANES = 128
SUBLANES = 8
TOKEN_SUBLANES = SUBLANES
TOKEN_DTYPE = F32
D_MODEL = TOKEN_SUBLANES * LANES
VMEM_LIMIT_BYTES = 56 * 1024 * 1024

ATTN_TILE = 512
MOVE_TILE = 256
EXPERT_TILE = 512
DMA_GROUPS = 8
INPUT_SLOTS = 3
_TAIL_PIECES = tuple(EXPERT_TILE >> (b + 1) for b in range(EXPERT_TILE.bit_length() - 1))


def _max_blocks(n_tok):
    return (2 * n_tok) // EXPERT_TILE + N_EXPERTS


def _total_blocks(n_tok):
    return _max_blocks(n_tok) + (2 * ATTN_TILE) // EXPERT_TILE


_NT = (((1,), (1,)), ((), ()))
_TN = (((0,), (0,)), ((), ()))

_LOG_GAMMA = [math.log(1.0 - 2.0 ** (-5.0 - h)) for h in range(RET_HEADS)]
_CHUNK_DECAY = [math.exp(lg * CHUNK) for lg in _LOG_GAMMA]
_K_SCALE = HEAD_DIM ** -0.5


def _rms(x, gain):
    return x * lax.rsqrt(jnp.mean(x * x, axis=-1, keepdims=True) + EPS) * gain


def _gelu(x):
    return 0.5 * x * (1.0 + lax.erf(x * 0.7071067811865476))


def _sigmoid(x):
    return 1.0 / (1.0 + jnp.exp(-x))


def _token_rows(token):
    return pl.ds(pl.multiple_of(token * TOKEN_SUBLANES, TOKEN_SUBLANES), TOKEN_SUBLANES)


def _load_token_major(ref, lead, tok0, n_tok):
    return jnp.concatenate(
        [ref[lead + (pl.ds(tok0 * TOKEN_SUBLANES + sl, n_tok, stride=TOKEN_SUBLANES), slice(None))]
         for sl in range(TOKEN_SUBLANES)], axis=1)


def _store_token_major(ref, value):
    n_tok = value.shape[0]
    for sl in range(TOKEN_SUBLANES):
        ref[pl.ds(sl, n_tok, stride=TOKEN_SUBLANES), :] = value[:, sl * LANES:(sl + 1) * LANES]


def _attn_router_kernel(x_ref, pos_ref, gmix_ref, win_ref, gn_ref, sgn_ref, sw_ref, sb_ref, wout_ref,
                        invf_ref, gffn_ref, wrh_ref, wrl_ref, br_ref,
                        x1_ref, key_ref, cnt_ref, blocks_ref, gcol_ref, xs_ref,
                        proj_ref, mix_ref, attn_ref, rot_ref, state_ref, dintra_ref, qdec_ref, kdec_ref, wc_ref,
                        btab_ref, before_ref, open_ref, nalloc_ref, stage_ref, keyv_ref, keys_ref, zero_ref,
                        metav_ref, meta_ref, ssem, ksem, fsem, *, placeholder_row0, max_blocks):
    ts, d = x_ref.shape
    b = pl.program_id(0)
    s = pl.program_id(1)
    li = b * pl.num_programs(1) + s
    n_lin = pl.num_programs(0) * pl.num_programs(1)
    slot_cur = li & 1
    slot_prev = 1 - slot_cur
    tok_per_group = ts // DMA_GROUPS

    def scatter_group(g):
        for t in range(g * tok_per_group, (g + 1) * tok_per_group):
            for k in (0, 1):
                pltpu.make_async_copy(stage_ref.at[slot_prev, _token_rows(t)],
                                      xs_ref.at[_token_rows(keys_ref[slot_prev, k, t])],
                                      ssem.at[slot_prev]).start(priority=k)

    def wait_scatter(sl):
        for _ in (0, 1):
            pltpu.make_async_copy(stage_ref.at[sl], xs_ref.at[pl.ds(0, ts * TOKEN_SUBLANES)], ssem.at[sl]).wait()

    def keys_copy(sl):
        return pltpu.make_async_copy(keyv_ref.at[sl], keys_ref.at[sl], ksem.at[sl])

    @pl.when((b == 0) & (s == 0))
    def _init_tables():
        i = lax.broadcasted_iota(I32, (CHUNK, CHUNK), 0)
        j = lax.broadcasted_iota(I32, (CHUNK, CHUNK), 1)
        diff = (i - j).astype(F32)
        fi = i.astype(F32)
        for h in range(RET_HEADS):
            lg = _LOG_GAMMA[h]
            dintra_ref[h] = jnp.where(i >= j, jnp.exp(lg * diff), 0.0) * _K_SCALE
            qdec_ref[h] = jnp.exp(lg * (fi + 1.0))
            kdec_ref[h] = jnp.exp(lg * (CHUNK - 1.0 - fi)) * _K_SCALE
        for g in range(SGU_GROUPS):
            wc_ref[g] = jnp.where(i >= j, sw_ref[g], 0.0).astype(BF16)
            btab_ref[g] = jnp.broadcast_to(sb_ref[g], (CHUNK, GROUP_DIM))
        ta = lax.broadcasted_iota(I32, (ts, ts), 0)
        tb = lax.broadcasted_iota(I32, (ts, ts), 1)
        before_ref[...] = jnp.where(ta < tb, 1.0, 0.0).astype(BF16)
        cnt_ref[...] = jnp.zeros_like(cnt_ref)
        blocks_ref[...] = jnp.zeros_like(blocks_ref)
        open_ref[...] = jnp.zeros_like(open_ref)
        nalloc_ref[...] = jnp.zeros_like(nalloc_ref)
        stage_ref[1] = jnp.zeros(stage_ref.shape[1:], TOKEN_DTYPE)
        kk = lax.broadcasted_iota(I32, (2, ts), 0)
        tt = lax.broadcasted_iota(I32, (2, ts), 1)
        keyv_ref[1] = placeholder_row0 + 2 * tt + kk
        pltpu.sync_copy(keyv_ref.at[1], keys_ref.at[1])

    @pl.when(li > 0)
    def _():
        keys_copy(slot_prev).wait()

    @pl.when(s == 0)
    def _reset_state():
        state_ref[...] = jnp.zeros_like(state_ref)

    n_chunks = ts // CHUNK
    pair = 2 * HEAD_DIM
    x = x_ref[...]
    h = _rms(x, gmix_ref[...]).astype(BF16)

    def project(sec, col0, width):
        cols = slice(col0, col0 + width)
        proj_ref[sec, :, cols] = jnp.dot(h, win_ref[:, sec * RET_WIDTH + col0:sec * RET_WIDTH + col0 + width],
                                         preferred_element_type=F32)

    def spatial_gating(c):
        rows = slice(c * CHUNK, (c + 1) * CHUNK)
        for g in range(SGU_GROUPS):
            gs = slice(g * GROUP_DIM, (g + 1) * GROUP_DIM)
            u = _gelu(proj_ref[4, rows, gs])
            v = _gelu(proj_ref[5, rows, gs])
            vn = v * lax.rsqrt(jnp.mean(v * v, axis=-1, keepdims=True) + EPS) * sgn_ref[:, gs]
            sg = jnp.dot(wc_ref[g], vn.astype(BF16), preferred_element_type=F32) + btab_ref[g]
            mix_ref[rows, RET_WIDTH + g * GROUP_DIM:RET_WIDTH + (g + 1) * GROUP_DIM] = (u * sg).astype(BF16)

    def rotary_tables(c):
        ang_t = invf_ref[...] * pos_ref[c].astype(F32)
        cos_t = jnp.cos(ang_t)
        sin_t = jnp.sin(ang_t)
        rot_ref[0, c] = jnp.concatenate([cos_t, cos_t], axis=0).T
        rot_ref[1, c] = jnp.concatenate([-sin_t, sin_t], axis=0).T

    def retention(c, hd):
        rows = slice(c * CHUNK, (c + 1) * CHUNK)
        hs = slice(hd * HEAD_DIM, (hd + 1) * HEAD_DIM)
        cosf = rot_ref[0, c]
        sinf = rot_ref[1, c]
        q = proj_ref[0, rows, hs]
        k = proj_ref[1, rows, hs]
        vb = proj_ref[2, rows, hs].astype(BF16)
        gate = proj_ref[3, rows, hs]
        qr = q * cosf + pltpu.roll(q, HEAD_DIM // 2, 1) * sinf
        kr = k * cosf + pltpu.roll(k, HEAD_DIM // 2, 1) * sinf
        scores = lax.dot_general(qr.astype(BF16), kr.astype(BF16), _NT,
                                 preferred_element_type=F32) * dintra_ref[hd]
        st = state_ref[hd]
        o = (jnp.dot(scores.astype(BF16), vb, preferred_element_type=F32)
             + jnp.dot((qr * qdec_ref[hd]).astype(BF16), st.astype(BF16), preferred_element_type=F32))
        kv = lax.dot_general((kr * kdec_ref[hd]).astype(BF16), vb, _TN, preferred_element_type=F32)
        state_ref[hd] = st * _CHUNK_DECAY[hd] + kv
        dv = o - jnp.mean(o, axis=-1, keepdims=True)
        on = dv * lax.rsqrt(jnp.mean(dv * dv, axis=-1, keepdims=True) + EPS) * gn_ref[:, hs]
        mix_ref[rows, hs] = (gate * _sigmoid(gate) * on).astype(BF16)

    def out_projection(col0, width):
        return jnp.dot(mix_ref[:, col0:col0 + width], wout_ref[col0:col0 + width, :], preferred_element_type=F32)

    project(4, 0, SGU_WIDTH)
    project(5, 0, SGU_WIDTH)
    scatter_group(0)
    for sec in range(4):
        project(sec, 0, pair)
    scatter_group(1)
    for c in range(n_chunks):
        spatial_gating(c)
    scatter_group(2)
    attn_ref[...] = out_projection(RET_WIDTH, SGU_WIDTH)
    for sec in range(4):
        project(sec, pair, pair)
    scatter_group(3)
    for c in range(n_chunks):
        rotary_tables(c)
    scatter_group(4)
    for c in range(n_chunks):
        retention(c, 0)
        retention(c, 1)
    scatter_group(5)
    attn_ref[...] += out_projection(0, pair)
    scatter_group(6)
    for c in range(n_chunks):
        retention(c, 2)
        retention(c, 3)
    scatter_group(7)
    x1 = x_ref[...] + (attn_ref[...] + out_projection(pair, pair))
    x1_ref[...] = x1
    h2 = _rms(x1, gffn_ref[...])

    h2h = h2.astype(BF16)
    h2l = (h2 - h2h.astype(F32)).astype(BF16)
    wrh = wrh_ref[...]
    logits = (lax.dot_general(wrh, h2h, _NT, preferred_element_type=F32)
              + lax.dot_general(wrh, h2l, _NT, preferred_element_type=F32)
              + lax.dot_general(wrl_ref[...], h2h, _NT, preferred_element_type=F32)
              + br_ref[...])
    el = logits[0:N_EXPERTS]
    gl = logits[N_EXPERTS:N_EXPERTS + N_GROUPS]
    rg = lax.broadcasted_iota(I32, (N_GROUPS, ts), 0).astype(F32)
    gmax = jnp.max(gl, axis=0, keepdims=True)
    gidx = jnp.min(jnp.where(gl == gmax, rg, float(N_GROUPS)), axis=0, keepdims=True)
    gweight = 1.0 / jnp.sum(jnp.exp(gl - gmax), axis=0, keepdims=True)
    re_i = lax.broadcasted_iota(I32, (N_EXPERTS, ts), 0)
    re = re_i.astype(F32)
    in_group = (re_i // N_PER_GROUP).astype(F32) == gidx
    neg = -jnp.inf
    sel = jnp.where(in_group, el, neg)
    m1 = jnp.max(sel, axis=0, keepdims=True)
    i1 = jnp.min(jnp.where(sel == m1, re, float(N_EXPERTS)), axis=0, keepdims=True)
    sel2 = jnp.where(re == i1, neg, sel)
    m2 = jnp.max(sel2, axis=0, keepdims=True)
    i2 = jnp.min(jnp.where(sel2 == m2, re, float(N_EXPERTS)), axis=0, keepdims=True)
    e2 = jnp.exp(m2 - m1)
    w1 = 1.0 / (1.0 + e2)

    hit1 = re == i1
    hit2 = re == i2
    onehot = jnp.where(hit1, 1.0, jnp.where(hit2, 1.0, 0.0))
    seen = cnt_ref[:, 0:1] + jnp.dot(onehot.astype(BF16), before_ref[...], preferred_element_type=F32)
    inv_blk = 1.0 / EXPERT_TILE
    c0 = cnt_ref[:, 0:1]
    tile_cnt = jnp.sum(onehot, axis=1, keepdims=True)
    c1 = c0 + tile_cnt
    up0 = jnp.floor((c0 + (EXPERT_TILE - 1.0)) * inv_blk)
    n_new = jnp.floor((c1 + (EXPERT_TILE - 1.0)) * inv_blk) - up0
    ea = lax.broadcasted_iota(I32, (N_EXPERTS, N_EXPERTS), 0)
    eb = lax.broadcasted_iota(I32, (N_EXPERTS, N_EXPERTS), 1)
    earlier = jnp.where(eb < ea, 1.0, 0.0).astype(BF16)
    n_new_b = jnp.broadcast_to(n_new, (N_EXPERTS, LANES))
    new1 = nalloc_ref[:, 0:1] + jnp.dot(earlier, n_new_b.astype(BF16), preferred_element_type=F32)[:, 0:1]
    new2 = new1 + 1.0
    open_blk = open_ref[:, 0:1]
    rank_blk = jnp.floor(seen * inv_blk)
    which = rank_blk - up0
    block_of = jnp.where(which < 0.0, open_blk, jnp.where(which == 0.0, new1, new2))
    row_of = block_of * float(EXPERT_TILE) + (seen - rank_blk * float(EXPERT_TILE))
    keys = jnp.concatenate([jnp.sum(jnp.where(hit1, row_of, 0.0), axis=0, keepdims=True),
                            jnp.sum(jnp.where(hit2, row_of, 0.0), axis=0, keepdims=True)], axis=0).astype(I32)
    key_ref[...] = keys
    lane = lax.broadcasted_iota(I32, (N_EXPERTS, LANES), 1).astype(F32)
    blocks = blocks_ref[...]
    blocks = jnp.where(lane == up0, jnp.where(n_new >= 1.0, new1, blocks), blocks)
    blocks = jnp.where(lane == up0 + 1.0, jnp.where(n_new == 2.0, new2, blocks), blocks)
    blocks_ref[...] = blocks
    open_ref[...] = jnp.broadcast_to(jnp.where(n_new == 0.0, open_blk, jnp.where(n_new == 1.0, new1, new2)),
                                     open_ref.shape)
    nalloc_ref[...] += jnp.sum(n_new_b, axis=0, keepdims=True)
    cnt_ref[...] += jnp.broadcast_to(tile_cnt, cnt_ref.shape)
    gates_t = jnp.concatenate([gweight * w1, gweight * (e2 * w1), jnp.zeros((LANES - 2, ts), F32)], axis=0)
    gcol_ref[...] = gates_t.T

    @pl.when(li > 0)
    def _():
        wait_scatter(slot_cur)
    _store_token_major(stage_ref.at[slot_cur], h2)
    keyv_ref[slot_cur] = keys
    keys_copy(slot_cur).start()

    @pl.when(li == n_lin - 1)
    def _():
        keys_copy(slot_cur).wait()

        def last_rows(t, carry):
            for k in (0, 1):
                pltpu.make_async_copy(stage_ref.at[slot_cur, _token_rows(t)],
                                      xs_ref.at[_token_rows(keys_ref[slot_cur, k, t])],
                                      ssem.at[slot_cur]).start(priority=k)
            return carry
        lax.fori_loop(0, ts, last_rows, 0, unroll=4)

        total = cnt_ref[...]
        used = total - jnp.floor(total * inv_blk) * float(EXPERT_TILE)
        meta_lane = lax.broadcasted_iota(I32, (N_EXPERTS, LANES), 1)
        metav_ref[...] = jnp.where(
            meta_lane == 0, open_ref[...] * float(EXPERT_TILE) + used,
            jnp.where(meta_lane == 1, jnp.where(used > 0.0, float(EXPERT_TILE) - used, 0.0),
                      nalloc_ref[...])).astype(I32)
        pltpu.sync_copy(metav_ref, meta_ref)
        zero_ref[...] = jnp.zeros_like(zero_ref)

        def piece(row, n_rows):
            return pltpu.make_async_copy(
                zero_ref.at[pl.ds(0, n_rows * TOKEN_SUBLANES)],
                xs_ref.at[pl.ds(pl.multiple_of(row * TOKEN_SUBLANES, TOKEN_SUBLANES), n_rows * TOKEN_SUBLANES)],
                fsem)

        def for_each_piece(action):
            for e in range(N_EXPERTS):
                row = meta_ref[e, 0]
                length = meta_ref[e, 1]
                for n_rows in _TAIL_PIECES:
                    @pl.when((length & n_rows) != 0)
                    def _():
                        action(piece(row, n_rows))
                    row = row + (length & n_rows)
            for blk in range(max_blocks - N_EXPERTS, max_blocks):
                @pl.when(blk >= meta_ref[0, 2])
                def _():
                    action(piece(blk * EXPERT_TILE, EXPERT_TILE))

        for_each_piece(lambda cp: cp.start())
        for_each_piece(lambda cp: cp.wait())
        wait_scatter(slot_prev)
        wait_scatter(slot_cur)


def _attn_router(x, pos4, gmix, win, gn, sgn, sw, sb, wout, invf, gffn, wrh, wrl, br):
    bsz, seq, d = x.shape
    ts = ATTN_TILE
    ns = seq // ts
    t = bsz * seq
    const2 = lambda b, s: (0, 0)
    const3 = lambda b, s: (0, 0, 0)
    single = dict(pipeline_mode=pl.Buffered(1))
    in_specs = [
        pl.BlockSpec((None, ts, d), lambda b, s: (b, s, 0)),
        pl.BlockSpec((None, ts // CHUNK, 1, CHUNK), lambda b, s: (b, s, 0, 0)),
        pl.BlockSpec((1, d), const2),
        pl.BlockSpec(win.shape, const2, **single),
        pl.BlockSpec((1, RET_WIDTH), const2),
        pl.BlockSpec((1, SGU_WIDTH), const2),
        pl.BlockSpec(sw.shape, const3),
        pl.BlockSpec(sb.shape, const3),
        pl.BlockSpec(wout.shape, const2, **single),
        pl.BlockSpec(invf.shape, const2),
        pl.BlockSpec((1, d), const2),
        pl.BlockSpec(wrh.shape, const2),
        pl.BlockSpec(wrl.shape, const2),
        pl.BlockSpec(br.shape, const2),
    ]
    tok = lambda b, s: (b * ns + s, 0)
    out_specs = [
        pl.BlockSpec((ts, d), tok),
        pl.BlockSpec((2, ts), lambda b, s: (0, b * ns + s)),
        pl.BlockSpec((N_EXPERTS, LANES), const2),
        pl.BlockSpec((N_EXPERTS, LANES), const2),
        pl.BlockSpec((ts, LANES), tok),
        pl.BlockSpec(memory_space=pl.ANY),
    ]
    out_shape = [
        jax.ShapeDtypeStruct((t, d), F32),
        jax.ShapeDtypeStruct((2, t), I32),
        jax.ShapeDtypeStruct((N_EXPERTS, LANES), F32),
        jax.ShapeDtypeStruct((N_EXPERTS, LANES), F32),
        jax.ShapeDtypeStruct((t, LANES), F32),
        jax.ShapeDtypeStruct((_total_blocks(t) * EXPERT_TILE * TOKEN_SUBLANES, LANES), TOKEN_DTYPE),
    ]
    scratch = [
        pltpu.VMEM((N_SECTIONS, ts, RET_WIDTH), F32),
        pltpu.VMEM((ts, RET_WIDTH + SGU_WIDTH), BF16),
        pltpu.VMEM((ts, d), F32),
        pltpu.VMEM((2, ts // CHUNK, CHUNK, HEAD_DIM), F32),
        pltpu.VMEM((RET_HEADS, HEAD_DIM, HEAD_DIM), F32),
        pltpu.VMEM((RET_HEADS, CHUNK, CHUNK), F32),
        pltpu.VMEM((RET_HEADS, CHUNK, HEAD_DIM), F32),
        pltpu.VMEM((RET_HEADS, CHUNK, HEAD_DIM), F32),
        pltpu.VMEM((SGU_GROUPS, CHUNK, CHUNK), BF16),
        pltpu.VMEM((SGU_GROUPS, CHUNK, GROUP_DIM), F32),
        pltpu.VMEM((ts, ts), BF16),
        pltpu.VMEM((N_EXPERTS, LANES), F32),
        pltpu.VMEM((N_EXPERTS, LANES), F32),
        pltpu.VMEM((2, ts * TOKEN_SUBLANES, LANES), TOKEN_DTYPE),
        pltpu.VMEM((2, 2, ts), I32),
        pltpu.SMEM((2, 2, ts), I32),
        pltpu.VMEM((EXPERT_TILE * TOKEN_SUBLANES, LANES), TOKEN_DTYPE),
        pltpu.VMEM((N_EXPERTS, LANES), I32),
        pltpu.SMEM((N_EXPERTS, LANES), I32),
        pltpu.SemaphoreType.DMA((2,)),
        pltpu.SemaphoreType.DMA((2,)),
        pltpu.SemaphoreType.DMA,
    ]
    return pl.pallas_call(
        functools.partial(_attn_router_kernel, placeholder_row0=_max_blocks(t) * EXPERT_TILE,
                          max_blocks=_max_blocks(t)),
        grid=(bsz, ns),
        in_specs=in_specs,
        out_specs=out_specs,
        out_shape=out_shape,
        scratch_shapes=scratch,
        compiler_params=pltpu.CompilerParams(
            dimension_semantics=("arbitrary", "arbitrary"), vmem_limit_bytes=VMEM_LIMIT_BYTES),
        name="attn_router",
    )(x, pos4, gmix, win, gn, sgn, sw, sb, wout, invf, gffn, wrh, wrl, br)


def _expert_kernel(texp_ref, tin_ref, tout_ref, nt_ref, xs_ref, wg_ref, wu_ref, wd_ref, ys_ref,
                   xbuf_ref, wgb_ref, wub_ref, wdb_ref, sems):
    i = pl.program_id(0)
    n_steps = pl.num_programs(0)
    rows = xbuf_ref.shape[1]
    tm = rows // TOKEN_SUBLANES
    valid = i < nt_ref[0]
    ahead = INPUT_SLOTS - 1

    def fetch(step):
        src = xs_ref.at[pl.ds(pl.multiple_of(tin_ref[step] * rows, rows), rows)]
        slot = lax.rem(step, INPUT_SLOTS)
        return pltpu.make_async_copy(src, xbuf_ref.at[slot], sems.at[slot])

    @pl.when(i == 0)
    def _():
        for step in range(ahead):
            fetch(step).start()

    @pl.when(i + ahead < n_steps)
    def _():
        fetch(i + ahead).start()

    @pl.when(valid & ((i == 0) | (texp_ref[i] != texp_ref[jnp.maximum(i - 1, 0)])))
    def _():
        wgb_ref[...] = wg_ref[...].astype(BF16)
        wub_ref[...] = wu_ref[...].astype(BF16)
        wdb_ref[...] = wd_ref[...].astype(BF16)

    fetch(i).wait()

    @pl.when(valid)
    def _():
        xt = _load_token_major(xbuf_ref, (lax.rem(i, INPUT_SLOTS),), 0, tm).astype(BF16)
        a = jnp.dot(xt, wgb_ref[...], preferred_element_type=F32)
        bb = jnp.dot(xt, wub_ref[...], preferred_element_type=F32)
        act = a * _sigmoid(a) * bb
        o = jnp.dot(act.astype(BF16), wdb_ref[...], preferred_element_type=F32)
        _store_token_major(ys_ref, o)

    @pl.when(jnp.logical_not(valid))
    def _():
        ys_ref[...] = jnp.zeros_like(ys_ref)


def _experts(texp, tin, tout, ntiles, xs, wg, wu, wd):
    d = D_MODEL
    f = wg.shape[-1]
    rows = EXPERT_TILE * TOKEN_SUBLANES
    n_steps = texp.shape[0]
    assert n_steps >= INPUT_SLOTS
    grid_spec = pltpu.PrefetchScalarGridSpec(
        num_scalar_prefetch=4,
        grid=(n_steps,),
        in_specs=[
            pl.BlockSpec(memory_space=pl.ANY),
            pl.BlockSpec((None, d, f), lambda i, te, ti, to, n: (te[i], 0, 0)),
            pl.BlockSpec((None, d, f), lambda i, te, ti, to, n: (te[i], 0, 0)),
            pl.BlockSpec((None, f, d), lambda i, te, ti, to, n: (te[i], 0, 0)),
        ],
        out_specs=pl.BlockSpec((rows, LANES), lambda i, te, ti, to, n: (to[i], 0)),
        scratch_shapes=[pltpu.VMEM((INPUT_SLOTS, rows, LANES), TOKEN_DTYPE),
                        pltpu.VMEM((d, f), BF16), pltpu.VMEM((d, f), BF16), pltpu.VMEM((f, d), BF16),
                        pltpu.SemaphoreType.DMA((INPUT_SLOTS,))],
    )
    return pl.pallas_call(
        _expert_kernel,
        grid_spec=grid_spec,
        out_shape=jax.ShapeDtypeStruct(xs.shape, xs.dtype),
        compiler_params=pltpu.CompilerParams(
            dimension_semantics=("arbitrary",), vmem_limit_bytes=VMEM_LIMIT_BYTES),
        name="experts",
    )(texp, tin, tout, ntiles, xs, wg, wu, wd)


def _block_plan(cnt, blocks, n_steps):
    n_e = (cnt + EXPERT_TILE - 1) // EXPERT_TILE
    ends = jnp.cumsum(n_e)
    total = ends[-1]
    step = jnp.arange(n_steps, dtype=I32)
    p = jnp.minimum(step, total - 1)
    e = jnp.sum((p[:, None] >= ends[None, :]).astype(I32), axis=1)
    is_e = e[:, None] == jnp.arange(N_EXPERTS, dtype=I32)[None, :]
    k = p - jnp.sum(jnp.where(is_e, (ends - n_e)[None, :], 0), axis=1)
    rows = jnp.sum(jnp.where(is_e[:, :, None], blocks[None, :, :], 0), axis=1)
    blk_in = jnp.sum(jnp.where(k[:, None] == jnp.arange(LANES, dtype=I32)[None, :], rows, 0), axis=1)
    blk_out = jnp.where(step < total, blk_in, step)
    return e.astype(I32), blk_in.astype(I32), blk_out.astype(I32), total.reshape(1).astype(I32)


def _combine_kernel(p1_ref, p2_ref, p1n_ref, p2n_ref, gcol_ref, x1_ref, gfin_ref, os_ref, out_ref, buf_ref, sems):
    i = pl.program_id(0)
    n_steps = pl.num_programs(0)
    tm = x1_ref.shape[0]
    per_group = tm // DMA_GROUPS

    def row_copy(src_row, j, sl, k):
        return pltpu.make_async_copy(os_ref.at[_token_rows(src_row)], buf_ref.at[sl, k, _token_rows(j)], sems.at[sl])

    def start_group(r1_ref, r2_ref, sl, g):
        for j in range(g * per_group, (g + 1) * per_group):
            row_copy(r1_ref[j], j, sl, 0).start(priority=0)
            row_copy(r2_ref[j], j, sl, 1).start(priority=1)

    def wait_tile(sl):
        for k in (0, 1):
            pltpu.make_async_copy(os_ref.at[pl.ds(0, tm * TOKEN_SUBLANES)], buf_ref.at[sl, k], sems.at[sl]).wait()

    @pl.when(i == 0)
    def _():
        for g in range(DMA_GROUPS):
            start_group(p1_ref, p2_ref, 0, g)

    def step(sl):
        wait_tile(sl)
        for g in range(DMA_GROUPS):
            rows = slice(g * per_group, (g + 1) * per_group)
            o1 = _load_token_major(buf_ref, (sl, 0), g * per_group, per_group)
            o2 = _load_token_major(buf_ref, (sl, 1), g * per_group, per_group)
            gates = gcol_ref[rows, :]
            y = gates[:, 0:1] * o1 + gates[:, 1:2] * o2
            out_ref[rows, :] = _rms(x1_ref[rows, :] + y, gfin_ref[...])
            start_group(p1n_ref, p2n_ref, 1 - sl, g)

    for sl in (0, 1):
        @pl.when((i & 1) == sl)
        def _():
            step(sl)

    @pl.when(i == n_steps - 1)
    def _():
        wait_tile(1 - (i & 1))


def _combine(p1, p2, gcol, x1, gfin, osorted):
    t, d = x1.shape
    tm = MOVE_TILE
    n_steps = t // tm
    cur = lambda: pl.BlockSpec((tm,), lambda i: (i,), memory_space=pltpu.SMEM)
    nxt = lambda: pl.BlockSpec((tm,), lambda i: (jnp.minimum(i + 1, n_steps - 1),), memory_space=pltpu.SMEM)
    return pl.pallas_call(
        _combine_kernel,
        grid=(n_steps,),
        in_specs=[cur(), cur(), nxt(), nxt(),
                  pl.BlockSpec((tm, LANES), lambda i: (i, 0)),
                  pl.BlockSpec((tm, d), lambda i: (i, 0)),
                  pl.BlockSpec((1, d), lambda i: (0, 0)),
                  pl.BlockSpec(memory_space=pl.ANY)],
        out_specs=pl.BlockSpec((tm, d), lambda i: (i, 0)),
        out_shape=jax.ShapeDtypeStruct((t, d), F32),
        scratch_shapes=[pltpu.VMEM((2, 2, tm * TOKEN_SUBLANES, LANES), TOKEN_DTYPE), pltpu.SemaphoreType.DMA((2,))],
        compiler_params=pltpu.CompilerParams(dimension_semantics=("arbitrary",)),
        name="combine",
    )(p1, p2, p1, p2, gcol, x1, gfin, osorted)


def kernel(x, positions, norm_mix_gain, w_in, ret_gn_gain, sgu_norm_gain, sgu_w, sgu_b, w_out, norm_ffn_gain,
           w_router_group, b_router_group, w_router_expert, b_router_expert, w_expert_gate_up, w_expert_up,
           w_expert_down, final_norm_gain):
    bsz, seq, d = x.shape
    assert w_in.shape[0] == 1, "single-layer block"
    assert d == D_MODEL, "the packed token-major layout is written for 1024 features"
    assert seq % ATTN_TILE == 0 and (2 * bsz * seq) % EXPERT_TILE == 0
    assert bsz * seq // EXPERT_TILE + 1 <= LANES, "an expert's block list must fit one row of the block table"
    t = bsz * seq
    half = HEAD_DIM // 2

    pos4 = positions.reshape(bsz, seq // CHUNK, 1, CHUNK)
    invf = (ROPE_BASE ** (-jnp.arange(half, dtype=F32) * 2.0 / HEAD_DIM)).reshape(half, 1)
    wr = jnp.concatenate([
        jnp.transpose(w_router_expert[0], (0, 2, 1)).reshape(N_EXPERTS, d),
        w_router_group[0].T,
        jnp.zeros((ROUTER_ROWS - N_EXPERTS - N_GROUPS, d), F32)], axis=0)
    wrh = wr.astype(BF16)
    wrl = (wr - wrh.astype(F32)).astype(BF16)
    br = jnp.concatenate([b_router_expert[0].reshape(N_EXPERTS), b_router_group[0],
                          jnp.zeros((ROUTER_ROWS - N_EXPERTS - N_GROUPS,), F32)]).reshape(ROUTER_ROWS, 1)

    x1, keys, cnt, blocks, gcol, xs = _attn_router(
        x, pos4, norm_mix_gain[0].reshape(1, d), w_in[0].astype(BF16), ret_gn_gain[0].reshape(1, RET_WIDTH),
        sgu_norm_gain[0].reshape(1, SGU_WIDTH), sgu_w[0], sgu_b[0].reshape(SGU_GROUPS, CHUNK, 1),
        w_out[0].astype(BF16), invf, norm_ffn_gain[0].reshape(1, d), wrh, wrl, br)

    texp, tin, tout, nalloc = _block_plan(cnt[:, 0].astype(I32), blocks.astype(I32), _total_blocks(t))
    ff = w_expert_gate_up.shape[-1]
    ys = _experts(texp, tin, tout, nalloc, xs,
                  w_expert_gate_up[0].reshape(N_EXPERTS, d, ff),
                  w_expert_up[0].reshape(N_EXPERTS, d, ff),
                  w_expert_down[0].reshape(N_EXPERTS, ff, d))

    out = _combine(keys[0], keys[1], gcol, x1, final_norm_gain.reshape(1, d), ys)
    return out.reshape(bsz, seq, d)
```

```python
import functools
import math

import jax
import jax.numpy as jnp
from jax import lax
from jax.experimental import pallas as pl
from jax.experimental.pallas import tpu as pltpu

F32 = jnp.float32
BF16 = jnp.bfloat16
I32 = jnp.int32

RET_HEADS = 4
HEAD_DIM = 128
CHUNK = 128
SGU_GROUPS = 4
GROUP_DIM = 128
RET_WIDTH = RET_HEADS * HEAD_DIM
SGU_WIDTH = SGU_GROUPS * GROUP_DIM
N_SECTIONS = 6
N_GROUPS = 4
N_PER_GROUP = 8
N_EXPERTS = N_GROUPS * N_PER_GROUP
ROUTER_ROWS = 40
ROPE_BASE = 10000.0
EPS = 1e-6

LANES = 128
SUBLANES = 8
TOKEN_SUBLANES = SUBLANES
TOKEN_DTYPE = F32
D_MODEL = TOKEN_SUBLANES * LANES
VMEM_LIMIT_BYTES = 56 * 1024 * 1024

ATTN_TILE = 512
MOVE_TILE = 256
EXPERT_TILE = 512
DMA_GROUPS = 8
INPUT_SLOTS = 3
_TAIL_PIECES = tuple(EXPERT_TILE >> (b + 1) for b in range(EXPERT_TILE.bit_length() - 1))


def _max_blocks(n_tok):
    return (2 * n_tok) // EXPERT_TILE + N_EXPERTS


def _total_blocks(n_tok):
    return _max_blocks(n_tok) + (2 * ATTN_TILE) // EXPERT_TILE


_NT = (((1,), (1,)), ((), ()))
_TN = (((0,), (0,)), ((), ()))

_LOG_GAMMA = [math.log(1.0 - 2.0 ** (-5.0 - h)) for h in range(RET_HEADS)]
_CHUNK_DECAY = [math.exp(lg * CHUNK) for lg in _LOG_GAMMA]
_K_SCALE = HEAD_DIM ** -0.5


def _rms(x, gain):
    return x * lax.rsqrt(jnp.mean(x * x, axis=-1, keepdims=True) + EPS) * gain


def _gelu(x):
    return 0.5 * x * (1.0 + lax.erf(x * 0.7071067811865476))


def _sigmoid(x):
    return 1.0 / (1.0 + jnp.exp(-x))


def _token_rows(token):
    return pl.ds(pl.multiple_of(token * TOKEN_SUBLANES, TOKEN_SUBLANES), TOKEN_SUBLANES)


def _load_token_major(ref, lead, tok0, n_tok):
    return jnp.concatenate(
        [ref[lead + (pl.ds(tok0 * TOKEN_SUBLANES + sl, n_tok, stride=TOKEN_SUBLANES), slice(None))]
         for sl in range(TOKEN_SUBLANES)], axis=1)


def _store_token_major(ref, value):
    n_tok = value.shape[0]
    for sl in range(TOKEN_SUBLANES):
        ref[pl.ds(sl, n_tok, stride=TOKEN_SUBLANES), :] = value[:, sl * LANES:(sl + 1) * LANES]


def _attn_router_kernel(x_ref, pos_ref, gmix_ref, win_ref, gn_ref, sgn_ref, sw_ref, sb_ref, wout_ref,
                        invf_ref, gffn_ref, wrh_ref, wrl_ref, br_ref,
                        x1_ref, key_ref, cnt_ref, blocks_ref, gcol_ref, xs_ref,
                        proj_ref, mix_ref, attn_ref, rot_ref, state_ref, dintra_ref, qdec_ref, kdec_ref, wc_ref,
                        btab_ref, before_ref, open_ref, nalloc_ref, stage_ref, keyv_ref, keys_ref, zero_ref,
                        metav_ref, meta_ref, ssem, ksem, fsem, *, placeholder_row0, max_blocks):
    ts, d = x_ref.shape
    b = pl.program_id(0)
    s = pl.program_id(1)
    li = b * pl.num_programs(1) + s
    n_lin = pl.num_programs(0) * pl.num_programs(1)
    tok_per_group = ts // DMA_GROUPS

    def scatter_rows(t):
        for k in (0, 1):
            pltpu.make_async_copy(stage_ref.at[_token_rows(t)], xs_ref.at[_token_rows(keys_ref[k, t])],
                                  ssem).start(priority=k)

    def scatter_group(g):
        for t in range(g * tok_per_group, (g + 1) * tok_per_group):
            scatter_rows(t)

    def wait_scatter():
        for _ in (0, 1):
            pltpu.make_async_copy(stage_ref, xs_ref.at[pl.ds(0, ts * TOKEN_SUBLANES)], ssem).wait()

    def keys_copy():
        return pltpu.make_async_copy(keyv_ref, keys_ref, ksem)

    @pl.when((b == 0) & (s == 0))
    def _init_tables():
        i = lax.broadcasted_iota(I32, (CHUNK, CHUNK), 0)
        j = lax.broadcasted_iota(I32, (CHUNK, CHUNK), 1)
        diff = (i - j).astype(F32)
        fi = i.astype(F32)
        for h in range(RET_HEADS):
            lg = _LOG_GAMMA[h]
            dintra_ref[h] = jnp.where(i >= j, jnp.exp(lg * diff), 0.0) * _K_SCALE
            qdec_ref[h] = jnp.exp(lg * (fi + 1.0))
            kdec_ref[h] = jnp.exp(lg * (CHUNK - 1.0 - fi)) * _K_SCALE
        for g in range(SGU_GROUPS):
            wc_ref[g] = jnp.where(i >= j, sw_ref[g], 0.0).astype(BF16)
            btab_ref[g] = jnp.broadcast_to(sb_ref[g], (CHUNK, GROUP_DIM))
        ta = lax.broadcasted_iota(I32, (ts, ts), 0)
        tb = lax.broadcasted_iota(I32, (ts, ts), 1)
        before_ref[...] = jnp.where(ta < tb, 1.0, 0.0).astype(BF16)
        cnt_ref[...] = jnp.zeros_like(cnt_ref)
        blocks_ref[...] = jnp.zeros_like(blocks_ref)
        open_ref[...] = jnp.zeros_like(open_ref)
        nalloc_ref[...] = jnp.zeros_like(nalloc_ref)
        stage_ref[...] = jnp.zeros(stage_ref.shape, TOKEN_DTYPE)
        kk = lax.broadcasted_iota(I32, (2, ts), 0)
        tt = lax.broadcasted_iota(I32, (2, ts), 1)
        keyv_ref[...] = placeholder_row0 + 2 * tt + kk
        pltpu.sync_copy(keyv_ref, keys_ref)

    @pl.when(li > 0)
    def _():
        keys_copy().wait()

    @pl.when(s == 0)
    def _reset_state():
        state_ref[...] = jnp.zeros_like(state_ref)

    n_chunks = ts // CHUNK
    pair = 2 * HEAD_DIM
    x = x_ref[...]
    h = _rms(x, gmix_ref[...]).astype(BF16)

    def project(sec, col0, width):
        cols = slice(col0, col0 + width)
        proj_ref[sec, :, cols] = jnp.dot(h, win_ref[:, sec * RET_WIDTH + col0:sec * RET_WIDTH + col0 + width],
                                         preferred_element_type=F32)

    def spatial_gating(c):
        rows = slice(c * CHUNK, (c + 1) * CHUNK)
        for g in range(SGU_GROUPS):
            gs = slice(g * GROUP_DIM, (g + 1) * GROUP_DIM)
            u = _gelu(proj_ref[4, rows, gs])
            v = _gelu(proj_ref[5, rows, gs])
            vn = v * lax.rsqrt(jnp.mean(v * v, axis=-1, keepdims=True) + EPS) * sgn_ref[:, gs]
            sg = jnp.dot(wc_ref[g], vn.astype(BF16), preferred_element_type=F32) + btab_ref[g]
            mix_ref[rows, RET_WIDTH + g * GROUP_DIM:RET_WIDTH + (g + 1) * GROUP_DIM] = (u * sg).astype(BF16)

    def rotary_tables(c):
        ang_t = invf_ref[...] * pos_ref[c].astype(F32)
        cos_t = jnp.cos(ang_t)
        sin_t = jnp.sin(ang_t)
        rot_ref[0, c] = jnp.concatenate([cos_t, cos_t], axis=0).T
        rot_ref[1, c] = jnp.concatenate([-sin_t, sin_t], axis=0).T

    def retention(c, hd):
        rows = slice(c * CHUNK, (c + 1) * CHUNK)
        hs = slice(hd * HEAD_DIM, (hd + 1) * HEAD_DIM)
        cosf = rot_ref[0, c]
        sinf = rot_ref[1, c]
        q = proj_ref[0, rows, hs]
        k = proj_ref[1, rows, hs]
        vb = proj_ref[2, rows, hs].astype(BF16)
        gate = proj_ref[3, rows, hs]
        qr = q * cosf + pltpu.roll(q, HEAD_DIM // 2, 1) * sinf
        kr = k * cosf + pltpu.roll(k, HEAD_DIM // 2, 1) * sinf
        scores = lax.dot_general(qr.astype(BF16), kr.astype(BF16), _NT,
                                 preferred_element_type=F32) * dintra_ref[hd]
        st = state_ref[hd]
        o = (jnp.dot(scores.astype(BF16), vb, preferred_element_type=F32)
             + jnp.dot((qr * qdec_ref[hd]).astype(BF16), st.astype(BF16), preferred_element_type=F32))
        kv = lax.dot_general((kr * kdec_ref[hd]).astype(BF16), vb, _TN, preferred_element_type=F32)
        state_ref[hd] = st * _CHUNK_DECAY[hd] + kv
        dv = o - jnp.mean(o, axis=-1, keepdims=True)
        on = dv * lax.rsqrt(jnp.mean(dv * dv, axis=-1, keepdims=True) + EPS) * gn_ref[:, hs]
        mix_ref[rows, hs] = (gate * _sigmoid(gate) * on).astype(BF16)

    def out_projection(col0, width):
        return jnp.dot(mix_ref[:, col0:col0 + width], wout_ref[col0:col0 + width, :], preferred_element_type=F32)

    project(4, 0, SGU_WIDTH)
    project(5, 0, SGU_WIDTH)
    scatter_group(0)
    scatter_group(1)
    for sec in range(4):
        project(sec, 0, pair)
    scatter_group(2)
    scatter_group(3)
    for c in range(n_chunks):
        spatial_gating(c)
    scatter_group(4)
    attn_ref[...] = out_projection(RET_WIDTH, SGU_WIDTH)
    for sec in range(4):
        project(sec, pair, pair)
    scatter_group(5)
    for c in range(n_chunks):
        rotary_tables(c)
    scatter_group(6)
    for c in range(n_chunks):
        retention(c, 0)
        retention(c, 1)
    scatter_group(7)
    attn_ref[...] += out_projection(0, pair)
    for c in range(n_chunks):
        retention(c, 2)
        retention(c, 3)
    x1 = x_ref[...] + (attn_ref[...] + out_projection(pair, pair))
    x1_ref[...] = x1
    h2 = _rms(x1, gffn_ref[...])

    h2h = h2.astype(BF16)
    h2l = (h2 - h2h.astype(F32)).astype(BF16)
    wrh = wrh_ref[...]
    logits = (lax.dot_general(wrh, h2h, _NT, preferred_element_type=F32)
              + lax.dot_general(wrh, h2l, _NT, preferred_element_type=F32)
              + lax.dot_general(wrl_ref[...], h2h, _NT, preferred_element_type=F32)
              + br_ref[...])
    el = logits[0:N_EXPERTS]
    gl = logits[N_EXPERTS:N_EXPERTS + N_GROUPS]
    rg = lax.broadcasted_iota(I32, (N_GROUPS, ts), 0).astype(F32)
    gmax = jnp.max(gl, axis=0, keepdims=True)
    gidx = jnp.min(jnp.where(gl == gmax, rg, float(N_GROUPS)), axis=0, keepdims=True)
    gweight = 1.0 / jnp.sum(jnp.exp(gl - gmax), axis=0, keepdims=True)
    re_i = lax.broadcasted_iota(I32, (N_EXPERTS, ts), 0)
    re = re_i.astype(F32)
    in_group = (re_i // N_PER_GROUP).astype(F32) == gidx
    neg = -jnp.inf
    sel = jnp.where(in_group, el, neg)
    m1 = jnp.max(sel, axis=0, keepdims=True)
    i1 = jnp.min(jnp.where(sel == m1, re, float(N_EXPERTS)), axis=0, keepdims=True)
    sel2 = jnp.where(re == i1, neg, sel)
    m2 = jnp.max(sel2, axis=0, keepdims=True)
    i2 = jnp.min(jnp.where(sel2 == m2, re, float(N_EXPERTS)), axis=0, keepdims=True)
    e2 = jnp.exp(m2 - m1)
    w1 = 1.0 / (1.0 + e2)

    hit1 = re == i1
    hit2 = re == i2
    onehot = jnp.where(hit1, 1.0, jnp.where(hit2, 1.0, 0.0))
    seen = cnt_ref[:, 0:1] + jnp.dot(onehot.astype(BF16), before_ref[...], preferred_element_type=F32)
    inv_blk = 1.0 / EXPERT_TILE
    c0 = cnt_ref[:, 0:1]
    tile_cnt = jnp.sum(onehot, axis=1, keepdims=True)
    c1 = c0 + tile_cnt
    up0 = jnp.floor((c0 + (EXPERT_TILE - 1.0)) * inv_blk)
    n_new = jnp.floor((c1 + (EXPERT_TILE - 1.0)) * inv_blk) - up0
    ea = lax.broadcasted_iota(I32, (N_EXPERTS, N_EXPERTS), 0)
    eb = lax.broadcasted_iota(I32, (N_EXPERTS, N_EXPERTS), 1)
    earlier = jnp.where(eb < ea, 1.0, 0.0).astype(BF16)
    n_new_b = jnp.broadcast_to(n_new, (N_EXPERTS, LANES))
    new1 = nalloc_ref[:, 0:1] + jnp.dot(earlier, n_new_b.astype(BF16), preferred_element_type=F32)[:, 0:1]
    new2 = new1 + 1.0
    open_blk = open_ref[:, 0:1]
    rank_blk = jnp.floor(seen * inv_blk)
    which = rank_blk - up0
    block_of = jnp.where(which < 0.0, open_blk, jnp.where(which == 0.0, new1, new2))
    row_of = block_of * float(EXPERT_TILE) + (seen - rank_blk * float(EXPERT_TILE))
    keys = jnp.concatenate([jnp.sum(jnp.where(hit1, row_of, 0.0), axis=0, keepdims=True),
                            jnp.sum(jnp.where(hit2, row_of, 0.0), axis=0, keepdims=True)], axis=0).astype(I32)
    key_ref[...] = keys
    lane = lax.broadcasted_iota(I32, (N_EXPERTS, LANES), 1).astype(F32)
    blocks = blocks_ref[...]
    blocks = jnp.where(lane == up0, jnp.where(n_new >= 1.0, new1, blocks), blocks)
    blocks = jnp.where(lane == up0 + 1.0, jnp.where(n_new == 2.0, new2, blocks), blocks)
    blocks_ref[...] = blocks
    open_ref[...] = jnp.broadcast_to(jnp.where(n_new == 0.0, open_blk, jnp.where(n_new == 1.0, new1, new2)),
                                     open_ref.shape)
    nalloc_ref[...] += jnp.sum(n_new_b, axis=0, keepdims=True)
    cnt_ref[...] += jnp.broadcast_to(tile_cnt, cnt_ref.shape)
    gates_t = jnp.concatenate([gweight * w1, gweight * (e2 * w1), jnp.zeros((LANES - 2, ts), F32)], axis=0)
    gcol_ref[...] = gates_t.T

    wait_scatter()
    _store_token_major(stage_ref, h2)
    keyv_ref[...] = keys
    keys_copy().start()

    @pl.when(li == n_lin - 1)
    def _():
        keys_copy().wait()

        def last_rows(t, carry):
            scatter_rows(t)
            return carry
        lax.fori_loop(0, ts, last_rows, 0, unroll=4)

        total = cnt_ref[...]
        used = total - jnp.floor(total * inv_blk) * float(EXPERT_TILE)
        meta_lane = lax.broadcasted_iota(I32, (N_EXPERTS, LANES), 1)
        metav_ref[...] = jnp.where(
            meta_lane == 0, open_ref[...] * float(EXPERT_TILE) + used,
            jnp.where(meta_lane == 1, jnp.where(used > 0.0, float(EXPERT_TILE) - used, 0.0),
                      nalloc_ref[...])).astype(I32)
        pltpu.sync_copy(metav_ref, meta_ref)
        zero_ref[...] = jnp.zeros_like(zero_ref)

        def piece(row, n_rows):
            return pltpu.make_async_copy(
                zero_ref.at[pl.ds(0, n_rows * TOKEN_SUBLANES)],
                xs_ref.at[pl.ds(pl.multiple_of(row * TOKEN_SUBLANES, TOKEN_SUBLANES), n_rows * TOKEN_SUBLANES)],
                fsem)

        def for_each_piece(action):
            for e in range(N_EXPERTS):
                row = meta_ref[e, 0]
                length = meta_ref[e, 1]
                for n_rows in _TAIL_PIECES:
                    @pl.when((length & n_rows) != 0)
                    def _():
                        action(piece(row, n_rows))
                    row = row + (length & n_rows)
            for blk in range(max_blocks - N_EXPERTS, max_blocks):
                @pl.when(blk >= meta_ref[0, 2])
                def _():
                    action(piece(blk * EXPERT_TILE, EXPERT_TILE))

        for_each_piece(lambda cp: cp.start())
        for_each_piece(lambda cp: cp.wait())
        wait_scatter()


def _attn_router(x, pos4, gmix, win, gn, sgn, sw, sb, wout, invf, gffn, wrh, wrl, br):
    bsz, seq, d = x.shape
    ts = ATTN_TILE
    ns = seq // ts
    t = bsz * seq
    const2 = lambda b, s: (0, 0)
    const3 = lambda b, s: (0, 0, 0)
    single = dict(pipeline_mode=pl.Buffered(1))
    in_specs = [
        pl.BlockSpec((None, ts, d), lambda b, s: (b, s, 0)),
        pl.BlockSpec((None, ts // CHUNK, 1, CHUNK), lambda b, s: (b, s, 0, 0)),
        pl.BlockSpec((1, d), const2),
        pl.BlockSpec(win.shape, const2, **single),
        pl.BlockSpec((1, RET_WIDTH), const2),
        pl.BlockSpec((1, SGU_WIDTH), const2),
        pl.BlockSpec(sw.shape, const3),
        pl.BlockSpec(sb.shape, const3),
        pl.BlockSpec(wout.shape, const2, **single),
        pl.BlockSpec(invf.shape, const2),
        pl.BlockSpec((1, d), const2),
        pl.BlockSpec(wrh.shape, const2),
        pl.BlockSpec(wrl.shape, const2),
        pl.BlockSpec(br.shape, const2),
    ]
    tok = lambda b, s: (b * ns + s, 0)
    out_specs = [
        pl.BlockSpec((ts, d), tok),
        pl.BlockSpec((2, ts), lambda b, s: (0, b * ns + s)),
        pl.BlockSpec((N_EXPERTS, LANES), const2),
        pl.BlockSpec((N_EXPERTS, LANES), const2),
        pl.BlockSpec((ts, LANES), tok),
        pl.BlockSpec(memory_space=pl.ANY),
    ]
    out_shape = [
        jax.ShapeDtypeStruct((t, d), F32),
        jax.ShapeDtypeStruct((2, t), I32),
        jax.ShapeDtypeStruct((N_EXPERTS, LANES), F32),
        jax.ShapeDtypeStruct((N_EXPERTS, LANES), F32),
        jax.ShapeDtypeStruct((t, LANES), F32),
        jax.ShapeDtypeStruct((_total_blocks(t) * EXPERT_TILE * TOKEN_SUBLANES, LANES), TOKEN_DTYPE),
    ]
    scratch = [
        pltpu.VMEM((N_SECTIONS, ts, RET_WIDTH), F32),
        pltpu.VMEM((ts, RET_WIDTH + SGU_WIDTH), BF16),
        pltpu.VMEM((ts, d), F32),
        pltpu.VMEM((2, ts // CHUNK, CHUNK, HEAD_DIM), F32),
        pltpu.VMEM((RET_HEADS, HEAD_DIM, HEAD_DIM), F32),
        pltpu.VMEM((RET_HEADS, CHUNK, CHUNK), F32),
        pltpu.VMEM((RET_HEADS, CHUNK, HEAD_DIM), F32),
        pltpu.VMEM((RET_HEADS, CHUNK, HEAD_DIM), F32),
        pltpu.VMEM((SGU_GROUPS, CHUNK, CHUNK), BF16),
        pltpu.VMEM((SGU_GROUPS, CHUNK, GROUP_DIM), F32),
        pltpu.VMEM((ts, ts), BF16),
        pltpu.VMEM((N_EXPERTS, LANES), F32),
        pltpu.VMEM((N_EXPERTS, LANES), F32),
        pltpu.VMEM((ts * TOKEN_SUBLANES, LANES), TOKEN_DTYPE),
        pltpu.VMEM((2, ts), I32),
        pltpu.SMEM((2, ts), I32),
        pltpu.VMEM((EXPERT_TILE * TOKEN_SUBLANES, LANES), TOKEN_DTYPE),
        pltpu.VMEM((N_EXPERTS, LANES), I32),
        pltpu.SMEM((N_EXPERTS, LANES), I32),
        pltpu.SemaphoreType.DMA,
        pltpu.SemaphoreType.DMA,
        pltpu.SemaphoreType.DMA,
    ]
    return pl.pallas_call(
        functools.partial(_attn_router_kernel, placeholder_row0=_max_blocks(t) * EXPERT_TILE,
                          max_blocks=_max_blocks(t)),
        grid=(bsz, ns),
        in_specs=in_specs,
        out_specs=out_specs,
        out_shape=out_shape,
        scratch_shapes=scratch,
        compiler_params=pltpu.CompilerParams(
            dimension_semantics=("arbitrary", "arbitrary"), vmem_limit_bytes=VMEM_LIMIT_BYTES),
        name="attn_router",
    )(x, pos4, gmix, win, gn, sgn, sw, sb, wout, invf, gffn, wrh, wrl, br)


def _expert_kernel(texp_ref, tin_ref, tout_ref, nt_ref, xs_ref, wg_ref, wu_ref, wd_ref, ys_ref,
                   xbuf_ref, wgb_ref, wub_ref, wdb_ref, sems):
    i = pl.program_id(0)
    n_steps = pl.num_programs(0)
    rows = xbuf_ref.shape[1]
    tm = rows // TOKEN_SUBLANES
    valid = i < nt_ref[0]
    ahead = INPUT_SLOTS - 1

    def fetch(step):
        src = xs_ref.at[pl.ds(pl.multiple_of(tin_ref[step] * rows, rows), rows)]
        slot = lax.rem(step, INPUT_SLOTS)
        return pltpu.make_async_copy(src, xbuf_ref.at[slot], sems.at[slot])

    @pl.when(i == 0)
    def _():
        for step in range(ahead):
            fetch(step).start()

    @pl.when(i + ahead < n_steps)
    def _():
        fetch(i + ahead).start()

    @pl.when(valid & ((i == 0) | (texp_ref[i] != texp_ref[jnp.maximum(i - 1, 0)])))
    def _():
        wgb_ref[...] = wg_ref[...].astype(BF16)
        wub_ref[...] = wu_ref[...].astype(BF16)
        wdb_ref[...] = wd_ref[...].astype(BF16)

    fetch(i).wait()

    @pl.when(valid)
    def _():
        xt = _load_token_major(xbuf_ref, (lax.rem(i, INPUT_SLOTS),), 0, tm).astype(BF16)
        a = jnp.dot(xt, wgb_ref[...], preferred_element_type=F32)
        bb = jnp.dot(xt, wub_ref[...], preferred_element_type=F32)
        act = a * _sigmoid(a) * bb
        o = jnp.dot(act.astype(BF16), wdb_ref[...], preferred_element_type=F32)
        _store_token_major(ys_ref, o)

    @pl.when(jnp.logical_not(valid))
    def _():
        ys_ref[...] = jnp.zeros_like(ys_ref)


def _experts(texp, tin, tout, ntiles, xs, wg, wu, wd):
    d = D_MODEL
    f = wg.shape[-1]
    rows = EXPERT_TILE * TOKEN_SUBLANES
    n_steps = texp.shape[0]
    assert n_steps >= INPUT_SLOTS
    grid_spec = pltpu.PrefetchScalarGridSpec(
        num_scalar_prefetch=4,
        grid=(n_steps,),
        in_specs=[
            pl.BlockSpec(memory_space=pl.ANY),
            pl.BlockSpec((None, d, f), lambda i, te, ti, to, n: (te[i], 0, 0)),
            pl.BlockSpec((None, d, f), lambda i, te, ti, to, n: (te[i], 0, 0)),
            pl.BlockSpec((None, f, d), lambda i, te, ti, to, n: (te[i], 0, 0)),
        ],
        out_specs=pl.BlockSpec((rows, LANES), lambda i, te, ti, to, n: (to[i], 0)),
        scratch_shapes=[pltpu.VMEM((INPUT_SLOTS, rows, LANES), TOKEN_DTYPE),
                        pltpu.VMEM((d, f), BF16), pltpu.VMEM((d, f), BF16), pltpu.VMEM((f, d), BF16),
                        pltpu.SemaphoreType.DMA((INPUT_SLOTS,))],
    )
    return pl.pallas_call(
        _expert_kernel,
        grid_spec=grid_spec,
        out_shape=jax.ShapeDtypeStruct(xs.shape, xs.dtype),
        compiler_params=pltpu.CompilerParams(
            dimension_semantics=("arbitrary",), vmem_limit_bytes=VMEM_LIMIT_BYTES),
        name="experts",
    )(texp, tin, tout, ntiles, xs, wg, wu, wd)


def _block_plan(cnt, blocks, n_steps):
    n_e = (cnt + EXPERT_TILE - 1) // EXPERT_TILE
    ends = jnp.cumsum(n_e)
    total = ends[-1]
    step = jnp.arange(n_steps, dtype=I32)
    p = jnp.minimum(step, total - 1)
    e = jnp.sum((p[:, None] >= ends[None, :]).astype(I32), axis=1)
    is_e = e[:, None] == jnp.arange(N_EXPERTS, dtype=I32)[None, :]
    k = p - jnp.sum(jnp.where(is_e, (ends - n_e)[None, :], 0), axis=1)
    rows = jnp.sum(jnp.where(is_e[:, :, None], blocks[None, :, :], 0), axis=1)
    blk_in = jnp.sum(jnp.where(k[:, None] == jnp.arange(LANES, dtype=I32)[None, :], rows, 0), axis=1)
    blk_out = jnp.where(step < total, blk_in, step)
    return e.astype(I32), blk_in.astype(I32), blk_out.astype(I32), total.reshape(1).astype(I32)


def _combine_kernel(p1_ref, p2_ref, p1n_ref, p2n_ref, gcol_ref, x1_ref, gfin_ref, os_ref, out_ref, buf_ref, sems):
    i = pl.program_id(0)
    n_steps = pl.num_programs(0)
    tm = x1_ref.shape[0]
    per_group = tm // DMA_GROUPS

    def row_copy(src_row, j, sl, k):
        return pltpu.make_async_copy(os_ref.at[_token_rows(src_row)], buf_ref.at[sl, k, _token_rows(j)], sems.at[sl])

    def start_group(r1_ref, r2_ref, sl, g):
        for j in range(g * per_group, (g + 1) * per_group):
            row_copy(r1_ref[j], j, sl, 0).start(priority=0)
            row_copy(r2_ref[j], j, sl, 1).start(priority=1)

    def wait_tile(sl):
        for k in (0, 1):
            pltpu.make_async_copy(os_ref.at[pl.ds(0, tm * TOKEN_SUBLANES)], buf_ref.at[sl, k], sems.at[sl]).wait()

    @pl.when(i == 0)
    def _():
        for g in range(DMA_GROUPS):
            start_group(p1_ref, p2_ref, 0, g)

    def step(sl):
        wait_tile(sl)
        for g in range(DMA_GROUPS):
            rows = slice(g * per_group, (g + 1) * per_group)
            o1 = _load_token_major(buf_ref, (sl, 0), g * per_group, per_group)
            o2 = _load_token_major(buf_ref, (sl, 1), g * per_group, per_group)
            gates = gcol_ref[rows, :]
            y = gates[:, 0:1] * o1 + gates[:, 1:2] * o2
            out_ref[rows, :] = _rms(x1_ref[rows, :] + y, gfin_ref[...])
            start_group(p1n_ref, p2n_ref, 1 - sl, g)

    for sl in (0, 1):
        @pl.when((i & 1) == sl)
        def _():
            step(sl)

    @pl.when(i == n_steps - 1)
    def _():
        wait_tile(1 - (i & 1))


def _combine(p1, p2, gcol, x1, gfin, osorted):
    t, d = x1.shape
    tm = MOVE_TILE
    n_steps = t // tm
    cur = lambda: pl.BlockSpec((tm,), lambda i: (i,), memory_space=pltpu.SMEM)
    nxt = lambda: pl.BlockSpec((tm,), lambda i: (jnp.minimum(i + 1, n_steps - 1),), memory_space=pltpu.SMEM)
    return pl.pallas_call(
        _combine_kernel,
        grid=(n_steps,),
        in_specs=[cur(), cur(), nxt(), nxt(),
                  pl.BlockSpec((tm, LANES), lambda i: (i, 0)),
                  pl.BlockSpec((tm, d), lambda i: (i, 0)),
                  pl.BlockSpec((1, d), lambda i: (0, 0)),
                  pl.BlockSpec(memory_space=pl.ANY)],
        out_specs=pl.BlockSpec((tm, d), lambda i: (i, 0)),
        out_shape=jax.ShapeDtypeStruct((t, d), F32),
        scratch_shapes=[pltpu.VMEM((2, 2, tm * TOKEN_SUBLANES, LANES), TOKEN_DTYPE), pltpu.SemaphoreType.DMA((2,))],
        compiler_params=pltpu.CompilerParams(dimension_semantics=("arbitrary",)),
        name="combine",
    )(p1, p2, p1, p2, gcol, x1, gfin, osorted)


def kernel(x, positions, norm_mix_gain, w_in, ret_gn_gain, sgu_norm_gain, sgu_w, sgu_b, w_out, norm_ffn_gain,
           w_router_group, b_router_group, w_router_expert, b_router_expert, w_expert_gate_up, w_expert_up,
           w_expert_down, final_norm_gain):
    bsz, seq, d = x.shape
    assert w_in.shape[0] == 1, "single-layer block"
    assert d == D_MODEL, "the packed token-major layout is written for 1024 features"
    assert seq % ATTN_TILE == 0 and (2 * bsz * seq) % EXPERT_TILE == 0
    assert bsz * seq // EXPERT_TILE + 1 <= LANES, "an expert's block list must fit one row of the block table"
    t = bsz * seq
    half = HEAD_DIM // 2

    pos4 = positions.reshape(bsz, seq // CHUNK, 1, CHUNK)
    invf = (ROPE_BASE ** (-jnp.arange(half, dtype=F32) * 2.0 / HEAD_DIM)).reshape(half, 1)
    wr = jnp.concatenate([
        jnp.transpose(w_router_expert[0], (0, 2, 1)).reshape(N_EXPERTS, d),
        w_router_group[0].T,
        jnp.zeros((ROUTER_ROWS - N_EXPERTS - N_GROUPS, d), F32)], axis=0)
    wrh = wr.astype(BF16)
    wrl = (wr - wrh.astype(F32)).astype(BF16)
    br = jnp.concatenate([b_router_expert[0].reshape(N_EXPERTS), b_router_group[0],
                          jnp.zeros((ROUTER_ROWS - N_EXPERTS - N_GROUPS,), F32)]).reshape(ROUTER_ROWS, 1)

    x1, keys, cnt, blocks, gcol, xs = _attn_router(
        x, pos4, norm_mix_gain[0].reshape(1, d), w_in[0].astype(BF16), ret_gn_gain[0].reshape(1, RET_WIDTH),
        sgu_norm_gain[0].reshape(1, SGU_WIDTH), sgu_w[0], sgu_b[0].reshape(SGU_GROUPS, CHUNK, 1),
        w_out[0].astype(BF16), invf, norm_ffn_gain[0].reshape(1, d), wrh, wrl, br)

    texp, tin, tout, nalloc = _block_plan(cnt[:, 0].astype(I32), blocks.astype(I32), _total_blocks(t))
    ff = w_expert_gate_up.shape[-1]
    ys = _experts(texp, tin, tout, nalloc, xs,
                  w_expert_gate_up[0].reshape(N_EXPERTS, d, ff),
                  w_expert_up[0].reshape(N_EXPERTS, d, ff),
                  w_expert_down[0].reshape(N_EXPERTS, ff, d))

    out = _combine(keys[0], keys[1], gcol, x1, final_norm_gain.reshape(1, d), ys)
    return out.reshape(bsz, seq, d)
```

```python
import functools
import math

import jax
import jax.numpy as jnp
from jax import lax
from jax.experimental import pallas as pl
from jax.experimental.pallas import tpu as pltpu

F32 = jnp.float32
BF16 = jnp.bfloat16
I32 = jnp.int32

RET_HEADS = 4
HEAD_DIM = 128
CHUNK = 128
SGU_GROUPS = 4
GROUP_DIM = 128
RET_WIDTH = RET_HEADS * HEAD_DIM
SGU_WIDTH = SGU_GROUPS * GROUP_DIM
N_SECTIONS = 6
N_GROUPS = 4
N_PER_GROUP = 8
N_EXPERTS = N_GROUPS * N_PER_GROUP
ROUTER_ROWS = 40
ROPE_BASE = 10000.0
EPS = 1e-6

LANES = 128
SUBLANES = 8
TOKEN_SUBLANES = SUBLANES
TOKEN_DTYPE = F32
D_MODEL = TOKEN_SUBLANES * LANES
VMEM_LIMIT_BYTES = 56 * 1024 * 1024

ATTN_TILE = 1024
MOVE_TILE = 256
EXPERT_TILE = 512
DMA_GROUPS = 8
INPUT_SLOTS = 3
_TAIL_PIECES = tuple(EXPERT_TILE >> (b + 1) for b in range(EXPERT_TILE.bit_length() - 1))


def _max_blocks(n_tok):
    return (2 * n_tok) // EXPERT_TILE + N_EXPERTS


def _total_blocks(n_tok):
    return _max_blocks(n_tok) + (2 * ATTN_TILE) // EXPERT_TILE


_NT = (((1,), (1,)), ((), ()))
_TN = (((0,), (0,)), ((), ()))

_LOG_GAMMA = [math.log(1.0 - 2.0 ** (-5.0 - h)) for h in range(RET_HEADS)]
_CHUNK_DECAY = [math.exp(lg * CHUNK) for lg in _LOG_GAMMA]
_K_SCALE = HEAD_DIM ** -0.5


def _rms(x, gain):
    return x * lax.rsqrt(jnp.mean(x * x, axis=-1, keepdims=True) + EPS) * gain


def _gelu(x):
    return 0.5 * x * (1.0 + lax.erf(x * 0.7071067811865476))


def _sigmoid(x):
    return 1.0 / (1.0 + jnp.exp(-x))


def _token_rows(token):
    return pl.ds(pl.multiple_of(token * TOKEN_SUBLANES, TOKEN_SUBLANES), TOKEN_SUBLANES)


def _load_token_major(ref, lead, tok0, n_tok):
    return jnp.concatenate(
        [ref[lead + (pl.ds(tok0 * TOKEN_SUBLANES + sl, n_tok, stride=TOKEN_SUBLANES), slice(None))]
         for sl in range(TOKEN_SUBLANES)], axis=1)


def _store_token_major(ref, value):
    n_tok = value.shape[0]
    for sl in range(TOKEN_SUBLANES):
        ref[pl.ds(sl, n_tok, stride=TOKEN_SUBLANES), :] = value[:, sl * LANES:(sl + 1) * LANES]


def _attn_router_kernel(x_ref, pos_ref, gmix_ref, win_ref, gn_ref, sgn_ref, sw_ref, sb_ref, wout_ref,
                        invf_ref, gffn_ref, wrh_ref, wrl_ref, br_ref,
                        x1_ref, key_ref, cnt_ref, blocks_ref, gcol_ref, xs_ref,
                        proj_ref, mix_ref, rot_ref, state_ref, dintra_ref, qdec_ref, kdec_ref, wc_ref,
                        btab_ref, before_ref, open_ref, nalloc_ref, stage_ref, keyv_ref, keys_ref,
                        metav_ref, meta_ref, ssem, ksem, fsem, *, placeholder_row0, max_blocks):
    ts, d = x_ref.shape
    b = pl.program_id(0)
    s = pl.program_id(1)
    li = b * pl.num_programs(1) + s
    n_lin = pl.num_programs(0) * pl.num_programs(1)
    slot_cur = li & 1
    slot_prev = 1 - slot_cur
    tok_per_group = ts // DMA_GROUPS

    def scatter_group(g):
        for t in range(g * tok_per_group, (g + 1) * tok_per_group):
            for k in (0, 1):
                pltpu.make_async_copy(stage_ref.at[slot_prev, _token_rows(t)],
                                      xs_ref.at[_token_rows(keys_ref[slot_prev, k, t])],
                                      ssem.at[slot_prev]).start(priority=k)

    def wait_scatter(sl):
        for _ in (0, 1):
            pltpu.make_async_copy(stage_ref.at[sl], xs_ref.at[pl.ds(0, ts * TOKEN_SUBLANES)], ssem.at[sl]).wait()

    def keys_copy(sl):
        return pltpu.make_async_copy(keyv_ref.at[sl], keys_ref.at[sl], ksem.at[sl])

    @pl.when((b == 0) & (s == 0))
    def _init_tables():
        i = lax.broadcasted_iota(I32, (CHUNK, CHUNK), 0)
        j = lax.broadcasted_iota(I32, (CHUNK, CHUNK), 1)
        diff = (i - j).astype(F32)
        fi = i.astype(F32)
        for h in range(RET_HEADS):
            lg = _LOG_GAMMA[h]
            dintra_ref[h] = jnp.where(i >= j, jnp.exp(lg * diff), 0.0) * _K_SCALE
            qdec_ref[h] = jnp.exp(lg * (fi + 1.0))
            kdec_ref[h] = jnp.exp(lg * (CHUNK - 1.0 - fi)) * _K_SCALE
        for g in range(SGU_GROUPS):
            wc_ref[g] = jnp.where(i >= j, sw_ref[g], 0.0).astype(BF16)
            btab_ref[g] = jnp.broadcast_to(sb_ref[g], (CHUNK, GROUP_DIM))
        ta = lax.broadcasted_iota(I32, (ts, ts), 0)
        tb = lax.broadcasted_iota(I32, (ts, ts), 1)
        before_ref[...] = jnp.where(ta < tb, 1.0, 0.0).astype(BF16)
        cnt_ref[...] = jnp.zeros_like(cnt_ref)
        blocks_ref[...] = jnp.zeros_like(blocks_ref)
        open_ref[...] = jnp.zeros_like(open_ref)
        nalloc_ref[...] = jnp.zeros_like(nalloc_ref)
        stage_ref[1] = jnp.zeros(stage_ref.shape[1:], TOKEN_DTYPE)
        kk = lax.broadcasted_iota(I32, (2, ts), 0)
        tt = lax.broadcasted_iota(I32, (2, ts), 1)
        keyv_ref[1] = placeholder_row0 + 2 * tt + kk
        pltpu.sync_copy(keyv_ref.at[1], keys_ref.at[1])

    @pl.when(li > 0)
    def _():
        keys_copy(slot_prev).wait()

    @pl.when(s == 0)
    def _reset_state():
        state_ref[...] = jnp.zeros_like(state_ref)

    n_chunks = ts // CHUNK
    pair = 2 * HEAD_DIM
    x = x_ref[...]
    h = _rms(x, gmix_ref[...]).astype(BF16)

    def project(sec, col0, width):
        cols = slice(col0, col0 + width)
        proj_ref[sec, :, cols] = jnp.dot(h, win_ref[:, sec * RET_WIDTH + col0:sec * RET_WIDTH + col0 + width],
                                         preferred_element_type=F32).astype(BF16)

    def spatial_gating(c):
        rows = slice(c * CHUNK, (c + 1) * CHUNK)
        for g in range(SGU_GROUPS):
            gs = slice(g * GROUP_DIM, (g + 1) * GROUP_DIM)
            u = _gelu(proj_ref[4, rows, gs].astype(F32))
            v = _gelu(proj_ref[5, rows, gs].astype(F32))
            vn = v * lax.rsqrt(jnp.mean(v * v, axis=-1, keepdims=True) + EPS) * sgn_ref[:, gs]
            sg = jnp.dot(wc_ref[g], vn.astype(BF16), preferred_element_type=F32) + btab_ref[g]
            mix_ref[rows, RET_WIDTH + g * GROUP_DIM:RET_WIDTH + (g + 1) * GROUP_DIM] = (u * sg).astype(BF16)

    def rotary_tables(c):
        ang_t = invf_ref[...] * pos_ref[c].astype(F32)
        cos_t = jnp.cos(ang_t)
        sin_t = jnp.sin(ang_t)
        rot_ref[0, c] = jnp.concatenate([cos_t, cos_t], axis=0).T
        rot_ref[1, c] = jnp.concatenate([-sin_t, sin_t], axis=0).T

    def retention(c, hd):
        rows = slice(c * CHUNK, (c + 1) * CHUNK)
        hs = slice(hd * HEAD_DIM, (hd + 1) * HEAD_DIM)
        cosf = rot_ref[0, c]
        sinf = rot_ref[1, c]
        q = proj_ref[0, rows, hs].astype(F32)
        k = proj_ref[1, rows, hs].astype(F32)
        vb = proj_ref[2, rows, hs]
        gate = proj_ref[3, rows, hs].astype(F32)
        qr = q * cosf + pltpu.roll(q, HEAD_DIM // 2, 1) * sinf
        kr = k * cosf + pltpu.roll(k, HEAD_DIM // 2, 1) * sinf
        scores = lax.dot_general(qr.astype(BF16), kr.astype(BF16), _NT,
                                 preferred_element_type=F32) * dintra_ref[hd]
        st = state_ref[hd]
        o = (jnp.dot(scores.astype(BF16), vb, preferred_element_type=F32)
             + jnp.dot((qr * qdec_ref[hd]).astype(BF16), st.astype(BF16), preferred_element_type=F32))
        kv = lax.dot_general((kr * kdec_ref[hd]).astype(BF16), vb, _TN, preferred_element_type=F32)
        state_ref[hd] = st * _CHUNK_DECAY[hd] + kv
        dv = o - jnp.mean(o, axis=-1, keepdims=True)
        on = dv * lax.rsqrt(jnp.mean(dv * dv, axis=-1, keepdims=True) + EPS) * gn_ref[:, hs]
        mix_ref[rows, hs] = (gate * _sigmoid(gate) * on).astype(BF16)

    def out_projection(col0, width):
        return jnp.dot(mix_ref[:, col0:col0 + width], wout_ref[col0:col0 + width, :], preferred_element_type=F32)

    project(4, 0, SGU_WIDTH)
    project(5, 0, SGU_WIDTH)
    scatter_group(0)
    for sec in range(4):
        project(sec, 0, pair)
    scatter_group(1)
    for c in range(n_chunks):
        spatial_gating(c)
    scatter_group(2)
    x1_ref[...] = x_ref[...] + out_projection(RET_WIDTH, SGU_WIDTH)
    for sec in range(4):
        project(sec, pair, pair)
    scatter_group(3)
    for c in range(n_chunks):
        rotary_tables(c)
    scatter_group(4)
    for c in range(n_chunks):
        retention(c, 0)
        retention(c, 1)
    scatter_group(5)
    x1_ref[...] += out_projection(0, pair)
    scatter_group(6)
    for c in range(n_chunks):
        retention(c, 2)
        retention(c, 3)
    scatter_group(7)
    x1 = x1_ref[...] + out_projection(pair, pair)
    x1_ref[...] = x1
    h2 = _rms(x1, gffn_ref[...])

    h2h = h2.astype(BF16)
    h2l = (h2 - h2h.astype(F32)).astype(BF16)
    wrh = wrh_ref[...]
    logits = (lax.dot_general(wrh, h2h, _NT, preferred_element_type=F32)
              + lax.dot_general(wrh, h2l, _NT, preferred_element_type=F32)
              + lax.dot_general(wrl_ref[...], h2h, _NT, preferred_element_type=F32)
              + br_ref[...])
    el = logits[0:N_EXPERTS]
    gl = logits[N_EXPERTS:N_EXPERTS + N_GROUPS]
    rg = lax.broadcasted_iota(I32, (N_GROUPS, ts), 0).astype(F32)
    gmax = jnp.max(gl, axis=0, keepdims=True)
    gidx = jnp.min(jnp.where(gl == gmax, rg, float(N_GROUPS)), axis=0, keepdims=True)
    gweight = 1.0 / jnp.sum(jnp.exp(gl - gmax), axis=0, keepdims=True)
    re_i = lax.broadcasted_iota(I32, (N_EXPERTS, ts), 0)
    re = re_i.astype(F32)
    in_group = (re_i // N_PER_GROUP).astype(F32) == gidx
    neg = -jnp.inf
    sel = jnp.where(in_group, el, neg)
    m1 = jnp.max(sel, axis=0, keepdims=True)
    i1 = jnp.min(jnp.where(sel == m1, re, float(N_EXPERTS)), axis=0, keepdims=True)
    sel2 = jnp.where(re == i1, neg, sel)
    m2 = jnp.max(sel2, axis=0, keepdims=True)
    i2 = jnp.min(jnp.where(sel2 == m2, re, float(N_EXPERTS)), axis=0, keepdims=True)
    e2 = jnp.exp(m2 - m1)
    w1 = 1.0 / (1.0 + e2)

    hit1 = re == i1
    hit2 = re == i2
    onehot = jnp.where(hit1, 1.0, jnp.where(hit2, 1.0, 0.0))
    seen = cnt_ref[:, 0:1] + jnp.dot(onehot.astype(BF16), before_ref[...], preferred_element_type=F32)
    inv_blk = 1.0 / EXPERT_TILE
    c0 = cnt_ref[:, 0:1]
    tile_cnt = jnp.sum(onehot, axis=1, keepdims=True)
    c1 = c0 + tile_cnt
    up0 = jnp.floor((c0 + (EXPERT_TILE - 1.0)) * inv_blk)
    n_new = jnp.floor((c1 + (EXPERT_TILE - 1.0)) * inv_blk) - up0
    ea = lax.broadcasted_iota(I32, (N_EXPERTS, N_EXPERTS), 0)
    eb = lax.broadcasted_iota(I32, (N_EXPERTS, N_EXPERTS), 1)
    earlier = jnp.where(eb < ea, 1.0, 0.0).astype(BF16)
    n_new_b = jnp.broadcast_to(n_new, (N_EXPERTS, LANES))
    new1 = nalloc_ref[:, 0:1] + jnp.dot(earlier, n_new_b.astype(BF16), preferred_element_type=F32)[:, 0:1]
    new2 = new1 + 1.0
    open_blk = open_ref[:, 0:1]
    rank_blk = jnp.floor(seen * inv_blk)
    which = rank_blk - up0
    block_of = jnp.where(which < 0.0, open_blk, jnp.where(which == 0.0, new1, new2))
    row_of = block_of * float(EXPERT_TILE) + (seen - rank_blk * float(EXPERT_TILE))
    keys = jnp.concatenate([jnp.sum(jnp.where(hit1, row_of, 0.0), axis=0, keepdims=True),
                            jnp.sum(jnp.where(hit2, row_of, 0.0), axis=0, keepdims=True)], axis=0).astype(I32)
    key_ref[...] = keys
    lane = lax.broadcasted_iota(I32, (N_EXPERTS, LANES), 1).astype(F32)
    blocks = blocks_ref[...]
    blocks = jnp.where(lane == up0, jnp.where(n_new >= 1.0, new1, blocks), blocks)
    blocks = jnp.where(lane == up0 + 1.0, jnp.where(n_new == 2.0, new2, blocks), blocks)
    blocks_ref[...] = blocks
    open_ref[...] = jnp.broadcast_to(jnp.where(n_new == 0.0, open_blk, jnp.where(n_new == 1.0, new1, new2)),
                                     open_ref.shape)
    nalloc_ref[...] += jnp.sum(n_new_b, axis=0, keepdims=True)
    cnt_ref[...] += jnp.broadcast_to(tile_cnt, cnt_ref.shape)
    gates_t = jnp.concatenate([gweight * w1, gweight * (e2 * w1), jnp.zeros((LANES - 2, ts), F32)], axis=0)
    gcol_ref[...] = gates_t.T

    @pl.when(li > 0)
    def _():
        wait_scatter(slot_cur)
    _store_token_major(stage_ref.at[slot_cur], h2)
    keyv_ref[slot_cur] = keys
    keys_copy(slot_cur).start()

    @pl.when(li == n_lin - 1)
    def _():
        keys_copy(slot_cur).wait()

        def last_rows(t, carry):
            for k in (0, 1):
                pltpu.make_async_copy(stage_ref.at[slot_cur, _token_rows(t)],
                                      xs_ref.at[_token_rows(keys_ref[slot_cur, k, t])],
                                      ssem.at[slot_cur]).start(priority=k)
            return carry
        lax.fori_loop(0, ts, last_rows, 0, unroll=4)

        total = cnt_ref[...]
        used = total - jnp.floor(total * inv_blk) * float(EXPERT_TILE)
        meta_lane = lax.broadcasted_iota(I32, (N_EXPERTS, LANES), 1)
        metav_ref[...] = jnp.where(
            meta_lane == 0, open_ref[...] * float(EXPERT_TILE) + used,
            jnp.where(meta_lane == 1, jnp.where(used > 0.0, float(EXPERT_TILE) - used, 0.0),
                      nalloc_ref[...])).astype(I32)
        pltpu.sync_copy(metav_ref, meta_ref)
        rot_ref[...] = jnp.zeros_like(rot_ref)
        zero_ref = rot_ref.reshape(rot_ref.shape[0] * rot_ref.shape[1] * CHUNK, LANES)
        zero_tokens = zero_ref.shape[0] // TOKEN_SUBLANES

        def piece(row, n_rows):
            assert n_rows <= zero_tokens
            return pltpu.make_async_copy(
                zero_ref.at[pl.ds(0, n_rows * TOKEN_SUBLANES)],
                xs_ref.at[pl.ds(pl.multiple_of(row * TOKEN_SUBLANES, TOKEN_SUBLANES), n_rows * TOKEN_SUBLANES)],
                fsem)

        def for_each_piece(action):
            for e in range(N_EXPERTS):
                row = meta_ref[e, 0]
                length = meta_ref[e, 1]
                for n_rows in _TAIL_PIECES:
                    @pl.when((length & n_rows) != 0)
                    def _():
                        action(piece(row, n_rows))
                    row = row + (length & n_rows)
            for blk in range(max_blocks - N_EXPERTS, max_blocks):
                @pl.when(blk >= meta_ref[0, 2])
                def _():
                    for part in range(0, EXPERT_TILE, zero_tokens):
                        action(piece(blk * EXPERT_TILE + part, min(zero_tokens, EXPERT_TILE - part)))

        for_each_piece(lambda cp: cp.start())
        for_each_piece(lambda cp: cp.wait())
        wait_scatter(slot_prev)
        wait_scatter(slot_cur)


def _attn_router(x, pos4, gmix, win, gn, sgn, sw, sb, wout, invf, gffn, wrh, wrl, br):
    bsz, seq, d = x.shape
    ts = ATTN_TILE
    ns = seq // ts
    t = bsz * seq
    const2 = lambda b, s: (0, 0)
    const3 = lambda b, s: (0, 0, 0)
    single = dict(pipeline_mode=pl.Buffered(1))
    in_specs = [
        pl.BlockSpec((None, ts, d), lambda b, s: (b, s, 0)),
        pl.BlockSpec((None, ts // CHUNK, 1, CHUNK), lambda b, s: (b, s, 0, 0)),
        pl.BlockSpec((1, d), const2),
        pl.BlockSpec(win.shape, const2, **single),
        pl.BlockSpec((1, RET_WIDTH), const2),
        pl.BlockSpec((1, SGU_WIDTH), const2),
        pl.BlockSpec(sw.shape, const3),
        pl.BlockSpec(sb.shape, const3),
        pl.BlockSpec(wout.shape, const2, **single),
        pl.BlockSpec(invf.shape, const2),
        pl.BlockSpec((1, d), const2),
        pl.BlockSpec(wrh.shape, const2),
        pl.BlockSpec(wrl.shape, const2),
        pl.BlockSpec(br.shape, const2),
    ]
    tok = lambda b, s: (b * ns + s, 0)
    out_specs = [
        pl.BlockSpec((ts, d), tok),
        pl.BlockSpec((2, ts), lambda b, s: (0, b * ns + s)),
        pl.BlockSpec((N_EXPERTS, LANES), const2),
        pl.BlockSpec((N_EXPERTS, LANES), const2),
        pl.BlockSpec((ts, LANES), tok),
        pl.BlockSpec(memory_space=pl.ANY),
    ]
    out_shape = [
        jax.ShapeDtypeStruct((t, d), F32),
        jax.ShapeDtypeStruct((2, t), I32),
        jax.ShapeDtypeStruct((N_EXPERTS, LANES), F32),
        jax.ShapeDtypeStruct((N_EXPERTS, LANES), F32),
        jax.ShapeDtypeStruct((t, LANES), F32),
        jax.ShapeDtypeStruct((_total_blocks(t) * EXPERT_TILE * TOKEN_SUBLANES, LANES), TOKEN_DTYPE),
    ]
    scratch = [
        pltpu.VMEM((N_SECTIONS, ts, RET_WIDTH), BF16),
        pltpu.VMEM((ts, RET_WIDTH + SGU_WIDTH), BF16),
        pltpu.VMEM((2, ts // CHUNK, CHUNK, HEAD_DIM), F32),
        pltpu.VMEM((RET_HEADS, HEAD_DIM, HEAD_DIM), F32),
        pltpu.VMEM((RET_HEADS, CHUNK, CHUNK), F32),
        pltpu.VMEM((RET_HEADS, CHUNK, HEAD_DIM), F32),
        pltpu.VMEM((RET_HEADS, CHUNK, HEAD_DIM), F32),
        pltpu.VMEM((SGU_GROUPS, CHUNK, CHUNK), BF16),
        pltpu.VMEM((SGU_GROUPS, CHUNK, GROUP_DIM), F32),
        pltpu.VMEM((ts, ts), BF16),
        pltpu.VMEM((N_EXPERTS, LANES), F32),
        pltpu.VMEM((N_EXPERTS, LANES), F32),
        pltpu.VMEM((2, ts * TOKEN_SUBLANES, LANES), TOKEN_DTYPE),
        pltpu.VMEM((2, 2, ts), I32),
        pltpu.SMEM((2, 2, ts), I32),
        pltpu.VMEM((N_EXPERTS, LANES), I32),
        pltpu.SMEM((N_EXPERTS, LANES), I32),
        pltpu.SemaphoreType.DMA((2,)),
        pltpu.SemaphoreType.DMA((2,)),
        pltpu.SemaphoreType.DMA,
    ]
    return pl.pallas_call(
        functools.partial(_attn_router_kernel, placeholder_row0=_max_blocks(t) * EXPERT_TILE,
                          max_blocks=_max_blocks(t)),
        grid=(bsz, ns),
        in_specs=in_specs,
        out_specs=out_specs,
        out_shape=out_shape,
        scratch_shapes=scratch,
        compiler_params=pltpu.CompilerParams(
            dimension_semantics=("arbitrary", "arbitrary"), vmem_limit_bytes=VMEM_LIMIT_BYTES),
        name="attn_router",
    )(x, pos4, gmix, win, gn, sgn, sw, sb, wout, invf, gffn, wrh, wrl, br)


def _expert_kernel(texp_ref, tin_ref, tout_ref, nt_ref, xs_ref, wg_ref, wu_ref, wd_ref, ys_ref,
                   xbuf_ref, wgb_ref, wub_ref, wdb_ref, sems):
    i = pl.program_id(0)
    n_steps = pl.num_programs(0)
    rows = xbuf_ref.shape[1]
    tm = rows // TOKEN_SUBLANES
    valid = i < nt_ref[0]
    ahead = INPUT_SLOTS - 1

    def fetch(step):
        src = xs_ref.at[pl.ds(pl.multiple_of(tin_ref[step] * rows, rows), rows)]
        slot = lax.rem(step, INPUT_SLOTS)
        return pltpu.make_async_copy(src, xbuf_ref.at[slot], sems.at[slot])

    @pl.when(i == 0)
    def _():
        for step in range(ahead):
            fetch(step).start()

    @pl.when(i + ahead < n_steps)
    def _():
        fetch(i + ahead).start()

    @pl.when(valid & ((i == 0) | (texp_ref[i] != texp_ref[jnp.maximum(i - 1, 0)])))
    def _():
        wgb_ref[...] = wg_ref[...].astype(BF16)
        wub_ref[...] = wu_ref[...].astype(BF16)
        wdb_ref[...] = wd_ref[...].astype(BF16)

    fetch(i).wait()

    @pl.when(valid)
    def _():
        xt = _load_token_major(xbuf_ref, (lax.rem(i, INPUT_SLOTS),), 0, tm).astype(BF16)
        a = jnp.dot(xt, wgb_ref[...], preferred_element_type=F32)
        bb = jnp.dot(xt, wub_ref[...], preferred_element_type=F32)
        act = a * _sigmoid(a) * bb
        o = jnp.dot(act.astype(BF16), wdb_ref[...], preferred_element_type=F32)
        _store_token_major(ys_ref, o)

    @pl.when(jnp.logical_not(valid))
    def _():
        ys_ref[...] = jnp.zeros_like(ys_ref)


def _experts(texp, tin, tout, ntiles, xs, wg, wu, wd):
    d = D_MODEL
    f = wg.shape[-1]
    rows = EXPERT_TILE * TOKEN_SUBLANES
    n_steps = texp.shape[0]
    assert n_steps >= INPUT_SLOTS
    grid_spec = pltpu.PrefetchScalarGridSpec(
        num_scalar_prefetch=4,
        grid=(n_steps,),
        in_specs=[
            pl.BlockSpec(memory_space=pl.ANY),
            pl.BlockSpec((None, d, f), lambda i, te, ti, to, n: (te[i], 0, 0)),
            pl.BlockSpec((None, d, f), lambda i, te, ti, to, n: (te[i], 0, 0)),
            pl.BlockSpec((None, f, d), lambda i, te, ti, to, n: (te[i], 0, 0)),
        ],
        out_specs=pl.BlockSpec((rows, LANES), lambda i, te, ti, to, n: (to[i], 0)),
        scratch_shapes=[pltpu.VMEM((INPUT_SLOTS, rows, LANES), TOKEN_DTYPE),
                        pltpu.VMEM((d, f), BF16), pltpu.VMEM((d, f), BF16), pltpu.VMEM((f, d), BF16),
                        pltpu.SemaphoreType.DMA((INPUT_SLOTS,))],
    )
    return pl.pallas_call(
        _expert_kernel,
        grid_spec=grid_spec,
        out_shape=jax.ShapeDtypeStruct(xs.shape, xs.dtype),
        compiler_params=pltpu.CompilerParams(
            dimension_semantics=("arbitrary",), vmem_limit_bytes=VMEM_LIMIT_BYTES),
        name="experts",
    )(texp, tin, tout, ntiles, xs, wg, wu, wd)


def _block_plan(cnt, blocks, n_steps):
    n_e = (cnt + EXPERT_TILE - 1) // EXPERT_TILE
    ends = jnp.cumsum(n_e)
    total = ends[-1]
    step = jnp.arange(n_steps, dtype=I32)
    p = jnp.minimum(step, total - 1)
    e = jnp.sum((p[:, None] >= ends[None, :]).astype(I32), axis=1)
    is_e = e[:, None] == jnp.arange(N_EXPERTS, dtype=I32)[None, :]
    k = p - jnp.sum(jnp.where(is_e, (ends - n_e)[None, :], 0), axis=1)
    rows = jnp.sum(jnp.where(is_e[:, :, None], blocks[None, :, :], 0), axis=1)
    blk_in = jnp.sum(jnp.where(k[:, None] == jnp.arange(LANES, dtype=I32)[None, :], rows, 0), axis=1)
    blk_out = jnp.where(step < total, blk_in, step)
    return e.astype(I32), blk_in.astype(I32), blk_out.astype(I32), total.reshape(1).astype(I32)


def _combine_kernel(p1_ref, p2_ref, p1n_ref, p2n_ref, gcol_ref, x1_ref, gfin_ref, os_ref, out_ref, buf_ref, sems):
    i = pl.program_id(0)
    n_steps = pl.num_programs(0)
    tm = x1_ref.shape[0]
    per_group = tm // DMA_GROUPS

    def row_copy(src_row, j, sl, k):
        return pltpu.make_async_copy(os_ref.at[_token_rows(src_row)], buf_ref.at[sl, k, _token_rows(j)], sems.at[sl])

    def start_group(r1_ref, r2_ref, sl, g):
        for j in range(g * per_group, (g + 1) * per_group):
            row_copy(r1_ref[j], j, sl, 0).start(priority=0)
            row_copy(r2_ref[j], j, sl, 1).start(priority=1)

    def wait_tile(sl):
        for k in (0, 1):
            pltpu.make_async_copy(os_ref.at[pl.ds(0, tm * TOKEN_SUBLANES)], buf_ref.at[sl, k], sems.at[sl]).wait()

    @pl.when(i == 0)
    def _():
        for g in range(DMA_GROUPS):
            start_group(p1_ref, p2_ref, 0, g)

    def step(sl):
        wait_tile(sl)
        for g in range(DMA_GROUPS):
            rows = slice(g * per_group, (g + 1) * per_group)
            o1 = _load_token_major(buf_ref, (sl, 0), g * per_group, per_group)
            o2 = _load_token_major(buf_ref, (sl, 1), g * per_group, per_group)
            gates = gcol_ref[rows, :]
            y = gates[:, 0:1] * o1 + gates[:, 1:2] * o2
            out_ref[rows, :] = _rms(x1_ref[rows, :] + y, gfin_ref[...])
            start_group(p1n_ref, p2n_ref, 1 - sl, g)

    for sl in (0, 1):
        @pl.when((i & 1) == sl)
        def _():
            step(sl)

    @pl.when(i == n_steps - 1)
    def _():
        wait_tile(1 - (i & 1))


def _combine(p1, p2, gcol, x1, gfin, osorted):
    t, d = x1.shape
    tm = MOVE_TILE
    n_steps = t // tm
    cur = lambda: pl.BlockSpec((tm,), lambda i: (i,), memory_space=pltpu.SMEM)
    nxt = lambda: pl.BlockSpec((tm,), lambda i: (jnp.minimum(i + 1, n_steps - 1),), memory_space=pltpu.SMEM)
    return pl.pallas_call(
        _combine_kernel,
        grid=(n_steps,),
        in_specs=[cur(), cur(), nxt(), nxt(),
                  pl.BlockSpec((tm, LANES), lambda i: (i, 0)),
                  pl.BlockSpec((tm, d), lambda i: (i, 0)),
                  pl.BlockSpec((1, d), lambda i: (0, 0)),
                  pl.BlockSpec(memory_space=pl.ANY)],
        out_specs=pl.BlockSpec((tm, d), lambda i: (i, 0)),
        out_shape=jax.ShapeDtypeStruct((t, d), F32),
        scratch_shapes=[pltpu.VMEM((2, 2, tm * TOKEN_SUBLANES, LANES), TOKEN_DTYPE), pltpu.SemaphoreType.DMA((2,))],
        compiler_params=pltpu.CompilerParams(dimension_semantics=("arbitrary",)),
        name="combine",
    )(p1, p2, p1, p2, gcol, x1, gfin, osorted)


def kernel(x, positions, norm_mix_gain, w_in, ret_gn_gain, sgu_norm_gain, sgu_w, sgu_b, w_out, norm_ffn_gain,
           w_router_group, b_router_group, w_router_expert, b_router_expert, w_expert_gate_up, w_expert_up,
           w_expert_down, final_norm_gain):
    bsz, seq, d = x.shape
    assert w_in.shape[0] == 1, "single-layer block"
    assert d == D_MODEL, "the packed token-major layout is written for 1024 features"
    assert seq % ATTN_TILE == 0 and (2 * bsz * seq) % EXPERT_TILE == 0
    assert bsz * seq // EXPERT_TILE + 1 <= LANES, "an expert's block list must fit one row of the block table"
    t = bsz * seq
    half = HEAD_DIM // 2

    pos4 = positions.reshape(bsz, seq // CHUNK, 1, CHUNK)
    invf = (ROPE_BASE ** (-jnp.arange(half, dtype=F32) * 2.0 / HEAD_DIM)).reshape(half, 1)
    wr = jnp.concatenate([
        jnp.transpose(w_router_expert[0], (0, 2, 1)).reshape(N_EXPERTS, d),
        w_router_group[0].T,
        jnp.zeros((ROUTER_ROWS - N_EXPERTS - N_GROUPS, d), F32)], axis=0)
    wrh = wr.astype(BF16)
    wrl = (wr - wrh.astype(F32)).astype(BF16)
    br = jnp.concatenate([b_router_expert[0].reshape(N_EXPERTS), b_router_group[0],
                          jnp.zeros((ROUTER_ROWS - N_EXPERTS - N_GROUPS,), F32)]).reshape(ROUTER_ROWS, 1)

    x1, keys, cnt, blocks, gcol, xs = _attn_router(
        x, pos4, norm_mix_gain[0].reshape(1, d), w_in[0].astype(BF16), ret_gn_gain[0].reshape(1, RET_WIDTH),
        sgu_norm_gain[0].reshape(1, SGU_WIDTH), sgu_w[0], sgu_b[0].reshape(SGU_GROUPS, CHUNK, 1),
        w_out[0].astype(BF16), invf, norm_ffn_gain[0].reshape(1, d), wrh, wrl, br)

    texp, tin, tout, nalloc = _block_plan(cnt[:, 0].astype(I32), blocks.astype(I32), _total_blocks(t))
    ff = w_expert_gate_up.shape[-1]
    ys = _experts(texp, tin, tout, nalloc, xs,
                  w_expert_gate_up[0].reshape(N_EXPERTS, d, ff),
                  w_expert_up[0].reshape(N_EXPERTS, d, ff),
                  w_expert_down[0].reshape(N_EXPERTS, ff, d))

    out = _combine(keys[0], keys[1], gcol, x1, final_norm_gain.reshape(1, d), ys)
    return out.reshape(bsz, seq, d)
```
